```python
import math
import jax, jax.numpy as jnp
from jax import lax
import numpy as np

D_MODEL = 1024
BATCH = 8
SEQ = 4096
DEPTH = 4

GRID_W = 64
CTX_LEN = 256
D_MIX = D_MODEL
CONV_W = D_MIX // 4
CONV_GROUPS = 4
CONV_K = 31
GLA_W = D_MIX // 4
GLA_HEADS = 4
GLA_DV = GLA_W // GLA_HEADS
GLA_DK = GLA_DV // 2
GLA_RANK = 16
GLA_GATE_NORM = 16.0
GLA_CHUNK = 64
ATT_W = D_MIX - CONV_W - GLA_W
HEAD_DIM = 64
N_HEADS = ATT_W // HEAD_DIM
N_KV_HEADS = 2
Q_BLOCK = 128
ROPE_BASE = 10000.0
ROT_FREQS = HEAD_DIM // 4
D_FF = 2816
N_MOD = 9
EPS = 1e-6
IN_SIZES = (CONV_W, CONV_W, GLA_HEADS * GLA_DK, GLA_HEADS * GLA_DK, GLA_W, GLA_W,
            GLA_RANK, GLA_RANK, ATT_W, N_KV_HEADS * HEAD_DIM, N_KV_HEADS * HEAD_DIM)
IN_WIDTH = sum(IN_SIZES)

kernel_name = "hymba_style_conv_gla_gqa_macaron_dit"


def rms_norm(x, g):
    xf = x.astype(jnp.float32)
    y = xf * lax.rsqrt(jnp.mean(xf * xf, axis=-1, keepdims=True) + EPS)
    return (y * g.astype(jnp.float32)).astype(x.dtype)


def pre(x, mod, j, g):
    return rms_norm(x, g) * (1 + mod[:, :, 3 * j + 1]) + mod[:, :, 3 * j]


def half_ffn(x, mod, j, g, w1, w2):
    gu = pre(x, mod, j, g) @ w1
    a, u = jnp.split(gu, 2, axis=-1)
    return x + 0.5 * mod[:, :, 3 * j + 2] * ((jax.nn.silu(a) * u) @ w2)


def split_proj(p):
    idx = np.cumsum(np.array(IN_SIZES))[:-1].tolist()
    return jnp.split(p, idx, axis=-1)


def conv_module(a, gate, w_dw, b_dw, g, b):
    h = a * jax.nn.sigmoid(gate)
    h = lax.conv_general_dilated(h, w_dw[:, None, :].astype(h.dtype), window_strides=(1,),
                                 padding=[(CONV_K // 2, CONV_K // 2)],
                                 dimension_numbers=('NWC', 'WIO', 'NWC'),
                                 feature_group_count=CONV_W) + b_dw
    B, T, C = h.shape
    hf = h.astype(jnp.float32).reshape(B, T, CONV_GROUPS, C // CONV_GROUPS)
    mu = jnp.mean(hf, axis=-1, keepdims=True)
    var = jnp.mean(jnp.square(hf - mu), axis=-1, keepdims=True)
    hf = ((hf - mu) * lax.rsqrt(var + EPS)).reshape(B, T, C) * g + b
    return jax.nn.silu(hf).astype(a.dtype)


def gla_inputs(p, w_gate, b_gate):
    B, T, _ = p[2].shape
    f = lambda t, d: t.astype(jnp.float32).reshape(B, T, GLA_HEADS, d)
    q = f(p[2], GLA_DK) * (GLA_DK ** -0.5)
    k = f(p[3], GLA_DK)
    v = f(p[4], GLA_DV)
    la_f = f(jax.nn.log_sigmoid(p[6].astype(jnp.float32) @ w_gate[0].astype(jnp.float32)
                                 + b_gate[0].astype(jnp.float32)) / GLA_GATE_NORM, GLA_DK)
    la_b = f(jax.nn.log_sigmoid(p[7].astype(jnp.float32) @ w_gate[1].astype(jnp.float32)
                                 + b_gate[1].astype(jnp.float32)) / GLA_GATE_NORM, GLA_DK)
    return q, k, v, la_f, la_b


def gla_chunked(q, k, v, log_a, s0):
    B, T, H, DK = q.shape
    DV = v.shape[-1]
    C = GLA_CHUNK
    N = T // C
    q = q.reshape(B, N, C, H, DK)
    k = k.reshape(B, N, C, H, DK)
    v = v.reshape(B, N, C, H, DV)
    bcum = jnp.cumsum(log_a.reshape(B, N, C, H, DK), axis=2)
    b_last = bcum[:, :, -1]
    q_e = q * jnp.exp(bcum)
    k_e = k * jnp.exp(-bcum)
    k_end = k * jnp.exp(b_last[:, :, None] - bcum)
    mask = jnp.tril(jnp.ones((C, C), dtype=bool))
    att = jnp.where(mask, jnp.einsum('bnihd,bnjhd->bnhij', q_e, k_e), 0.0)
    o = jnp.einsum('bnhij,bnjhv->bnihv', att, v)
    ds = jnp.einsum('bnjhd,bnjhv->bnhdv', k_end, v)
    decay = jnp.exp(b_last)

    def step(s, inp):
        dec, d = inp
        return dec[..., None] * s + d, s

    s_fin, s_start = lax.scan(step, s0, (jnp.moveaxis(decay, 1, 0), jnp.moveaxis(ds, 1, 0)))
    s_start = jnp.moveaxis(s_start, 0, 1)
    o = o + jnp.einsum('bnihd,bnhdv->bnihv', q_e, s_start)
    return o.reshape(B, T, H, DV), s_fin


def gla_bidir(q, k, v, la_f, la_b, s0_f, s0_b):
    flip = lambda t: jnp.flip(t, axis=1)
    o_f, s_f = gla_chunked(q, k, v, la_f, s0_f)
    o_b, s_b = gla_chunked(flip(q), flip(k), flip(v), flip(la_b), s0_b)
    return o_f + flip(o_b), s_f, s_b


def gla_output(o, r, g):
    B, T = r.shape[:2]
    on = rms_norm(o, g).reshape(B, T, GLA_W)
    return (on * jax.nn.silu(r.astype(jnp.float32))).astype(r.dtype)


def rope2d(x, cos, sin):
    B, T, H, HD = x.shape
    xr = x.reshape(B, T, H, 2, 2, HD // 4)
    x1, x2 = xr[..., 0, :], xr[..., 1, :]
    c, s = cos[:, None], sin[:, None]
    out = jnp.stack([x1 * c - x2 * s, x2 * c + x1 * s], axis=-2)
    return out.reshape(B, T, H, HD).astype(x.dtype)


def group_q(q):
    B, T, H, HD = q.shape
    return (q * (HD ** -0.5)).reshape(B, T, N_KV_HEADS, H // N_KV_HEADS, HD)


def attend(qi, k, v):
    s = jnp.einsum('bqkgd,bskd->bkgqs', qi, k).astype(jnp.float32)
    p = jax.nn.softmax(s, axis=-1).astype(v.dtype)
    return jnp.einsum('bkgqs,bskd->bqkgd', p, v)


def attention_latent(q, k_all, v_all):
    B, S = q.shape[:2]
    nb = S // Q_BLOCK
    qg = group_q(q)
    qb = jnp.moveaxis(qg.reshape(B, nb, Q_BLOCK, *qg.shape[2:]), 1, 0)
    o = lax.map(lambda qi: attend(qi, k_all, v_all), qb)
    return jnp.moveaxis(o, 0, 1).reshape(B, S, ATT_W)


def token_mix(hx, hc, cos, sin, w_in, w_out, w_dw, b_dw, cn_g, cn_b, w_gate, b_gate,
              gla_g, qn_g, kn_g, last):
    B, S, _ = hx.shape
    L = hc.shape[1]
    px = split_proj(hx @ w_in)
    pc = split_proj(hc @ w_in)
    conv_x = conv_module(px[0], px[1], w_dw, b_dw, cn_g, cn_b)
    qx, kx, vx, lfx, lbx = gla_inputs(px, w_gate, b_gate)
    qc, kc, vc, lfc, lbc = gla_inputs(pc, w_gate, b_gate)
    zeros = jnp.zeros((B, GLA_HEADS, GLA_DK, GLA_DV), jnp.float32)
    o_c, s_f, s_b = gla_bidir(qc, kc, vc, lfc, lbc, zeros, zeros)
    o_x, _, _ = gla_bidir(qx, kx, vx, lfx, lbx, s_f, s_b)
    gla_x = gla_output(o_x, px[5], gla_g)
    aqx = rope2d(rms_norm(px[8].reshape(B, S, N_HEADS, HEAD_DIM), qn_g), cos, sin)
    akx = rope2d(rms_norm(px[9].reshape(B, S, N_KV_HEADS, HEAD_DIM), kn_g), cos, sin)
    avx = px[10].reshape(B, S, N_KV_HEADS, HEAD_DIM)
    akc = rms_norm(pc[9].reshape(B, L, N_KV_HEADS, HEAD_DIM), kn_g)
    avc = pc[10].reshape(B, L, N_KV_HEADS, HEAD_DIM)
    k_all = jnp.concatenate([akc, akx], axis=1)
    v_all = jnp.concatenate([avc, avx], axis=1)
    att_x = attention_latent(aqx, k_all, v_all)
    y_x = jnp.concatenate([conv_x, gla_x, att_x], axis=-1) @ w_out
    if last:
        return y_x, None
    conv_c = conv_module(pc[0], pc[1], w_dw, b_dw, cn_g, cn_b)
    gla_c = gla_output(o_c, pc[5], gla_g)
    aqc = rms_norm(pc[8].reshape(B, L, N_HEADS, HEAD_DIM), qn_g)
    att_c = attend(group_q(aqc), akc, avc).reshape(B, L, ATT_W)
    y_c = jnp.concatenate([conv_c, gla_c, att_c], axis=-1) @ w_out
    return y_x, y_c


def layer(x, ctx, mod_x, mod_c, cos, sin, g_norm, w_ffn_in, w_ffn_out, w_in, w_out, w_dw, b_dw,
          cn_g, cn_b, w_gate, b_gate, gla_g, qn_g, kn_g, last):
    x = half_ffn(x, mod_x, 0, g_norm[0], w_ffn_in[0], w_ffn_out[0])
    ctx = half_ffn(ctx, mod_c, 0, g_norm[0], w_ffn_in[0], w_ffn_out[0])
    hx = pre(x, mod_x, 1, g_norm[1])
    hc = pre(ctx, mod_c, 1, g_norm[1])
    y_x, y_c = token_mix(hx, hc, cos, sin, w_in, w_out, w_dw, b_dw, cn_g, cn_b, w_gate, b_gate,
                         gla_g, qn_g, kn_g, last)
    x = x + mod_x[:, :, 5] * y_x
    x = half_ffn(x, mod_x, 2, g_norm[2], w_ffn_in[1], w_ffn_out[1])
    if not last:
        ctx = ctx + mod_c[:, :, 5] * y_c
        ctx = half_ffn(ctx, mod_c, 2, g_norm[2], w_ffn_in[1], w_ffn_out[1])
    return x, ctx


def setup_inputs(seed: int = 0) -> dict:
    key = jax.random.key(seed)
    ks = jax.random.split(key, 20)
    nrm = lambda k, shape, s: jax.random.normal(k, shape, jnp.float32) * s
    D = D_MODEL
    return {
        "x": nrm(ks[0], (BATCH, SEQ, D), 1.0),
        "c": nrm(ks[1], (BATCH, D), 1.0),
        "ctx": nrm(ks[2], (BATCH, CTX_LEN, D), 1.0),
        "c_ctx": nrm(ks[3], (D,), 1.0),
        "w_ada": nrm(ks[4], (DEPTH, D, N_MOD * D), 0.5 * D ** -0.5),
        "b_ada": nrm(ks[5], (DEPTH, N_MOD * D), 0.02),
        "g_norm": 1.0 + nrm(ks[6], (DEPTH, 3, D), 0.02),
        "w_ffn_in": nrm(ks[7], (DEPTH, 2, D, 2 * D_FF), D ** -0.5),
        "w_ffn_out": nrm(ks[8], (DEPTH, 2, D_FF, D), D_FF ** -0.5),
        "w_in": nrm(ks[9], (DEPTH, D, IN_WIDTH), D ** -0.5),
        "w_out": nrm(ks[10], (DEPTH, D_MIX, D), D_MIX ** -0.5),
        "w_dw": nrm(ks[11], (DEPTH, CONV_K, CONV_W), CONV_K ** -0.5),
        "b_dw": nrm(ks[12], (DEPTH, CONV_W), 0.02),
        "conv_norm_g": 1.0 + nrm(ks[13], (DEPTH, CONV_W), 0.02),
        "conv_norm_b": nrm(ks[14], (DEPTH, CONV_W), 0.02),
        "w_gla_gate": nrm(ks[15], (DEPTH, 2, GLA_RANK, GLA_HEADS * GLA_DK), GLA_RANK ** -0.5),
        "b_gla_gate": nrm(ks[16], (DEPTH, 2, GLA_HEADS * GLA_DK), 0.1),
        "gla_norm_g": 1.0 + nrm(ks[17], (DEPTH, GLA_HEADS, GLA_DV), 0.02),
        "q_norm_g": 1.0 + nrm(ks[18], (DEPTH, HEAD_DIM), 0.02),
        "k_norm_g": 1.0 + nrm(ks[19], (DEPTH, HEAD_DIM), 0.02),
    }


def reference(x, c, ctx, c_ctx, w_ada, b_ada, g_norm, w_ffn_in, w_ffn_out, w_in, w_out, w_dw, b_dw,
              conv_norm_g, conv_norm_b, w_gla_gate, b_gla_gate, gla_norm_g, q_norm_g, k_norm_g):
    B, S, D = x.shape
    rows_n = S // GRID_W
    row = jnp.repeat(jnp.arange(rows_n), GRID_W).astype(jnp.float32)
    col = jnp.tile(jnp.arange(GRID_W), rows_n).astype(jnp.float32)
    freqs = ROPE_BASE ** (-jnp.arange(ROT_FREQS, dtype=jnp.float32) / ROT_FREQS)
    ang = jnp.stack([row[:, None] * freqs, col[:, None] * freqs], axis=1)
    cos, sin = jnp.cos(ang), jnp.sin(ang)
    sc = jax.nn.silu(c)
    scc = jax.nn.silu(c_ctx)
    for i in range(DEPTH):
        last = i == DEPTH - 1
        mod_x = (sc @ w_ada[i] + b_ada[i]).reshape(B, 1, N_MOD, D)
        mod_c = (scc @ w_ada[i] + b_ada[i]).reshape(1, 1, N_MOD, D)
        x, ctx = layer(x, ctx, mod_x, mod_c, cos, sin, g_norm[i], w_ffn_in[i], w_ffn_out[i],
                       w_in[i], w_out[i], w_dw[i], b_dw[i], conv_norm_g[i], conv_norm_b[i],
                       w_gla_gate[i], b_gla_gate[i], gla_norm_g[i], q_norm_g[i], k_norm_g[i], last)
    return x
```

```python
import functools

import numpy as np
import jax
import jax.numpy as jnp
from jax import lax
from jax.experimental import pallas as pl
from jax.experimental.pallas import tpu as pltpu

F32 = jnp.float32
BF16 = jnp.bfloat16

CONV_GROUPS = 4
CONV_K = 31
GLA_HEADS = 4
GLA_RANK = 16
GLA_GATE_NORM = 16.0
GLA_CHUNK = 64
HEAD_DIM = 64
N_KV_HEADS = 2
GRID_W = 64
ROPE_BASE = 10000.0
N_MOD = 9
EPS = 1e-6

LANES = 128
SUBLANES = 8
VMEM_LIMIT_BYTES = 56 * 1024 * 1024

FFN_CHUNK = 256
TOKEN_TILE = 512
CONV_ROWS = 32
CONV_HALO = 16
GLA_BLOCK = 512
ATT_TQ = 256
ATT_TK = 512


def _cparams(n_axes):
    return pltpu.CompilerParams(dimension_semantics=("arbitrary",) * n_axes,
                                vmem_limit_bytes=VMEM_LIMIT_BYTES)


def _const_spec(shape):
    nd = len(shape)
    return pl.BlockSpec(shape, lambda *_: (0,) * nd, pipeline_mode=pl.Buffered(1))


def _dot(a, b):
    return jnp.dot(a, b, preferred_element_type=F32)


def _dot_nt(a, b):
    return lax.dot_general(a, b, (((1,), (1,)), ((), ())), preferred_element_type=F32)


def _dot_tn(a, b):
    return lax.dot_general(a, b, (((0,), (0,)), ((), ())), preferred_element_type=F32)


def _sigmoid(x):
    return 1.0 / (1.0 + jnp.exp(-x))


def _silu(x):
    return x * _sigmoid(x)


def _split_dot(v, m):
    hi = v.astype(BF16)
    lo = (v - hi.astype(F32)).astype(BF16)
    return _dot(hi, m) + _dot(lo, m)


def _split3_dot_left(m, v):
    hi = v.astype(BF16)
    r1 = v - hi.astype(F32)
    mid = r1.astype(BF16)
    lo = (r1 - mid.astype(F32)).astype(BF16)
    return _dot(m, hi) + _dot(m, mid) + _dot(m, lo)


def _pre(x, g, shift, scale):
    r = lax.rsqrt(jnp.mean(x * x, axis=-1, keepdims=True) + EPS)
    return (x * r * g) * (1.0 + scale) + shift


def _group_mean_matrix(width, group):
    idx = np.arange(width) // group
    return jnp.asarray((idx[:, None] == idx[None, :]).astype(np.float32) / group, dtype=BF16)


def _rope_partner(x):
    n = x.shape[-1]
    lane = lax.broadcasted_iota(jnp.int32, x.shape, x.ndim - 1)
    up = pltpu.roll(x, n - 16, axis=x.ndim - 1)
    dn = pltpu.roll(x, 16, axis=x.ndim - 1)
    return jnp.where((lane % 32) < 16, up, dn)


def _mod_kernel(c_ref, w_ref, b_ref, o_ref):
    sc = _silu(c_ref[...]).astype(BF16)
    o_ref[...] = _dot(sc, w_ref[...].astype(BF16)) + b_ref[...]


def _mod_call(cc, w_ada, b_ada):
    depth, d, _ = w_ada.shape
    rows = cc.shape[0]
    return pl.pallas_call(
        _mod_kernel,
        grid=(depth, N_MOD),
        in_specs=[pl.BlockSpec((rows, d), lambda i, j: (0, 0)),
                  pl.BlockSpec((None, d, d), lambda i, j: (i, 0, j)),
                  pl.BlockSpec((None, 1, d), lambda i, j: (i, 0, j))],
        out_specs=pl.BlockSpec((None, None, rows, d), lambda i, j: (i, j, 0, 0)),
        out_shape=jax.ShapeDtypeStruct((depth, N_MOD, rows, d), F32),
        compiler_params=_cparams(2),
        name="mod",
    )(cc, w_ada, b_ada.reshape(depth, 1, N_MOD * d))


def _ffn_kernel(j, n_chunks, x_ref, mod_ref, g_ref, w1a_ref, w1u_ref, w2_ref, o_ref, acc_ref):
    x = x_ref[...]
    shift = mod_ref[3 * j:3 * j + 1, :]
    scale = mod_ref[3 * j + 1:3 * j + 2, :]
    gate = mod_ref[3 * j + 2:3 * j + 3, :]
    h = _pre(x, g_ref[...], shift, scale).astype(BF16)
    acc_ref[...] = jnp.zeros_like(acc_ref)

    def body(c, carry):
        a = _dot(h, w1a_ref[c])
        u = _dot(h, w1u_ref[c])
        act = (_silu(a) * u).astype(BF16)
        acc_ref[...] += _dot(act, w2_ref[c])
        return carry

    lax.fori_loop(0, n_chunks, body, 0)
    o_ref[...] = x + (0.5 * gate) * acc_ref[...]


def _ffn_call(x, mod_l, mod_row, g, w1a, w1u, w2, j, tm):
    b, t, d = x.shape
    n_chunks = w1a.shape[0]
    return pl.pallas_call(
        functools.partial(_ffn_kernel, j, n_chunks),
        grid=(b, t // tm),
        in_specs=[pl.BlockSpec((None, tm, d), lambda bi, i: (bi, i, 0)),
                  pl.BlockSpec((None, N_MOD, d), lambda bi, i: (mod_row(bi), 0, 0)),
                  _const_spec(g.shape), _const_spec(w1a.shape), _const_spec(w1u.shape),
                  _const_spec(w2.shape)],
        out_specs=pl.BlockSpec((None, tm, d), lambda bi, i: (bi, i, 0)),
        out_shape=jax.ShapeDtypeStruct(x.shape, F32),
        scratch_shapes=[pltpu.VMEM((tm, d), F32)],
        compiler_params=_cparams(2),
        name="ffn",
    )(x, mod_l, g, w1a, w1u, w2)


def _inproj_kernel(x_ref, mod_ref, g_ref, wc_ref, wg_ref, wa_ref, pc_ref, pg_ref, pa_ref):
    h = _pre(x_ref[...], g_ref[...], mod_ref[3:4, :], mod_ref[4:5, :]).astype(BF16)
    pc_ref[...] = _dot(h, wc_ref[...])
    pg_ref[...] = _dot(h, wg_ref[...])
    pa_ref[...] = _dot(h, wa_ref[...])


def _inproj_call(x, mod_l, mod_row, g, wc, wg, wa, tm):
    b, t, d = x.shape
    tok = lambda bi, i: (bi, i, 0)
    return pl.pallas_call(
        _inproj_kernel,
        grid=(b, t // tm),
        in_specs=[pl.BlockSpec((None, tm, d), tok),
                  pl.BlockSpec((None, N_MOD, d), lambda bi, i: (mod_row(bi), 0, 0)),
                  _const_spec(g.shape), _const_spec(wc.shape), _const_spec(wg.shape),
                  _const_spec(wa.shape)],
        out_specs=[pl.BlockSpec((None, tm, wc.shape[1]), tok),
                   pl.BlockSpec((None, tm, wg.shape[1]), tok),
                   pl.BlockSpec((None, tm, wa.shape[1]), tok)],
        out_shape=[jax.ShapeDtypeStruct((b, t, wc.shape[1]), F32),
                   jax.ShapeDtypeStruct((b, t, wg.shape[1]), F32),
                   jax.ShapeDtypeStruct((b, t, wa.shape[1]), F32)],
        compiler_params=_cparams(2),
        name="inproj",
    )(x, mod_l, g, wc, wg, wa)


def _conv_kernel(t, cw, pc_ref, wdw_ref, bdw_ref, cg_ref, cb_ref, gm_ref, o_ref, hbuf):
    r = CONV_ROWS
    halo = CONV_HALO
    hbuf[0:halo, :] = jnp.zeros((halo, cw), F32)
    hbuf[t + halo:t + 2 * halo, :] = jnp.zeros((halo, cw), F32)

    def glu(i, carry):
        r0 = pl.multiple_of(i * r, r)
        a = pc_ref[pl.ds(r0, r), 0:cw]
        gt = pc_ref[pl.ds(r0, r), cw:2 * cw]
        hbuf[pl.ds(r0 + halo, r), :] = a * _sigmoid(gt)
        return carry

    lax.fori_loop(0, t // r, glu, 0)
    gm = gm_ref[...]

    def tile(i, carry):
        r0 = pl.multiple_of(i * r, r)
        win = hbuf[pl.ds(r0, r + 2 * halo), :]
        acc = jnp.zeros((r, cw), F32)
        for res in range(SUBLANES):
            wr = win if res == 0 else pltpu.roll(win, r + 2 * halo - res, axis=0)
            for a in range(2 * halo // SUBLANES):
                s = SUBLANES * a + res
                if 1 <= s <= CONV_K:
                    acc = acc + wr[SUBLANES * a:SUBLANES * a + r, :] * wdw_ref[s - 1:s, :]
        h2 = acc + bdw_ref[...]
        mu = _split_dot(h2, gm)
        dlt = h2 - mu
        var = _split_dot(dlt * dlt, gm)
        y = dlt * lax.rsqrt(var + EPS) * cg_ref[...] + cb_ref[...]
        o_ref[pl.ds(r0, r), :] = _silu(y)
        return carry

    lax.fori_loop(0, t // r, tile, 0)


def _conv_call(pc, w_dw, b_dw, cn_g, cn_b, gm):
    b, t, two_cw = pc.shape
    cw = two_cw // 2
    return pl.pallas_call(
        functools.partial(_conv_kernel, t, cw),
        grid=(b,),
        in_specs=[pl.BlockSpec((None, t, two_cw), lambda bi: (bi, 0, 0)),
                  _const_spec(w_dw.shape), _const_spec(b_dw.shape), _const_spec(cn_g.shape),
                  _const_spec(cn_b.shape), _const_spec(gm.shape)],
        out_specs=pl.BlockSpec((None, t, cw), lambda bi: (bi, 0, 0)),
        out_shape=jax.ShapeDtypeStruct((b, t, cw), F32),
        scratch_shapes=[pltpu.VMEM((t + 2 * CONV_HALO, cw), F32)],
        compiler_params=_cparams(1),
        name="conv",
    )(pc, w_dw, b_dw, cn_g, cn_b, gm)


def _gla_kernel(n_ch, dkw, dvw, pg_ref, wg_ref, bg_ref, s0_ref, o_ref, sfin_ref, s_scr, la_scr):
    c = GLA_CHUNK
    d = pl.program_id(1)
    i = pl.program_id(2)
    fwd = d == 0

    @pl.when(i == 0)
    def _():
        s_scr[...] = s0_ref[...]

    lr_off = 2 * dkw + 2 * dvw
    z = _dot(pg_ref[:, lr_off:lr_off + LANES], wg_ref[...]) + bg_ref[...]
    la_scr[...] = -(jnp.maximum(-z, 0.0) + jnp.log(1.0 + jnp.exp(-jnp.abs(z)))) * (1.0 / GLA_GATE_NORM)

    ri = lax.broadcasted_iota(jnp.int32, (c, c), 0)
    ci = lax.broadcasted_iota(jnp.int32, (c, c), 1)
    tri = jnp.where(fwd, ci - ri, ri - ci) <= 0
    tri_m = jnp.where(tri, 1.0, 0.0).astype(BF16)
    rih = lax.broadcasted_iota(jnp.int32, (c, GLA_HEADS * c), 0)
    cih = lax.broadcasted_iota(jnp.int32, (c, GLA_HEADS * c), 1) % c
    tri_h = jnp.where(fwd, cih - rih, rih - cih) <= 0
    dk = dkw // GLA_HEADS
    dv = dvw // GLA_HEADS
    klane = lax.broadcasted_iota(jnp.int32, (c, dkw), 1) // dk
    vlane = lax.broadcasted_iota(jnp.int32, (c, dvw), 1) // dv
    srow = lax.broadcasted_iota(jnp.int32, (dvw, dkw), 0) // dv
    scol = lax.broadcasted_iota(jnp.int32, (dvw, dkw), 1) // dk
    smask = srow == scol

    def chunk(jj, carry):
        cidx = jnp.where(fwd, jj, n_ch - 1 - jj)
        rows = pl.ds(pl.multiple_of(cidx * c, c), c)
        q = pg_ref[rows, 0:dkw] * (dk ** -0.5)
        k = pg_ref[rows, dkw:2 * dkw]
        v = pg_ref[rows, 2 * dkw:2 * dkw + dvw]
        la = la_scr[rows, :]
        bcum = _split3_dot_left(tri_m, la)
        btot = jnp.sum(la, axis=0, keepdims=True)
        q_e = (q * jnp.exp(bcum)).astype(BF16)
        k_e = k * jnp.exp(-bcum)
        k_end = (k * jnp.exp(btot - bcum)).astype(BF16)
        kblk = jnp.concatenate([jnp.where(klane == h, k_e, 0.0) for h in range(GLA_HEADS)],
                               axis=0).astype(BF16)
        vblk = jnp.concatenate([jnp.where(vlane == h, v, 0.0) for h in range(GLA_HEADS)],
                               axis=0).astype(BF16)
        att = jnp.where(tri_h, _dot_nt(q_e, kblk), 0.0).astype(BF16)
        s_t = s_scr[...]
        o_ref[rows, :] = _dot(att, vblk) + _dot_nt(q_e, s_t.astype(BF16))
        upd = _dot_tn(v.astype(BF16), k_end)
        s_scr[...] = s_t * jnp.exp(btot) + jnp.where(smask, upd, 0.0)
        return carry

    lax.fori_loop(0, n_ch, chunk, 0)

    @pl.when(i == pl.num_programs(2) - 1)
    def _():
        sfin_ref[...] = s_scr[...]


def _gla_call(pg, wg2, bg2, s0, dkw, dvw):
    b, t, w = pg.shape
    blk = min(GLA_BLOCK, t)
    nb = t // blk
    n_ch = blk // GLA_CHUNK
    seq = lambda bi, d, i: (bi, jnp.where(d == 0, i, nb - 1 - i), 0)
    return pl.pallas_call(
        functools.partial(_gla_kernel, n_ch, dkw, dvw),
        grid=(b, 2, nb),
        in_specs=[pl.BlockSpec((None, blk, w), seq),
                  pl.BlockSpec((None, LANES, dkw), lambda bi, d, i: (d, 0, 0)),
                  pl.BlockSpec((None, 1, dkw), lambda bi, d, i: (d, 0, 0)),
                  pl.BlockSpec((None, None, dvw, dkw), lambda bi, d, i: (bi, d, 0, 0))],
        out_specs=[pl.BlockSpec((None, None, blk, dvw),
                                lambda bi, d, i: (d, bi, jnp.where(d == 0, i, nb - 1 - i), 0)),
                   pl.BlockSpec((None, None, dvw, dkw), lambda bi, d, i: (bi, d, 0, 0))],
        out_shape=[jax.ShapeDtypeStruct((2, b, t, dvw), F32),
                   jax.ShapeDtypeStruct((b, 2, dvw, dkw), F32)],
        scratch_shapes=[pltpu.VMEM((dvw, dkw), F32), pltpu.VMEM((blk, dkw), F32)],
        compiler_params=_cparams(3),
        name="gla",
    )(pg, wg2, bg2, s0)


def _kvprep_kernel(kv_ref, cos_ref, sin_ref, g_ref, gm_ref, k_ref, v_ref):
    hd = HEAD_DIM
    kk = kv_ref[:, 0:LANES]
    vv = kv_ref[:, LANES:2 * LANES]
    ms = _split_dot(kk * kk, gm_ref[...])
    kn = kk * lax.rsqrt(ms + EPS) * g_ref[...]
    kr = kn * cos_ref[...] + _rope_partner(kn) * sin_ref[...]
    lane = lax.broadcasted_iota(jnp.int32, kr.shape, 1)
    lo = lane < hd
    kswap = pltpu.roll(kr, hd, axis=1)
    k0 = jnp.where(lo, kr, kswap).astype(BF16)
    k1 = jnp.where(lo, kswap, kr).astype(BF16)
    k_ref[...] = jnp.concatenate([k0, k0, k1, k1], axis=1)
    v0 = jnp.where(lo, vv, 1.0).astype(BF16)
    v1 = jnp.where(lo, pltpu.roll(vv, hd, axis=1), 1.0).astype(BF16)
    v_ref[...] = jnp.concatenate([v0, v1], axis=1)


def _kvprep_call(pa, cos_t, sin_t, kg, gm, tt):
    b, t, _ = pa.shape
    return pl.pallas_call(
        _kvprep_kernel,
        grid=(b, t // tt),
        in_specs=[pl.BlockSpec((None, tt, 2 * LANES), lambda bi, i: (bi, i, 2)),
                  pl.BlockSpec((tt, LANES), lambda bi, i: (i, 0)),
                  pl.BlockSpec((tt, LANES), lambda bi, i: (i, 0)),
                  _const_spec(kg.shape), _const_spec(gm.shape)],
        out_specs=[pl.BlockSpec((None, tt, 4 * LANES), lambda bi, i: (bi, i, 0)),
                   pl.BlockSpec((None, tt, 2 * LANES), lambda bi, i: (bi, i, 0))],
        out_shape=[jax.ShapeDtypeStruct((b, t, 4 * LANES), BF16),
                   jax.ShapeDtypeStruct((b, t, 2 * LANES), BF16)],
        compiler_params=_cparams(2),
        name="kvprep",
    )(pa, cos_t, sin_t, kg, gm)


def _flash_kernel(seg_lens, q_ref, cos_ref, sin_ref, g_ref, gm_ref, *rest):
    n_seg = len(seg_lens)
    kv_refs = rest[:2 * n_seg]
    o_ref, m_scr, acc_scr = rest[2 * n_seg:]
    hd = HEAD_DIM
    g_heads = 2 * LANES // hd
    q = q_ref[...]
    tq = q.shape[0]
    ms = _split_dot(q * q, gm_ref[...])
    qn = q * lax.rsqrt(ms + EPS) * g_ref[...]
    qr = (qn * cos_ref[...] + _rope_partner(qn) * sin_ref[...]) * (hd ** -0.5)
    lane_head = lax.broadcasted_iota(jnp.int32, qr.shape, 1) // hd
    qs = jnp.concatenate([jnp.where(lane_head == h, qr, 0.0) for h in range(g_heads)],
                         axis=0).astype(BF16)
    m_scr[...] = jnp.full(m_scr.shape, -jnp.inf, F32)
    acc_scr[...] = jnp.zeros_like(acc_scr)

    def step(k, v):
        s = _dot_nt(qs, k)
        m_prev = m_scr[...]
        m_new = jnp.maximum(m_prev, jnp.max(s, axis=1, keepdims=True))
        alpha = jnp.exp(m_prev - m_new)
        p = jnp.exp(s - m_new[:, 0:1]).astype(BF16)
        acc_scr[...] = acc_scr[...] * alpha + _dot(p, v)
        m_scr[...] = m_new

    for si in range(n_seg):
        k_ref, v_ref = kv_refs[2 * si], kv_refs[2 * si + 1]
        length = seg_lens[si]
        tk = min(ATT_TK, length)
        if length == tk:
            step(k_ref[...], v_ref[...])
        else:
            def body(ci, carry, k_ref=k_ref, v_ref=v_ref, tk=tk):
                rows = pl.ds(pl.multiple_of(ci * tk, tk), tk)
                step(k_ref[rows, :], v_ref[rows, :])
                return carry
            lax.fori_loop(0, length // tk, body, 0)

    acc = acc_scr[...]
    o_all = acc * pltpu.roll(1.0 / acc, hd, axis=1)
    lo = lax.broadcasted_iota(jnp.int32, (tq, LANES), 1) < hd
    pieces = []
    for pair in range(g_heads // 2):
        a = o_all[(2 * pair) * tq:(2 * pair + 1) * tq, :]
        bq = o_all[(2 * pair + 1) * tq:(2 * pair + 2) * tq, :]
        pieces.append(jnp.where(lo, a, pltpu.roll(bq, hd, axis=1)))
    o_ref[...] = jnp.concatenate(pieces, axis=1)


def _flash_call(pa, cos_q, sin_q, qg, gm, segs, tq):
    b, t, _ = pa.shape
    tq = min(tq, t)
    seg_lens = tuple(k.shape[1] for k, _ in segs)
    in_specs = [pl.BlockSpec((None, tq, 2 * LANES), lambda bi, j, i: (bi, i, j)),
                pl.BlockSpec((tq, 2 * LANES), lambda bi, j, i: (i, 0)),
                pl.BlockSpec((tq, 2 * LANES), lambda bi, j, i: (i, 0)),
                _const_spec(qg.shape), _const_spec(gm.shape)]
    args = [pa, cos_q, sin_q, qg, gm]
    for k, v in segs:
        in_specs.append(pl.BlockSpec((None, k.shape[1], 2 * LANES), lambda bi, j, i: (bi, 0, j)))
        in_specs.append(pl.BlockSpec((None, v.shape[1], LANES), lambda bi, j, i: (bi, 0, j)))
        args += [k, v]
    rows = (2 * LANES // HEAD_DIM) * tq
    return pl.pallas_call(
        functools.partial(_flash_kernel, seg_lens),
        grid=(b, N_KV_HEADS, t // tq),
        in_specs=in_specs,
        out_specs=pl.BlockSpec((None, tq, 2 * LANES), lambda bi, j, i: (bi, i, j)),
        out_shape=jax.ShapeDtypeStruct((b, t, N_KV_HEADS * 2 * LANES), F32),
        scratch_shapes=[pltpu.VMEM((rows, LANES), F32), pltpu.VMEM((rows, LANES), F32)],
        compiler_params=_cparams(3),
        name="flash",
    )(*args)


def _out_kernel(cw, dvw, x_ref, mod_ref, conv_ref, of_ref, ob_ref, r_ref, att_ref, gg_ref, gm_ref,
                wo_ref, o_ref):
    o = of_ref[...] + ob_ref[...]
    ms = _split_dot(o * o, gm_ref[...])
    gl = (o * lax.rsqrt(ms + EPS) * gg_ref[...]) * _silu(r_ref[...])
    y = (_dot(conv_ref[...].astype(BF16), wo_ref[0:cw, :])
         + _dot(gl.astype(BF16), wo_ref[cw:cw + dvw, :])
         + _dot(att_ref[...].astype(BF16), wo_ref[cw + dvw:, :]))
    o_ref[...] = x_ref[...] + mod_ref[5:6, :] * y


def _out_call(x, mod_l, mod_row, conv, o_gla, pg, att, gg, gm, wo, tm):
    b, t, d = x.shape
    cw = conv.shape[2]
    dvw = o_gla.shape[3]
    aw = att.shape[2]
    r_blk = (pg.shape[2] - dvw) // dvw
    tok = lambda bi, i: (bi, i, 0)
    del r_blk
    return pl.pallas_call(
        functools.partial(_out_kernel, cw, dvw),
        grid=(b, t // tm),
        in_specs=[pl.BlockSpec((None, tm, d), tok),
                  pl.BlockSpec((None, N_MOD, d), lambda bi, i: (mod_row(bi), 0, 0)),
                  pl.BlockSpec((None, tm, cw), tok),
                  pl.BlockSpec((None, None, tm, dvw), lambda bi, i: (0, bi, i, 0)),
                  pl.BlockSpec((None, None, tm, dvw), lambda bi, i: (1, bi, i, 0)),
                  pl.BlockSpec((None, tm, dvw), lambda bi, i: (bi, i, 2)),
                  pl.BlockSpec((None, tm, aw), tok),
                  _const_spec(gg.shape), _const_spec(gm.shape), _const_spec(wo.shape)],
        out_specs=pl.BlockSpec((None, tm, d), tok),
        out_shape=jax.ShapeDtypeStruct(x.shape, F32),
        compiler_params=_cparams(2),
        name="outproj",
    )(x, mod_l, conv, o_gla, o_gla, pg, att, gg, gm, wo)


def _rope_tables(s):
    nf = HEAD_DIM // 4
    rows_n = s // GRID_W
    row = jnp.repeat(jnp.arange(rows_n), GRID_W).astype(F32)
    col = jnp.tile(jnp.arange(GRID_W), rows_n).astype(F32)
    freqs = ROPE_BASE ** (-jnp.arange(nf, dtype=F32) / nf)
    ang_r = row[:, None] * freqs
    ang_c = col[:, None] * freqs
    cos = jnp.concatenate([jnp.cos(ang_r)] * 2 + [jnp.cos(ang_c)] * 2, axis=1)
    sin = jnp.concatenate([-jnp.sin(ang_r), jnp.sin(ang_r), -jnp.sin(ang_c), jnp.sin(ang_c)], axis=1)
    return cos, sin


def kernel(x, c, ctx, c_ctx, w_ada, b_ada, g_norm, w_ffn_in, w_ffn_out, w_in, w_out, w_dw, b_dw,
           conv_norm_g, conv_norm_b, w_gla_gate, b_gla_gate, gla_norm_g, q_norm_g, k_norm_g):
    b, s, d = x.shape
    l = ctx.shape[1]
    depth = w_ada.shape[0]
    d_ff = w_ffn_out.shape[2]
    cw = w_dw.shape[2]
    dkw = w_gla_gate.shape[3]
    dvw = gla_norm_g.shape[1] * gla_norm_g.shape[2]
    n_kv = N_KV_HEADS * HEAD_DIM
    aw = d - cw - dvw
    assert d_ff % FFN_CHUNK == 0 and s % TOKEN_TILE == 0 and l % GLA_CHUNK == 0
    assert dkw == LANES and n_kv == LANES and cw == 2 * LANES and dvw == 2 * LANES and aw == 4 * LANES

    rows = -(-(b + 1) // SUBLANES) * SUBLANES
    cc = jnp.zeros((rows, d), F32).at[:b].set(c).at[b].set(c_ctx)
    mod = jnp.transpose(_mod_call(cc, w_ada, b_ada), (0, 2, 1, 3))
    row_x = lambda bi: bi
    row_c = lambda bi: b

    cos64, sin64 = _rope_tables(s)
    cos_k, sin_k = jnp.tile(cos64, (1, 2)), jnp.tile(sin64, (1, 2))
    cos_q, sin_q = jnp.tile(cos64, (1, 4)), jnp.tile(sin64, (1, 4))
    one_k, zero_k = jnp.ones((l, LANES), F32), jnp.zeros((l, LANES), F32)
    one_q, zero_q = jnp.ones((l, 2 * LANES), F32), jnp.zeros((l, 2 * LANES), F32)

    gm64_128 = _group_mean_matrix(LANES, HEAD_DIM)
    gm64_256 = _group_mean_matrix(2 * LANES, HEAD_DIM)
    gm_conv = _group_mean_matrix(cw, cw // CONV_GROUPS)
    gm_gla = _group_mean_matrix(dvw, dvw // GLA_HEADS)

    sizes = (cw, cw, dkw, dkw, dvw, dvw, GLA_RANK, GLA_RANK, aw, n_kv, n_kv)
    offs = np.concatenate([[0], np.cumsum(sizes)])
    n_chunks = d_ff // FFN_CHUNK
    tm_c = min(TOKEN_TILE, l)

    for i in range(depth):
        last = i == depth - 1
        mod_l = mod[i]
        g_l = g_norm[i].reshape(3, 1, d)
        w1 = w_ffn_in[i].astype(BF16)
        w1a = w1[:, :, :d_ff].reshape(2, d, n_chunks, FFN_CHUNK).transpose(0, 2, 1, 3)
        w1u = w1[:, :, d_ff:].reshape(2, d, n_chunks, FFN_CHUNK).transpose(0, 2, 1, 3)
        w2 = w_ffn_out[i].astype(BF16).reshape(2, n_chunks, FFN_CHUNK, d)
        wi = w_in[i].astype(BF16)
        wc = wi[:, offs[0]:offs[2]]
        wgl = jnp.concatenate([wi[:, offs[2]:offs[8]],
                               jnp.zeros((d, LANES - 2 * GLA_RANK), BF16)], axis=1)
        wat = wi[:, offs[8]:offs[11]]
        wo = w_out[i].astype(BF16)
        wg2 = jnp.zeros((2, LANES, dkw), F32)
        wg2 = wg2.at[0, 0:GLA_RANK].set(w_gla_gate[i, 0]).at[1, GLA_RANK:2 * GLA_RANK].set(w_gla_gate[i, 1])
        bg2 = b_gla_gate[i].reshape(2, 1, dkw)
        qg = jnp.tile(q_norm_g[i], 4).reshape(1, 2 * LANES)
        kg = jnp.tile(k_norm_g[i], 2).reshape(1, LANES)
        gg = gla_norm_g[i].reshape(1, dvw)
        wdw, bdw = w_dw[i], b_dw[i].reshape(1, cw)
        cng, cnb = conv_norm_g[i].reshape(1, cw), conv_norm_b[i].reshape(1, cw)

        x = _ffn_call(x, mod_l, row_x, g_l[0], w1a[0], w1u[0], w2[0], 0, TOKEN_TILE)
        ctx = _ffn_call(ctx, mod_l, row_c, g_l[0], w1a[0], w1u[0], w2[0], 0, tm_c)
        pc_x, pg_x, pa_x = _inproj_call(x, mod_l, row_x, g_l[1], wc, wgl, wat, TOKEN_TILE)
        pc_c, pg_c, pa_c = _inproj_call(ctx, mod_l, row_c, g_l[1], wc, wgl, wat, tm_c)
        conv_x = _conv_call(pc_x, wdw, bdw, cng, cnb, gm_conv)
        o_c, s_c = _gla_call(pg_c, wg2, bg2, jnp.zeros((b, 2, dvw, dkw), F32), dkw, dvw)
        o_x, _ = _gla_call(pg_x, wg2, bg2, s_c, dkw, dvw)
        k_c, v_c = _kvprep_call(pa_c, one_k, zero_k, kg, gm64_128, tm_c)
        k_x, v_x = _kvprep_call(pa_x, cos_k, sin_k, kg, gm64_128, TOKEN_TILE)
        att_x = _flash_call(pa_x, cos_q, sin_q, qg, gm64_256, [(k_c, v_c), (k_x, v_x)], ATT_TQ)
        x = _out_call(x, mod_l, row_x, conv_x, o_x, pg_x, att_x, gg, gm_gla, wo, TOKEN_TILE)
        x = _ffn_call(x, mod_l, row_x, g_l[2], w1a[1], w1u[1], w2[1], 2, TOKEN_TILE)
        if not last:
            conv_c = _conv_call(pc_c, wdw, bdw, cng, cnb, gm_conv)
            att_c = _flash_call(pa_c, one_q, zero_q, qg, gm64_256, [(k_c, v_c)], ATT_TQ)
            ctx = _out_call(ctx, mod_l, row_c, conv_c, o_c, pg_c, att_c, gg, gm_gla, wo, tm_c)
            ctx = _ffn_call(ctx, mod_l, row_c, g_l[2], w1a[1], w1u[1], w2[1], 2, tm_c)
    return x
```

```python
import functools

import numpy as np
import jax
import jax.numpy as jnp
from jax import lax
from jax.experimental import pallas as pl
from jax.experimental.pallas import tpu as pltpu

F32 = jnp.float32
BF16 = jnp.bfloat16

CONV_GROUPS = 4
CONV_K = 31
GLA_HEADS = 4
GLA_RANK = 16
GLA_GATE_NORM = 16.0
GLA_CHUNK = 64
HEAD_DIM = 64
N_KV_HEADS = 2
GRID_W = 64
ROPE_BASE = 10000.0
N_MOD = 9
EPS = 1e-6
LOG2E = 1.4426950408889634

LANES = 128
SUBLANES = 8
VMEM_LIMIT_BYTES = 56 * 1024 * 1024

FFN_CHUNK = 256
TOKEN_TILE = 512
CONV_ROWS = 32
CONV_HALO = 16
GLA_BLOCK = 512
ATT_TQ = 256
ATT_TK = 512
ATT_LOOKAHEAD = 4


def _cparams(n_axes):
    return pltpu.CompilerParams(dimension_semantics=("arbitrary",) * n_axes,
                                vmem_limit_bytes=VMEM_LIMIT_BYTES)


def _const_spec(shape):
    nd = len(shape)
    return pl.BlockSpec(shape, lambda *_: (0,) * nd, pipeline_mode=pl.Buffered(1))


def _dot(a, b):
    return jnp.dot(a, b, preferred_element_type=F32)


def _dot_nt(a, b):
    return lax.dot_general(a, b, (((1,), (1,)), ((), ())), preferred_element_type=F32)


def _dot_tn(a, b):
    return lax.dot_general(a, b, (((0,), (0,)), ((), ())), preferred_element_type=F32)


def _sigmoid(x):
    return 1.0 / (1.0 + jnp.exp(-x))


def _silu(x):
    return x * _sigmoid(x)


def _split_dot(v, m):
    hi = v.astype(BF16)
    lo = (v - hi.astype(F32)).astype(BF16)
    return _dot(hi, m) + _dot(lo, m)


def _split3_dot_left(m, v):
    hi = v.astype(BF16)
    r1 = v - hi.astype(F32)
    mid = r1.astype(BF16)
    lo = (r1 - mid.astype(F32)).astype(BF16)
    return _dot(m, hi) + _dot(m, mid) + _dot(m, lo)


def _pre(x, g, shift, scale):
    r = lax.rsqrt(jnp.mean(x * x, axis=-1, keepdims=True) + EPS)
    return (x * r * g) * (1.0 + scale) + shift


def _group_mean_matrix(width, group):
    idx = np.arange(width) // group
    return jnp.asarray((idx[:, None] == idx[None, :]).astype(np.float32) / group, dtype=BF16)


def _rope_partner(x):
    n = x.shape[-1]
    lane = lax.broadcasted_iota(jnp.int32, x.shape, x.ndim - 1)
    up = pltpu.roll(x, n - 16, axis=x.ndim - 1)
    dn = pltpu.roll(x, 16, axis=x.ndim - 1)
    return jnp.where((lane % 32) < 16, up, dn)


def _mod_kernel(c_ref, w_ref, b_ref, o_ref):
    sc = _silu(c_ref[...]).astype(BF16)
    o_ref[...] = _dot(sc, w_ref[...].astype(BF16)) + b_ref[...]


def _mod_call(cc, w_ada, b_ada):
    depth, d, _ = w_ada.shape
    rows = cc.shape[0]
    return pl.pallas_call(
        _mod_kernel,
        grid=(depth, N_MOD),
        in_specs=[pl.BlockSpec((rows, d), lambda i, j: (0, 0)),
                  pl.BlockSpec((None, d, d), lambda i, j: (i, 0, j)),
                  pl.BlockSpec((None, 1, d), lambda i, j: (i, 0, j))],
        out_specs=pl.BlockSpec((None, None, rows, d), lambda i, j: (i, j, 0, 0)),
        out_shape=jax.ShapeDtypeStruct((depth, N_MOD, rows, d), F32),
        compiler_params=_cparams(2),
        name="mod",
    )(cc, w_ada, b_ada.reshape(depth, 1, N_MOD * d))


def _ffn_kernel(j, n_chunks, x_ref, mod_ref, g_ref, w1a_ref, w1u_ref, w2_ref, o_ref, acc_ref):
    x = x_ref[...]
    shift = mod_ref[3 * j:3 * j + 1, :]
    scale = mod_ref[3 * j + 1:3 * j + 2, :]
    gate = mod_ref[3 * j + 2:3 * j + 3, :]
    h = _pre(x, g_ref[...], shift, scale).astype(BF16)
    acc_ref[...] = jnp.zeros_like(acc_ref)

    def body(c, carry):
        a = _dot(h, w1a_ref[c])
        u = _dot(h, w1u_ref[c])
        act = (_silu(a) * u).astype(BF16)
        acc_ref[...] += _dot(act, w2_ref[c])
        return carry

    lax.fori_loop(0, n_chunks, body, 0)
    o_ref[...] = x + (0.5 * gate) * acc_ref[...]


def _ffn_call(x, mod_l, mod_row, g, w1a, w1u, w2, j, tm):
    b, t, d = x.shape
    n_chunks = w1a.shape[0]
    return pl.pallas_call(
        functools.partial(_ffn_kernel, j, n_chunks),
        grid=(b, t // tm),
        in_specs=[pl.BlockSpec((None, tm, d), lambda bi, i: (bi, i, 0)),
                  pl.BlockSpec((None, N_MOD, d), lambda bi, i: (mod_row(bi), 0, 0)),
                  _const_spec(g.shape), _const_spec(w1a.shape), _const_spec(w1u.shape),
                  _const_spec(w2.shape)],
        out_specs=pl.BlockSpec((None, tm, d), lambda bi, i: (bi, i, 0)),
        out_shape=jax.ShapeDtypeStruct(x.shape, F32),
        scratch_shapes=[pltpu.VMEM((tm, d), F32)],
        compiler_params=_cparams(2),
        name="ffn",
    )(x, mod_l, g, w1a, w1u, w2)


def _inproj_kernel(x_ref, mod_ref, g_ref, wc_ref, wg_ref, wa_ref, pc_ref, pg_ref, pa_ref):
    h = _pre(x_ref[...], g_ref[...], mod_ref[3:4, :], mod_ref[4:5, :]).astype(BF16)
    pc_ref[...] = _dot(h, wc_ref[...])
    pg_ref[...] = _dot(h, wg_ref[...])
    pa_ref[...] = _dot(h, wa_ref[...])


def _inproj_call(x, mod_l, mod_row, g, wc, wg, wa, tm):
    b, t, d = x.shape
    tok = lambda bi, i: (bi, i, 0)
    return pl.pallas_call(
        _inproj_kernel,
        grid=(b, t // tm),
        in_specs=[pl.BlockSpec((None, tm, d), tok),
                  pl.BlockSpec((None, N_MOD, d), lambda bi, i: (mod_row(bi), 0, 0)),
                  _const_spec(g.shape), _const_spec(wc.shape), _const_spec(wg.shape),
                  _const_spec(wa.shape)],
        out_specs=[pl.BlockSpec((None, tm, wc.shape[1]), tok),
                   pl.BlockSpec((None, tm, wg.shape[1]), tok),
                   pl.BlockSpec((None, tm, wa.shape[1]), tok)],
        out_shape=[jax.ShapeDtypeStruct((b, t, wc.shape[1]), F32),
                   jax.ShapeDtypeStruct((b, t, wg.shape[1]), F32),
                   jax.ShapeDtypeStruct((b, t, wa.shape[1]), F32)],
        compiler_params=_cparams(2),
        name="inproj",
    )(x, mod_l, g, wc, wg, wa)


def _conv_kernel(t, cw, pc_ref, wdw_ref, bdw_ref, cg_ref, cb_ref, gm_ref, o_ref, hbuf):
    r = CONV_ROWS
    halo = CONV_HALO
    hbuf[0:halo, :] = jnp.zeros((halo, cw), F32)
    hbuf[t + halo:t + 2 * halo, :] = jnp.zeros((halo, cw), F32)

    def glu(i, carry):
        r0 = pl.multiple_of(i * r, r)
        a = pc_ref[pl.ds(r0, r), 0:cw]
        gt = pc_ref[pl.ds(r0, r), cw:2 * cw]
        hbuf[pl.ds(r0 + halo, r), :] = a * _sigmoid(gt)
        return carry

    lax.fori_loop(0, t // r, glu, 0)
    gm = gm_ref[...]

    def tile(i, carry):
        r0 = pl.multiple_of(i * r, r)
        win = hbuf[pl.ds(r0, r + 2 * halo), :]
        acc = jnp.zeros((r, cw), F32)
        for res in range(SUBLANES):
            wr = win if res == 0 else pltpu.roll(win, r + 2 * halo - res, axis=0)
            for a in range(2 * halo // SUBLANES):
                s = SUBLANES * a + res
                if 1 <= s <= CONV_K:
                    acc = acc + wr[SUBLANES * a:SUBLANES * a + r, :] * wdw_ref[s - 1:s, :]
        h2 = acc + bdw_ref[...]
        mu = _split_dot(h2, gm)
        dlt = h2 - mu
        var = _split_dot(dlt * dlt, gm)
        y = dlt * lax.rsqrt(var + EPS) * cg_ref[...] + cb_ref[...]
        o_ref[pl.ds(r0, r), :] = _silu(y)
        return carry

    lax.fori_loop(0, t // r, tile, 0)


def _conv_call(pc, w_dw, b_dw, cn_g, cn_b, gm):
    b, t, two_cw = pc.shape
    cw = two_cw // 2
    return pl.pallas_call(
        functools.partial(_conv_kernel, t, cw),
        grid=(b,),
        in_specs=[pl.BlockSpec((None, t, two_cw), lambda bi: (bi, 0, 0)),
                  _const_spec(w_dw.shape), _const_spec(b_dw.shape), _const_spec(cn_g.shape),
                  _const_spec(cn_b.shape), _const_spec(gm.shape)],
        out_specs=pl.BlockSpec((None, t, cw), lambda bi: (bi, 0, 0)),
        out_shape=jax.ShapeDtypeStruct((b, t, cw), F32),
        scratch_shapes=[pltpu.VMEM((t + 2 * CONV_HALO, cw), F32)],
        compiler_params=_cparams(1),
        name="conv",
    )(pc, w_dw, b_dw, cn_g, cn_b, gm)


def _gla_kernel(n_ch, dkw, dvw, pg_ref, wg_ref, bg_ref, s0_ref, o_ref, sfin_ref, s_scr, la_scr):
    c = GLA_CHUNK
    d = pl.program_id(1)
    i = pl.program_id(2)
    fwd = d == 0

    @pl.when(i == 0)
    def _():
        s_scr[...] = s0_ref[...]

    lr_off = 2 * dkw + 2 * dvw
    z = _dot(pg_ref[:, lr_off:lr_off + LANES], wg_ref[...]) + bg_ref[...]
    la_scr[...] = -(jnp.maximum(-z, 0.0) + jnp.log(1.0 + jnp.exp(-jnp.abs(z)))) * (1.0 / GLA_GATE_NORM)

    ri = lax.broadcasted_iota(jnp.int32, (c, c), 0)
    ci = lax.broadcasted_iota(jnp.int32, (c, c), 1)
    tri = jnp.where(fwd, ci - ri, ri - ci) <= 0
    tri_m = jnp.where(tri, 1.0, 0.0).astype(BF16)
    rih = lax.broadcasted_iota(jnp.int32, (c, GLA_HEADS * c), 0)
    cih = lax.broadcasted_iota(jnp.int32, (c, GLA_HEADS * c), 1) % c
    tri_h = jnp.where(fwd, cih - rih, rih - cih) <= 0
    dk = dkw // GLA_HEADS
    dv = dvw // GLA_HEADS
    klane = lax.broadcasted_iota(jnp.int32, (c, dkw), 1) // dk
    vlane = lax.broadcasted_iota(jnp.int32, (c, dvw), 1) // dv
    srow = lax.broadcasted_iota(jnp.int32, (dvw, dkw), 0) // dv
    scol = lax.broadcasted_iota(jnp.int32, (dvw, dkw), 1) // dk
    smask = srow == scol

    def chunk(jj, carry):
        cidx = jnp.where(fwd, jj, n_ch - 1 - jj)
        rows = pl.ds(pl.multiple_of(cidx * c, c), c)
        q = pg_ref[rows, 0:dkw] * (dk ** -0.5)
        k = pg_ref[rows, dkw:2 * dkw]
        v = pg_ref[rows, 2 * dkw:2 * dkw + dvw]
        la = la_scr[rows, :]
        bcum = _split3_dot_left(tri_m, la)
        btot = jnp.sum(la, axis=0, keepdims=True)
        q_e = (q * jnp.exp(bcum)).astype(BF16)
        k_e = k * jnp.exp(-bcum)
        k_end = (k * jnp.exp(btot - bcum)).astype(BF16)
        kblk = jnp.concatenate([jnp.where(klane == h, k_e, 0.0) for h in range(GLA_HEADS)],
                               axis=0).astype(BF16)
        vblk = jnp.concatenate([jnp.where(vlane == h, v, 0.0) for h in range(GLA_HEADS)],
                               axis=0).astype(BF16)
        att = jnp.where(tri_h, _dot_nt(q_e, kblk), 0.0).astype(BF16)
        s_t = s_scr[...]
        o_ref[rows, :] = _dot(att, vblk) + _dot_nt(q_e, s_t.astype(BF16))
        upd = _dot_tn(v.astype(BF16), k_end)
        s_scr[...] = s_t * jnp.exp(btot) + jnp.where(smask, upd, 0.0)
        return carry

    lax.fori_loop(0, n_ch, chunk, 0)

    @pl.when(i == pl.num_programs(2) - 1)
    def _():
        sfin_ref[...] = s_scr[...]


def _gla_call(pg, wg2, bg2, s0, dkw, dvw):
    b, t, w = pg.shape
    blk = min(GLA_BLOCK, t)
    nb = t // blk
    n_ch = blk // GLA_CHUNK
    seq = lambda bi, d, i: (bi, jnp.where(d == 0, i, nb - 1 - i), 0)
    return pl.pallas_call(
        functools.partial(_gla_kernel, n_ch, dkw, dvw),
        grid=(b, 2, nb),
        in_specs=[pl.BlockSpec((None, blk, w), seq),
                  pl.BlockSpec((None, LANES, dkw), lambda bi, d, i: (d, 0, 0)),
                  pl.BlockSpec((None, 1, dkw), lambda bi, d, i: (d, 0, 0)),
                  pl.BlockSpec((None, None, dvw, dkw), lambda bi, d, i: (bi, d, 0, 0))],
        out_specs=[pl.BlockSpec((None, None, blk, dvw),
                                lambda bi, d, i: (d, bi, jnp.where(d == 0, i, nb - 1 - i), 0)),
                   pl.BlockSpec((None, None, dvw, dkw), lambda bi, d, i: (bi, d, 0, 0))],
        out_shape=[jax.ShapeDtypeStruct((2, b, t, dvw), F32),
                   jax.ShapeDtypeStruct((b, 2, dvw, dkw), F32)],
        scratch_shapes=[pltpu.VMEM((dvw, dkw), F32), pltpu.VMEM((blk, dkw), F32)],
        compiler_params=_cparams(3),
        name="gla",
    )(pg, wg2, bg2, s0)


def _kvprep_kernel(kv_ref, cos_ref, sin_ref, g_ref, gm_ref, k_ref, vt_ref):
    hd = HEAD_DIM
    kk = kv_ref[:, 0:LANES]
    vv = kv_ref[:, LANES:2 * LANES]
    ms = _split_dot(kk * kk, gm_ref[...])
    kn = kk * lax.rsqrt(ms + EPS) * g_ref[...]
    kr = kn * cos_ref[...] + _rope_partner(kn) * sin_ref[...]
    lane = lax.broadcasted_iota(jnp.int32, kr.shape, 1)
    lo = lane < hd
    kswap = pltpu.roll(kr, hd, axis=1)
    k0 = jnp.where(lo, kr, kswap).astype(BF16)
    k1 = jnp.where(lo, kswap, kr).astype(BF16)
    k_ref[...] = jnp.concatenate([k0, k0, k1, k1], axis=1)
    vt_ref[0] = jnp.where(lo, vv, 1.0).T.astype(BF16)
    vt_ref[1] = jnp.where(lo, pltpu.roll(vv, hd, axis=1), 1.0).T.astype(BF16)


def _kvprep_call(pa, cos_t, sin_t, kg, gm, tt):
    b, t, _ = pa.shape
    return pl.pallas_call(
        _kvprep_kernel,
        grid=(b, t // tt),
        in_specs=[pl.BlockSpec((None, tt, 2 * LANES), lambda bi, i: (bi, i, 2)),
                  pl.BlockSpec((tt, LANES), lambda bi, i: (i, 0)),
                  pl.BlockSpec((tt, LANES), lambda bi, i: (i, 0)),
                  _const_spec(kg.shape), _const_spec(gm.shape)],
        out_specs=[pl.BlockSpec((None, tt, 4 * LANES), lambda bi, i: (bi, i, 0)),
                   pl.BlockSpec((None, N_KV_HEADS, LANES, tt), lambda bi, i: (bi, 0, 0, i))],
        out_shape=[jax.ShapeDtypeStruct((b, t, 4 * LANES), BF16),
                   jax.ShapeDtypeStruct((b, N_KV_HEADS, LANES, t), BF16)],
        compiler_params=_cparams(2),
        name="kvprep",
    )(pa, cos_t, sin_t, kg, gm)


def _flash_kernel(seg_lens, q_ref, cos_ref, sin_ref, g_ref, gm_ref, *rest):
    n_seg = len(seg_lens)
    kv_refs = rest[:2 * n_seg]
    o_ref, qs_scr = rest[2 * n_seg:]
    hd = HEAD_DIM
    g_heads = 2 * LANES // hd
    q = q_ref[...]
    ms = _split_dot(q * q, gm_ref[...])
    qn = q * lax.rsqrt(ms + EPS) * g_ref[...]
    qr = (qn * cos_ref[...] + _rope_partner(qn) * sin_ref[...]) * (hd ** -0.5 * LOG2E)
    lane_head = lax.broadcasted_iota(jnp.int32, qr.shape, 1) // hd
    for h in range(g_heads):
        qs_scr[h] = jnp.where(lane_head == h, qr, 0.0).astype(BF16)

    tasks = []
    for h in range(g_heads):
        first = True
        for si in range(n_seg):
            k_ref, vt_ref = kv_refs[2 * si], kv_refs[2 * si + 1]
            tk = min(ATT_TK, seg_lens[si])
            for c0 in range(0, seg_lens[si], tk):
                tasks.append((h, k_ref, vt_ref, c0, tk, first))
                first = False

    def scores(task):
        h, k_ref, _, c0, tk, _ = task
        return _dot_nt(k_ref[c0:c0 + tk, :], qs_scr[h])

    m_run = [None] * g_heads
    acc = [None] * g_heads
    pending = [scores(t) for t in tasks[:ATT_LOOKAHEAD]]
    for n, (h, _, vt_ref, c0, tk, first) in enumerate(tasks):
        if n + ATT_LOOKAHEAD < len(tasks):
            pending.append(scores(tasks[n + ATT_LOOKAHEAD]))
        s = pending.pop(0)
        m_cur = jnp.max(s, axis=0, keepdims=True)
        if first:
            m_new = m_cur
            p = jnp.exp2(s - m_new).astype(BF16)
            acc[h] = _dot(vt_ref[:, c0:c0 + tk], p)
        else:
            m_new = jnp.maximum(m_run[h], m_cur)
            p = jnp.exp2(s - m_new).astype(BF16)
            acc[h] = acc[h] * jnp.exp2(m_run[h] - m_new) + _dot(vt_ref[:, c0:c0 + tk], p)
        m_run[h] = m_new

    o_t = jnp.concatenate([acc[h][0:hd, :] * (1.0 / acc[h][hd:hd + 1, :]) for h in range(g_heads)],
                          axis=0)
    o_ref[...] = o_t.T


def _flash_call(pa, cos_q, sin_q, qg, gm, segs, tq):
    b, t, _ = pa.shape
    tq = min(tq, t)
    seg_lens = tuple(k.shape[1] for k, _ in segs)
    in_specs = [pl.BlockSpec((None, tq, 2 * LANES), lambda bi, j, i: (bi, i, j)),
                pl.BlockSpec((tq, 2 * LANES), lambda bi, j, i: (i, 0)),
                pl.BlockSpec((tq, 2 * LANES), lambda bi, j, i: (i, 0)),
                _const_spec(qg.shape), _const_spec(gm.shape)]
    args = [pa, cos_q, sin_q, qg, gm]
    for k, v in segs:
        in_specs.append(pl.BlockSpec((None, k.shape[1], 2 * LANES), lambda bi, j, i: (bi, 0, j)))
        in_specs.append(pl.BlockSpec((None, None, LANES, v.shape[3]), lambda bi, j, i: (bi, j, 0, 0)))
        args += [k, v]
    g_heads = 2 * LANES // HEAD_DIM
    return pl.pallas_call(
        functools.partial(_flash_kernel, seg_lens),
        grid=(b, N_KV_HEADS, t // tq),
        in_specs=in_specs,
        out_specs=pl.BlockSpec((None, tq, 2 * LANES), lambda bi, j, i: (bi, i, j)),
        out_shape=jax.ShapeDtypeStruct((b, t, N_KV_HEADS * 2 * LANES), F32),
        scratch_shapes=[pltpu.VMEM((g_heads, tq, 2 * LANES), BF16)],
        compiler_params=_cparams(3),
        name="flash",
    )(*args)


def _out_kernel(cw, dvw, x_ref, mod_ref, conv_ref, of_ref, ob_ref, r_ref, att_ref, gg_ref, gm_ref,
                wo_ref, o_ref):
    o = of_ref[...] + ob_ref[...]
    ms = _split_dot(o * o, gm_ref[...])
    gl = (o * lax.rsqrt(ms + EPS) * gg_ref[...]) * _silu(r_ref[...])
    y = (_dot(conv_ref[...].astype(BF16), wo_ref[0:cw, :])
         + _dot(gl.astype(BF16), wo_ref[cw:cw + dvw, :])
         + _dot(att_ref[...].astype(BF16), wo_ref[cw + dvw:, :]))
    o_ref[...] = x_ref[...] + mod_ref[5:6, :] * y


def _out_call(x, mod_l, mod_row, conv, o_gla, pg, att, gg, gm, wo, tm):
    b, t, d = x.shape
    cw = conv.shape[2]
    dvw = o_gla.shape[3]
    aw = att.shape[2]
    r_blk = (pg.shape[2] - dvw) // dvw
    tok = lambda bi, i: (bi, i, 0)
    del r_blk
    return pl.pallas_call(
        functools.partial(_out_kernel, cw, dvw),
        grid=(b, t // tm),
        in_specs=[pl.BlockSpec((None, tm, d), tok),
                  pl.BlockSpec((None, N_MOD, d), lambda bi, i: (mod_row(bi), 0, 0)),
                  pl.BlockSpec((None, tm, cw), tok),
                  pl.BlockSpec((None, None, tm, dvw), lambda bi, i: (0, bi, i, 0)),
                  pl.BlockSpec((None, None, tm, dvw), lambda bi, i: (1, bi, i, 0)),
                  pl.BlockSpec((None, tm, dvw), lambda bi, i: (bi, i, 2)),
                  pl.BlockSpec((None, tm, aw), tok),
                  _const_spec(gg.shape), _const_spec(gm.shape), _const_spec(wo.shape)],
        out_specs=pl.BlockSpec((None, tm, d), tok),
        out_shape=jax.ShapeDtypeStruct(x.shape, F32),
        compiler_params=_cparams(2),
        name="outproj",
    )(x, mod_l, conv, o_gla, o_gla, pg, att, gg, gm, wo)


def _rope_tables(s):
    nf = HEAD_DIM // 4
    rows_n = s // GRID_W
    row = jnp.repeat(jnp.arange(rows_n), GRID_W).astype(F32)
    col = jnp.tile(jnp.arange(GRID_W), rows_n).astype(F32)
    freqs = ROPE_BASE ** (-jnp.arange(nf, dtype=F32) / nf)
    ang_r = row[:, None] * freqs
    ang_c = col[:, None] * freqs
    cos = jnp.concatenate([jnp.cos(ang_r)] * 2 + [jnp.cos(ang_c)] * 2, axis=1)
    sin = jnp.concatenate([-jnp.sin(ang_r), jnp.sin(ang_r), -jnp.sin(ang_c), jnp.sin(ang_c)], axis=1)
    return cos, sin


def kernel(x, c, ctx, c_ctx, w_ada, b_ada, g_norm, w_ffn_in, w_ffn_out, w_in, w_out, w_dw, b_dw,
           conv_norm_g, conv_norm_b, w_gla_gate, b_gla_gate, gla_norm_g, q_norm_g, k_norm_g):
    b, s, d = x.shape
    l = ctx.shape[1]
    depth = w_ada.shape[0]
    d_ff = w_ffn_out.shape[2]
    cw = w_dw.shape[2]
    dkw = w_gla_gate.shape[3]
    dvw = gla_norm_g.shape[1] * gla_norm_g.shape[2]
    n_kv = N_KV_HEADS * HEAD_DIM
    aw = d - cw - dvw
    assert d_ff % FFN_CHUNK == 0 and s % TOKEN_TILE == 0 and l % GLA_CHUNK == 0
    assert dkw == LANES and n_kv == LANES and cw == 2 * LANES and dvw == 2 * LANES and aw == 4 * LANES

    rows = -(-(b + 1) // SUBLANES) * SUBLANES
    cc = jnp.zeros((rows, d), F32).at[:b].set(c).at[b].set(c_ctx)
    mod = jnp.transpose(_mod_call(cc, w_ada, b_ada), (0, 2, 1, 3))
    row_x = lambda bi: bi
    row_c = lambda bi: b

    cos64, sin64 = _rope_tables(s)
    cos_k, sin_k = jnp.tile(cos64, (1, 2)), jnp.tile(sin64, (1, 2))
    cos_q, sin_q = jnp.tile(cos64, (1, 4)), jnp.tile(sin64, (1, 4))
    one_k, zero_k = jnp.ones((l, LANES), F32), jnp.zeros((l, LANES), F32)
    one_q, zero_q = jnp.ones((l, 2 * LANES), F32), jnp.zeros((l, 2 * LANES), F32)

    gm64_128 = _group_mean_matrix(LANES, HEAD_DIM)
    gm64_256 = _group_mean_matrix(2 * LANES, HEAD_DIM)
    gm_conv = _group_mean_matrix(cw, cw // CONV_GROUPS)
    gm_gla = _group_mean_matrix(dvw, dvw // GLA_HEADS)

    sizes = (cw, cw, dkw, dkw, dvw, dvw, GLA_RANK, GLA_RANK, aw, n_kv, n_kv)
    offs = np.concatenate([[0], np.cumsum(sizes)])
    n_chunks = d_ff // FFN_CHUNK
    tm_c = min(TOKEN_TILE, l)

    for i in range(depth):
        last = i == depth - 1
        mod_l = mod[i]
        g_l = g_norm[i].reshape(3, 1, d)
        w1 = w_ffn_in[i].astype(BF16)
        w1a = w1[:, :, :d_ff].reshape(2, d, n_chunks, FFN_CHUNK).transpose(0, 2, 1, 3)
        w1u = w1[:, :, d_ff:].reshape(2, d, n_chunks, FFN_CHUNK).transpose(0, 2, 1, 3)
        w2 = w_ffn_out[i].astype(BF16).reshape(2, n_chunks, FFN_CHUNK, d)
        wi = w_in[i].astype(BF16)
        wc = wi[:, offs[0]:offs[2]]
        wgl = jnp.concatenate([wi[:, offs[2]:offs[8]],
                               jnp.zeros((d, LANES - 2 * GLA_RANK), BF16)], axis=1)
        wat = wi[:, offs[8]:offs[11]]
        wo = w_out[i].astype(BF16)
        wg2 = jnp.zeros((2, LANES, dkw), F32)
        wg2 = wg2.at[0, 0:GLA_RANK].set(w_gla_gate[i, 0]).at[1, GLA_RANK:2 * GLA_RANK].set(w_gla_gate[i, 1])
        bg2 = b_gla_gate[i].reshape(2, 1, dkw)
        qg = jnp.tile(q_norm_g[i], 4).reshape(1, 2 * LANES)
        kg = jnp.tile(k_norm_g[i], 2).reshape(1, LANES)
        gg = gla_norm_g[i].reshape(1, dvw)
        wdw, bdw = w_dw[i], b_dw[i].reshape(1, cw)
        cng, cnb = conv_norm_g[i].reshape(1, cw), conv_norm_b[i].reshape(1, cw)

        x = _ffn_call(x, mod_l, row_x, g_l[0], w1a[0], w1u[0], w2[0], 0, TOKEN_TILE)
        ctx = _ffn_call(ctx, mod_l, row_c, g_l[0], w1a[0], w1u[0], w2[0], 0, tm_c)
        pc_x, pg_x, pa_x = _inproj_call(x, mod_l, row_x, g_l[1], wc, wgl, wat, TOKEN_TILE)
        pc_c, pg_c, pa_c = _inproj_call(ctx, mod_l, row_c, g_l[1], wc, wgl, wat, tm_c)
        conv_x = _conv_call(pc_x, wdw, bdw, cng, cnb, gm_conv)
        o_c, s_c = _gla_call(pg_c, wg2, bg2, jnp.zeros((b, 2, dvw, dkw), F32), dkw, dvw)
        o_x, _ = _gla_call(pg_x, wg2, bg2, s_c, dkw, dvw)
        k_c, v_c = _kvprep_call(pa_c, one_k, zero_k, kg, gm64_128, tm_c)
        k_x, v_x = _kvprep_call(pa_x, cos_k, sin_k, kg, gm64_128, TOKEN_TILE)
        att_x = _flash_call(pa_x, cos_q, sin_q, qg, gm64_256, [(k_c, v_c), (k_x, v_x)], ATT_TQ)
        x = _out_call(x, mod_l, row_x, conv_x, o_x, pg_x, att_x, gg, gm_gla, wo, TOKEN_TILE)
        x = _ffn_call(x, mod_l, row_x, g_l[2], w1a[1], w1u[1], w2[1], 2, TOKEN_TILE)
        if not last:
            conv_c = _conv_call(pc_c, wdw, bdw, cng, cnb, gm_conv)
            att_c = _flash_call(pa_c, one_q, zero_q, qg, gm64_256, [(k_c, v_c)], ATT_TQ)
            ctx = _out_call(ctx, mod_l, row_c, conv_c, o_c, pg_c, att_c, gg, gm_gla, wo, tm_c)
            ctx = _ffn_call(ctx, mod_l, row_c, g_l[2], w1a[1], w1u[1], w2[1], 2, tm_c)
    return x
```

```python
import functools

import numpy as np
import jax
import jax.numpy as jnp
from jax import lax
from jax.experimental import pallas as pl
from jax.experimental.pallas import tpu as pltpu

F32 = jnp.float32
BF16 = jnp.bfloat16

CONV_GROUPS = 4
CONV_K = 31
GLA_HEADS = 4
GLA_RANK = 16
GLA_GATE_NORM = 16.0
GLA_CHUNK = 64
HEAD_DIM = 64
N_KV_HEADS = 2
GRID_W = 64
ROPE_BASE = 10000.0
N_MOD = 9
EPS = 1e-6
LOG2E = 1.4426950408889634

LANES = 128
SUBLANES = 8
VMEM_LIMIT_BYTES = 56 * 1024 * 1024

FFN_CHUNK = 256
TOKEN_TILE = 512
CONV_ROWS = 32
CONV_HALO = 16
CONV_NORM_ROWS = 1024
GLA_BLOCK = 512
ATT_TQ = 256
ATT_TK = 512
ATT_LOOKAHEAD = 4


def _cparams(n_axes):
    return pltpu.CompilerParams(dimension_semantics=("arbitrary",) * n_axes,
                                vmem_limit_bytes=VMEM_LIMIT_BYTES)


def _const_spec(shape):
    nd = len(shape)
    return pl.BlockSpec(shape, lambda *_: (0,) * nd, pipeline_mode=pl.Buffered(1))


def _dot(a, b):
    return jnp.dot(a, b, preferred_element_type=F32)


def _dot_nt(a, b):
    return lax.dot_general(a, b, (((1,), (1,)), ((), ())), preferred_element_type=F32)


def _dot_tn(a, b):
    return lax.dot_general(a, b, (((0,), (0,)), ((), ())), preferred_element_type=F32)


def _sigmoid(x):
    return 1.0 / (1.0 + jnp.exp(-x))


def _silu(x):
    return x * _sigmoid(x)


def _split_dot(v, m):
    hi = v.astype(BF16)
    lo = (v - hi.astype(F32)).astype(BF16)
    return _dot(hi, m) + _dot(lo, m)


def _split3_dot_left(m, v):
    hi = v.astype(BF16)
    r1 = v - hi.astype(F32)
    mid = r1.astype(BF16)
    lo = (r1 - mid.astype(F32)).astype(BF16)
    return _dot(m, hi) + _dot(m, mid) + _dot(m, lo)


def _pre(x, g, shift, scale):
    r = lax.rsqrt(jnp.mean(x * x, axis=-1, keepdims=True) + EPS)
    return (x * r * g) * (1.0 + scale) + shift


def _group_mean_matrix(width, group):
    idx = np.arange(width) // group
    return jnp.asarray((idx[:, None] == idx[None, :]).astype(np.float32) / group, dtype=BF16)


def _rope_partner(x):
    n = x.shape[-1]
    lane = lax.broadcasted_iota(jnp.int32, x.shape, x.ndim - 1)
    up = pltpu.roll(x, n - 16, axis=x.ndim - 1)
    dn = pltpu.roll(x, 16, axis=x.ndim - 1)
    return jnp.where((lane % 32) < 16, up, dn)


def _mod_kernel(c_ref, w_ref, b_ref, o_ref):
    sc = _silu(c_ref[...]).astype(BF16)
    o_ref[...] = _dot(sc, w_ref[...].astype(BF16)) + b_ref[...]


def _mod_call(cc, w_ada, b_ada):
    depth, d, _ = w_ada.shape
    rows = cc.shape[0]
    return pl.pallas_call(
        _mod_kernel,
        grid=(depth, N_MOD),
        in_specs=[pl.BlockSpec((rows, d), lambda i, j: (0, 0)),
                  pl.BlockSpec((None, d, d), lambda i, j: (i, 0, j)),
                  pl.BlockSpec((None, 1, d), lambda i, j: (i, 0, j))],
        out_specs=pl.BlockSpec((None, None, rows, d), lambda i, j: (i, j, 0, 0)),
        out_shape=jax.ShapeDtypeStruct((depth, N_MOD, rows, d), F32),
        compiler_params=_cparams(2),
        name="mod",
    )(cc, w_ada, b_ada.reshape(depth, 1, N_MOD * d))


def _ffn_kernel(j, n_chunks, x_ref, mod_ref, g_ref, w1a_ref, w1u_ref, w2_ref, o_ref, acc_ref):
    x = x_ref[...]
    shift = mod_ref[3 * j:3 * j + 1, :]
    scale = mod_ref[3 * j + 1:3 * j + 2, :]
    gate = mod_ref[3 * j + 2:3 * j + 3, :]
    h = _pre(x, g_ref[...], shift, scale).astype(BF16)
    acc_ref[...] = jnp.zeros_like(acc_ref)

    def body(c, carry):
        a = _dot(h, w1a_ref[c])
        u = _dot(h, w1u_ref[c])
        act = (_silu(a) * u).astype(BF16)
        acc_ref[...] += _dot(act, w2_ref[c])
        return carry

    lax.fori_loop(0, n_chunks, body, 0, unroll=True)
    o_ref[...] = x + (0.5 * gate) * acc_ref[...]


def _ffn_call(x, mod_l, mod_row, g, w1a, w1u, w2, j, tm):
    b, t, d = x.shape
    n_chunks = w1a.shape[0]
    return pl.pallas_call(
        functools.partial(_ffn_kernel, j, n_chunks),
        grid=(b, t // tm),
        in_specs=[pl.BlockSpec((None, tm, d), lambda bi, i: (bi, i, 0)),
                  pl.BlockSpec((None, N_MOD, d), lambda bi, i: (mod_row(bi), 0, 0)),
                  _const_spec(g.shape), _const_spec(w1a.shape), _const_spec(w1u.shape),
                  _const_spec(w2.shape)],
        out_specs=pl.BlockSpec((None, tm, d), lambda bi, i: (bi, i, 0)),
        out_shape=jax.ShapeDtypeStruct(x.shape, F32),
        scratch_shapes=[pltpu.VMEM((tm, d), F32)],
        compiler_params=_cparams(2),
        name="ffn",
    )(x, mod_l, g, w1a, w1u, w2)


def _inproj_kernel(x_ref, mod_ref, g_ref, wc_ref, wg_ref, wa_ref, pc_ref, pg_ref, pa_ref):
    h = _pre(x_ref[...], g_ref[...], mod_ref[3:4, :], mod_ref[4:5, :]).astype(BF16)
    pc_ref[...] = _dot(h, wc_ref[...])
    pg_ref[...] = _dot(h, wg_ref[...])
    pa_ref[...] = _dot(h, wa_ref[...])


def _inproj_call(x, mod_l, mod_row, g, wc, wg, wa, tm):
    b, t, d = x.shape
    tok = lambda bi, i: (bi, i, 0)
    return pl.pallas_call(
        _inproj_kernel,
        grid=(b, t // tm),
        in_specs=[pl.BlockSpec((None, tm, d), tok),
                  pl.BlockSpec((None, N_MOD, d), lambda bi, i: (mod_row(bi), 0, 0)),
                  _const_spec(g.shape), _const_spec(wc.shape), _const_spec(wg.shape),
                  _const_spec(wa.shape)],
        out_specs=[pl.BlockSpec((None, tm, wc.shape[1]), tok),
                   pl.BlockSpec((None, tm, wg.shape[1]), tok),
                   pl.BlockSpec((None, tm, wa.shape[1]), tok)],
        out_shape=[jax.ShapeDtypeStruct((b, t, wc.shape[1]), F32),
                   jax.ShapeDtypeStruct((b, t, wg.shape[1]), F32),
                   jax.ShapeDtypeStruct((b, t, wa.shape[1]), F32)],
        compiler_params=_cparams(2),
        name="inproj",
    )(x, mod_l, g, wc, wg, wa)


def _conv_kernel(t, cw, pc_ref, wdw_ref, bdw_ref, cg_ref, cb_ref, gm_ref, o_ref, hbuf):
    r = CONV_ROWS
    halo = CONV_HALO
    hbuf[0:halo, :] = jnp.zeros((halo, cw), F32)
    hbuf[t + halo:t + 2 * halo, :] = jnp.zeros((halo, cw), F32)

    def glu(i, carry):
        r0 = pl.multiple_of(i * r, r)
        a = pc_ref[pl.ds(r0, r), 0:cw]
        gt = pc_ref[pl.ds(r0, r), cw:2 * cw]
        hbuf[pl.ds(r0 + halo, r), :] = a * _sigmoid(gt)
        return carry

    lax.fori_loop(0, t // r, glu, 0)
    gm = gm_ref[...]

    def tile(i, carry):
        r0 = pl.multiple_of(i * r, r)
        win = hbuf[pl.ds(r0, r + 2 * halo), :]
        acc = jnp.zeros((r, cw), F32)
        for res in range(SUBLANES):
            wr = win if res == 0 else pltpu.roll(win, r + 2 * halo - res, axis=0)
            for a in range(2 * halo // SUBLANES):
                s = SUBLANES * a + res
                if 1 <= s <= CONV_K:
                    acc = acc + wr[SUBLANES * a:SUBLANES * a + r, :] * wdw_ref[s - 1:s, :]
        o_ref[pl.ds(r0, r), :] = acc + bdw_ref[...]
        return carry

    lax.fori_loop(0, t // r, tile, 0, unroll=2)

    nb = min(CONV_NORM_ROWS, t)

    def norm(i, carry):
        rows = pl.ds(pl.multiple_of(i * nb, nb), nb)
        h2 = o_ref[rows, :]
        mu = _split_dot(h2, gm)
        dlt = h2 - mu
        var = _split_dot(dlt * dlt, gm)
        y = dlt * lax.rsqrt(var + EPS) * cg_ref[...] + cb_ref[...]
        o_ref[rows, :] = _silu(y)
        return carry

    lax.fori_loop(0, t // nb, norm, 0)


def _conv_call(pc, w_dw, b_dw, cn_g, cn_b, gm):
    b, t, two_cw = pc.shape
    cw = two_cw // 2
    return pl.pallas_call(
        functools.partial(_conv_kernel, t, cw),
        grid=(b,),
        in_specs=[pl.BlockSpec((None, t, two_cw), lambda bi: (bi, 0, 0)),
                  _const_spec(w_dw.shape), _const_spec(b_dw.shape), _const_spec(cn_g.shape),
                  _const_spec(cn_b.shape), _const_spec(gm.shape)],
        out_specs=pl.BlockSpec((None, t, cw), lambda bi: (bi, 0, 0)),
        out_shape=jax.ShapeDtypeStruct((b, t, cw), F32),
        scratch_shapes=[pltpu.VMEM((t + 2 * CONV_HALO, cw), F32)],
        compiler_params=_cparams(1),
        name="conv",
    )(pc, w_dw, b_dw, cn_g, cn_b, gm)


def _gla_block(fwd, n_ch, dkw, dvw, pg_ref, wg_ref, bg_ref, o_ref, s_scr):
    c = GLA_CHUNK
    dk = dkw // GLA_HEADS
    dv = dvw // GLA_HEADS
    lr_off = 2 * dkw + 2 * dvw
    z = _dot(pg_ref[:, lr_off:lr_off + LANES], wg_ref[...]) + bg_ref[...]
    la_all = -(jnp.maximum(-z, 0.0) + jnp.log(1.0 + jnp.exp(-jnp.abs(z)))) * (1.0 / GLA_GATE_NORM)

    ri = lax.broadcasted_iota(jnp.int32, (c, c), 0)
    ci = lax.broadcasted_iota(jnp.int32, (c, c), 1)
    tri_m = jnp.where((ci <= ri) if fwd else (ci >= ri), 1.0, 0.0).astype(BF16)
    rih = lax.broadcasted_iota(jnp.int32, (c, GLA_HEADS * c), 0)
    cih = lax.broadcasted_iota(jnp.int32, (c, GLA_HEADS * c), 1) % c
    tri_h = (cih <= rih) if fwd else (cih >= rih)
    klane = lax.broadcasted_iota(jnp.int32, (c, dkw), 1) // dk
    vlane = lax.broadcasted_iota(jnp.int32, (c, dvw), 1) // dv
    srow = lax.broadcasted_iota(jnp.int32, (dvw, dkw), 0) // dv
    scol = lax.broadcasted_iota(jnp.int32, (dvw, dkw), 1) // dk
    smask = srow == scol

    chunks = range(n_ch)
    rows = [slice(j * c, (j + 1) * c) for j in chunks]
    bcum = [_split3_dot_left(tri_m, la_all[rows[j], :]) for j in chunks]
    q_e, k_end, kblk, vblk, vb, decay = [], [], [], [], [], []
    for j in chunks:
        q = pg_ref[rows[j], 0:dkw] * (dk ** -0.5)
        k = pg_ref[rows[j], dkw:2 * dkw]
        v = pg_ref[rows[j], 2 * dkw:2 * dkw + dvw]
        btot = bcum[j][c - 1:c, :] if fwd else bcum[j][0:1, :]
        q_e.append((q * jnp.exp(bcum[j])).astype(BF16))
        k_e = k * jnp.exp(-bcum[j])
        k_end.append((k * jnp.exp(btot - bcum[j])).astype(BF16))
        kblk.append(jnp.concatenate([jnp.where(klane == h, k_e, 0.0) for h in range(GLA_HEADS)],
                                    axis=0).astype(BF16))
        vblk.append(jnp.concatenate([jnp.where(vlane == h, v, 0.0) for h in range(GLA_HEADS)],
                                    axis=0).astype(BF16))
        vb.append(v.astype(BF16))
        decay.append(jnp.exp(btot))
    att = [jnp.where(tri_h, _dot_nt(q_e[j], kblk[j]), 0.0).astype(BF16) for j in chunks]
    o_intra = [_dot(att[j], vblk[j]) for j in chunks]
    upd = [jnp.where(smask, _dot_tn(vb[j], k_end[j]), 0.0) for j in chunks]
    s_t = s_scr[...]
    s_start = [None] * n_ch
    for j in (chunks if fwd else reversed(chunks)):
        s_start[j] = s_t.astype(BF16)
        s_t = s_t * decay[j] + upd[j]
    s_scr[...] = s_t
    for j in chunks:
        o_ref[rows[j], :] = o_intra[j] + _dot_nt(q_e[j], s_start[j])


def _gla_kernel(n_ch, dkw, dvw, pg_ref, wg_ref, bg_ref, s0_ref, o_ref, sfin_ref, s_scr):
    d = pl.program_id(1)
    i = pl.program_id(2)

    @pl.when(i == 0)
    def _():
        s_scr[...] = s0_ref[...]

    @pl.when(d == 0)
    def _():
        _gla_block(True, n_ch, dkw, dvw, pg_ref, wg_ref, bg_ref, o_ref, s_scr)

    @pl.when(d != 0)
    def _():
        _gla_block(False, n_ch, dkw, dvw, pg_ref, wg_ref, bg_ref, o_ref, s_scr)

    @pl.when(i == pl.num_programs(2) - 1)
    def _():
        sfin_ref[...] = s_scr[...]


def _gla_call(pg, wg2, bg2, s0, dkw, dvw):
    b, t, w = pg.shape
    blk = min(GLA_BLOCK, t)
    nb = t // blk
    n_ch = blk // GLA_CHUNK
    seq = lambda bi, d, i: (bi, jnp.where(d == 0, i, nb - 1 - i), 0)
    return pl.pallas_call(
        functools.partial(_gla_kernel, n_ch, dkw, dvw),
        grid=(b, 2, nb),
        in_specs=[pl.BlockSpec((None, blk, w), seq),
                  pl.BlockSpec((None, LANES, dkw), lambda bi, d, i: (d, 0, 0)),
                  pl.BlockSpec((None, 1, dkw), lambda bi, d, i: (d, 0, 0)),
                  pl.BlockSpec((None, None, dvw, dkw), lambda bi, d, i: (bi, d, 0, 0))],
        out_specs=[pl.BlockSpec((None, None, blk, dvw),
                                lambda bi, d, i: (d, bi, jnp.where(d == 0, i, nb - 1 - i), 0)),
                   pl.BlockSpec((None, None, dvw, dkw), lambda bi, d, i: (bi, d, 0, 0))],
        out_shape=[jax.ShapeDtypeStruct((2, b, t, dvw), F32),
                   jax.ShapeDtypeStruct((b, 2, dvw, dkw), F32)],
        scratch_shapes=[pltpu.VMEM((dvw, dkw), F32)],
        compiler_params=_cparams(3),
        name="gla",
    )(pg, wg2, bg2, s0)


def _kvprep_kernel(kv_ref, cos_ref, sin_ref, g_ref, gm_ref, k_ref, vt_ref):
    hd = HEAD_DIM
    kk = kv_ref[:, 0:LANES]
    vv = kv_ref[:, LANES:2 * LANES]
    ms = _split_dot(kk * kk, gm_ref[...])
    kn = kk * lax.rsqrt(ms + EPS) * g_ref[...]
    kr = kn * cos_ref[...] + _rope_partner(kn) * sin_ref[...]
    lane = lax.broadcasted_iota(jnp.int32, kr.shape, 1)
    lo = lane < hd
    kswap = pltpu.roll(kr, hd, axis=1)
    k0 = jnp.where(lo, kr, kswap).astype(BF16)
    k1 = jnp.where(lo, kswap, kr).astype(BF16)
    k_ref[...] = jnp.concatenate([k0, k0, k1, k1], axis=1)
    vt_ref[0] = jnp.where(lo, vv, 1.0).T.astype(BF16)
    vt_ref[1] = jnp.where(lo, pltpu.roll(vv, hd, axis=1), 1.0).T.astype(BF16)


def _kvprep_call(pa, cos_t, sin_t, kg, gm, tt):
    b, t, _ = pa.shape
    return pl.pallas_call(
        _kvprep_kernel,
        grid=(b, t // tt),
        in_specs=[pl.BlockSpec((None, tt, 2 * LANES), lambda bi, i: (bi, i, 2)),
                  pl.BlockSpec((tt, LANES), lambda bi, i: (i, 0)),
                  pl.BlockSpec((tt, LANES), lambda bi, i: (i, 0)),
                  _const_spec(kg.shape), _const_spec(gm.shape)],
        out_specs=[pl.BlockSpec((None, tt, 4 * LANES), lambda bi, i: (bi, i, 0)),
                   pl.BlockSpec((None, N_KV_HEADS, LANES, tt), lambda bi, i: (bi, 0, 0, i))],
        out_shape=[jax.ShapeDtypeStruct((b, t, 4 * LANES), BF16),
                   jax.ShapeDtypeStruct((b, N_KV_HEADS, LANES, t), BF16)],
        compiler_params=_cparams(2),
        name="kvprep",
    )(pa, cos_t, sin_t, kg, gm)


def _flash_kernel(seg_lens, q_ref, cos_ref, sin_ref, g_ref, gm_ref, *rest):
    n_seg = len(seg_lens)
    kv_refs = rest[:2 * n_seg]
    o_ref, qs_scr = rest[2 * n_seg:]
    hd = HEAD_DIM
    g_heads = 2 * LANES // hd
    q = q_ref[...]
    ms = _split_dot(q * q, gm_ref[...])
    qn = q * lax.rsqrt(ms + EPS) * g_ref[...]
    qr = (qn * cos_ref[...] + _rope_partner(qn) * sin_ref[...]) * (hd ** -0.5 * LOG2E)
    lane_head = lax.broadcasted_iota(jnp.int32, qr.shape, 1) // hd
    for h in range(g_heads):
        qs_scr[h] = jnp.where(lane_head == h, qr, 0.0).astype(BF16)

    tasks = []
    for h in range(g_heads):
        first = True
        for si in range(n_seg):
            k_ref, vt_ref = kv_refs[2 * si], kv_refs[2 * si + 1]
            tk = min(ATT_TK, seg_lens[si])
            for c0 in range(0, seg_lens[si], tk):
                tasks.append((h, k_ref, vt_ref, c0, tk, first))
                first = False

    def scores(task):
        h, k_ref, _, c0, tk, _ = task
        return _dot_nt(k_ref[c0:c0 + tk, :], qs_scr[h])

    m_run = [None] * g_heads
    acc = [None] * g_heads
    pending = [scores(t) for t in tasks[:ATT_LOOKAHEAD]]
    for n, (h, _, vt_ref, c0, tk, first) in enumerate(tasks):
        if n + ATT_LOOKAHEAD < len(tasks):
            pending.append(scores(tasks[n + ATT_LOOKAHEAD]))
        s = pending.pop(0)
        m_cur = jnp.max(s, axis=0, keepdims=True)
        if first:
            m_new = m_cur
            p = jnp.exp2(s - m_new).astype(BF16)
            acc[h] = _dot(vt_ref[:, c0:c0 + tk], p)
        else:
            m_new = jnp.maximum(m_run[h], m_cur)
            p = jnp.exp2(s - m_new).astype(BF16)
            acc[h] = acc[h] * jnp.exp2(m_run[h] - m_new) + _dot(vt_ref[:, c0:c0 + tk], p)
        m_run[h] = m_new

    o_t = jnp.concatenate([acc[h][0:hd, :] * (1.0 / acc[h][hd:hd + 1, :]) for h in range(g_heads)],
                          axis=0)
    o_ref[...] = o_t.T


def _flash_call(pa, cos_q, sin_q, qg, gm, segs, tq):
    b, t, _ = pa.shape
    tq = min(tq, t)
    seg_lens = tuple(k.shape[1] for k, _ in segs)
    in_specs = [pl.BlockSpec((None, tq, 2 * LANES), lambda bi, j, i: (bi, i, j)),
                pl.BlockSpec((tq, 2 * LANES), lambda bi, j, i: (i, 0)),
                pl.BlockSpec((tq, 2 * LANES), lambda bi, j, i: (i, 0)),
                _const_spec(qg.shape), _const_spec(gm.shape)]
    args = [pa, cos_q, sin_q, qg, gm]
    for k, v in segs:
        in_specs.append(pl.BlockSpec((None, k.shape[1], 2 * LANES), lambda bi, j, i: (bi, 0, j)))
        in_specs.append(pl.BlockSpec((None, None, LANES, v.shape[3]), lambda bi, j, i: (bi, j, 0, 0)))
        args += [k, v]
    g_heads = 2 * LANES // HEAD_DIM
    return pl.pallas_call(
        functools.partial(_flash_kernel, seg_lens),
        grid=(b, N_KV_HEADS, t // tq),
        in_specs=in_specs,
        out_specs=pl.BlockSpec((None, tq, 2 * LANES), lambda bi, j, i: (bi, i, j)),
        out_shape=jax.ShapeDtypeStruct((b, t, N_KV_HEADS * 2 * LANES), F32),
        scratch_shapes=[pltpu.VMEM((g_heads, tq, 2 * LANES), BF16)],
        compiler_params=_cparams(3),
        name="flash",
    )(*args)


def _out_kernel(cw, dvw, x_ref, mod_ref, conv_ref, of_ref, ob_ref, r_ref, att_ref, gg_ref, gm_ref,
                wo_ref, o_ref):
    o = of_ref[...] + ob_ref[...]
    ms = _split_dot(o * o, gm_ref[...])
    gl = (o * lax.rsqrt(ms + EPS) * gg_ref[...]) * _silu(r_ref[...])
    y = (_dot(conv_ref[...].astype(BF16), wo_ref[0:cw, :])
         + _dot(gl.astype(BF16), wo_ref[cw:cw + dvw, :])
         + _dot(att_ref[...].astype(BF16), wo_ref[cw + dvw:, :]))
    o_ref[...] = x_ref[...] + mod_ref[5:6, :] * y


def _out_call(x, mod_l, mod_row, conv, o_gla, pg, att, gg, gm, wo, tm):
    b, t, d = x.shape
    cw = conv.shape[2]
    dvw = o_gla.shape[3]
    aw = att.shape[2]
    r_blk = (pg.shape[2] - dvw) // dvw
    tok = lambda bi, i: (bi, i, 0)
    del r_blk
    return pl.pallas_call(
        functools.partial(_out_kernel, cw, dvw),
        grid=(b, t // tm),
        in_specs=[pl.BlockSpec((None, tm, d), tok),
                  pl.BlockSpec((None, N_MOD, d), lambda bi, i: (mod_row(bi), 0, 0)),
                  pl.BlockSpec((None, tm, cw), tok),
                  pl.BlockSpec((None, None, tm, dvw), lambda bi, i: (0, bi, i, 0)),
                  pl.BlockSpec((None, None, tm, dvw), lambda bi, i: (1, bi, i, 0)),
                  pl.BlockSpec((None, tm, dvw), lambda bi, i: (bi, i, 2)),
                  pl.BlockSpec((None, tm, aw), tok),
                  _const_spec(gg.shape), _const_spec(gm.shape), _const_spec(wo.shape)],
        out_specs=pl.BlockSpec((None, tm, d), tok),
        out_shape=jax.ShapeDtypeStruct(x.shape, F32),
        compiler_params=_cparams(2),
        name="outproj",
    )(x, mod_l, conv, o_gla, o_gla, pg, att, gg, gm, wo)


def _rope_tables(s):
    nf = HEAD_DIM // 4
    rows_n = s // GRID_W
    row = jnp.repeat(jnp.arange(rows_n), GRID_W).astype(F32)
    col = jnp.tile(jnp.arange(GRID_W), rows_n).astype(F32)
    freqs = ROPE_BASE ** (-jnp.arange(nf, dtype=F32) / nf)
    ang_r = row[:, None] * freqs
    ang_c = col[:, None] * freqs
    cos = jnp.concatenate([jnp.cos(ang_r)] * 2 + [jnp.cos(ang_c)] * 2, axis=1)
    sin = jnp.concatenate([-jnp.sin(ang_r), jnp.sin(ang_r), -jnp.sin(ang_c), jnp.sin(ang_c)], axis=1)
    return cos, sin


def kernel(x, c, ctx, c_ctx, w_ada, b_ada, g_norm, w_ffn_in, w_ffn_out, w_in, w_out, w_dw, b_dw,
           conv_norm_g, conv_norm_b, w_gla_gate, b_gla_gate, gla_norm_g, q_norm_g, k_norm_g):
    b, s, d = x.shape
    l = ctx.shape[1]
    depth = w_ada.shape[0]
    d_ff = w_ffn_out.shape[2]
    cw = w_dw.shape[2]
    dkw = w_gla_gate.shape[3]
    dvw = gla_norm_g.shape[1] * gla_norm_g.shape[2]
    n_kv = N_KV_HEADS * HEAD_DIM
    aw = d - cw - dvw
    assert d_ff % FFN_CHUNK == 0 and s % TOKEN_TILE == 0 and l % GLA_CHUNK == 0
    assert dkw == LANES and n_kv == LANES and cw == 2 * LANES and dvw == 2 * LANES and aw == 4 * LANES

    rows = -(-(b + 1) // SUBLANES) * SUBLANES
    cc = jnp.zeros((rows, d), F32).at[:b].set(c).at[b].set(c_ctx)
    mod = jnp.transpose(_mod_call(cc, w_ada, b_ada), (0, 2, 1, 3))
    row_x = lambda bi: bi
    row_c = lambda bi: b

    cos64, sin64 = _rope_tables(s)
    cos_k, sin_k = jnp.tile(cos64, (1, 2)), jnp.tile(sin64, (1, 2))
    cos_q, sin_q = jnp.tile(cos64, (1, 4)), jnp.tile(sin64, (1, 4))
    one_k, zero_k = jnp.ones((l, LANES), F32), jnp.zeros((l, LANES), F32)
    one_q, zero_q = jnp.ones((l, 2 * LANES), F32), jnp.zeros((l, 2 * LANES), F32)

    gm64_128 = _group_mean_matrix(LANES, HEAD_DIM)
    gm64_256 = _group_mean_matrix(2 * LANES, HEAD_DIM)
    gm_conv = _group_mean_matrix(cw, cw // CONV_GROUPS)
    gm_gla = _group_mean_matrix(dvw, dvw // GLA_HEADS)

    sizes = (cw, cw, dkw, dkw, dvw, dvw, GLA_RANK, GLA_RANK, aw, n_kv, n_kv)
    offs = np.concatenate([[0], np.cumsum(sizes)])
    n_chunks = d_ff // FFN_CHUNK
    tm_c = min(TOKEN_TILE, l)

    for i in range(depth):
        last = i == depth - 1
        mod_l = mod[i]
        g_l = g_norm[i].reshape(3, 1, d)
        w1 = w_ffn_in[i].astype(BF16)
        w1a = w1[:, :, :d_ff].reshape(2, d, n_chunks, FFN_CHUNK).transpose(0, 2, 1, 3)
        w1u = w1[:, :, d_ff:].reshape(2, d, n_chunks, FFN_CHUNK).transpose(0, 2, 1, 3)
        w2 = w_ffn_out[i].astype(BF16).reshape(2, n_chunks, FFN_CHUNK, d)
        wi = w_in[i].astype(BF16)
        wc = wi[:, offs[0]:offs[2]]
        wgl = jnp.concatenate([wi[:, offs[2]:offs[8]],
                               jnp.zeros((d, LANES - 2 * GLA_RANK), BF16)], axis=1)
        wat = wi[:, offs[8]:offs[11]]
        wo = w_out[i].astype(BF16)
        wg2 = jnp.zeros((2, LANES, dkw), F32)
        wg2 = wg2.at[0, 0:GLA_RANK].set(w_gla_gate[i, 0]).at[1, GLA_RANK:2 * GLA_RANK].set(w_gla_gate[i, 1])
        bg2 = b_gla_gate[i].reshape(2, 1, dkw)
        qg = jnp.tile(q_norm_g[i], 4).reshape(1, 2 * LANES)
        kg = jnp.tile(k_norm_g[i], 2).reshape(1, LANES)
        gg = gla_norm_g[i].reshape(1, dvw)
        wdw, bdw = w_dw[i], b_dw[i].reshape(1, cw)
        cng, cnb = conv_norm_g[i].reshape(1, cw), conv_norm_b[i].reshape(1, cw)

        x = _ffn_call(x, mod_l, row_x, g_l[0], w1a[0], w1u[0], w2[0], 0, TOKEN_TILE)
        ctx = _ffn_call(ctx, mod_l, row_c, g_l[0], w1a[0], w1u[0], w2[0], 0, tm_c)
        pc_x, pg_x, pa_x = _inproj_call(x, mod_l, row_x, g_l[1], wc, wgl, wat, TOKEN_TILE)
        pc_c, pg_c, pa_c = _inproj_call(ctx, mod_l, row_c, g_l[1], wc, wgl, wat, tm_c)
        conv_x = _conv_call(pc_x, wdw, bdw, cng, cnb, gm_conv)
        o_c, s_c = _gla_call(pg_c, wg2, bg2, jnp.zeros((b, 2, dvw, dkw), F32), dkw, dvw)
        o_x, _ = _gla_call(pg_x, wg2, bg2, s_c, dkw, dvw)
        k_c, v_c = _kvprep_call(pa_c, one_k, zero_k, kg, gm64_128, tm_c)
        k_x, v_x = _kvprep_call(pa_x, cos_k, sin_k, kg, gm64_128, TOKEN_TILE)
        att_x = _flash_call(pa_x, cos_q, sin_q, qg, gm64_256, [(k_c, v_c), (k_x, v_x)], ATT_TQ)
        x = _out_call(x, mod_l, row_x, conv_x, o_x, pg_x, att_x, gg, gm_gla, wo, TOKEN_TILE)
        x = _ffn_call(x, mod_l, row_x, g_l[2], w1a[1], w1u[1], w2[1], 2, TOKEN_TILE)
        if not last:
            conv_c = _conv_call(pc_c, wdw, bdw, cng, cnb, gm_conv)
            att_c = _flash_call(pa_c, one_q, zero_q, qg, gm64_256, [(k_c, v_c)], ATT_TQ)
            ctx = _out_call(ctx, mod_l, row_c, conv_c, o_c, pg_c, att_c, gg, gm_gla, wo, tm_c)
            ctx = _ffn_call(ctx, mod_l, row_c, g_l[2], w1a[1], w1u[1], w2[1], 2, tm_c)
    return x
```

```python
import functools

import numpy as np
import jax
import jax.numpy as jnp
from jax import lax
from jax.experimental import pallas as pl
from jax.experimental.pallas import tpu as pltpu

F32 = jnp.float32
BF16 = jnp.bfloat16

CONV_GROUPS = 4
CONV_K = 31
GLA_HEADS = 4
GLA_RANK = 16
GLA_GATE_NORM = 16.0
GLA_CHUNK = 64
HEAD_DIM = 64
N_KV_HEADS = 2
GRID_W = 64
ROPE_BASE = 10000.0
N_MOD = 9
EPS = 1e-6
LOG2E = 1.4426950408889634

LANES = 128
SUBLANES = 8
VMEM_LIMIT_BYTES = 56 * 1024 * 1024

FFN_CHUNK = 256
TOKEN_TILE = 512
CONV_ROWS = 32
CONV_HALO = 16
CONV_NORM_ROWS = 1024
GLA_BLOCK = 512
ATT_TQ = 512
VT_ROWS = 80
ATT_TK = 512
ATT_LOOKAHEAD = 4


def _cparams(n_axes):
    return pltpu.CompilerParams(dimension_semantics=("arbitrary",) * n_axes,
                                vmem_limit_bytes=VMEM_LIMIT_BYTES)


def _const_spec(shape):
    nd = len(shape)
    return pl.BlockSpec(shape, lambda *_: (0,) * nd, pipeline_mode=pl.Buffered(1))


def _layer_spec(arr, *idx):
    lead = len(idx)
    blk = (None,) * lead + tuple(arr.shape[lead:])
    tail = (0,) * (arr.ndim - lead)
    return pl.BlockSpec(blk, lambda *_: tuple(idx) + tail, pipeline_mode=pl.Buffered(1))


def _dot(a, b):
    return jnp.dot(a, b, preferred_element_type=F32)


def _dot_nt(a, b):
    return lax.dot_general(a, b, (((1,), (1,)), ((), ())), preferred_element_type=F32)


def _dot_tn(a, b):
    return lax.dot_general(a, b, (((0,), (0,)), ((), ())), preferred_element_type=F32)


def _sigmoid(x):
    return 1.0 / (1.0 + jnp.exp(-x))


def _silu(x):
    return x * _sigmoid(x)


def _split_dot(v, m):
    hi = v.astype(BF16)
    lo = (v - hi.astype(F32)).astype(BF16)
    return _dot(hi, m) + _dot(lo, m)


def _split3_dot_left(m, v):
    hi = v.astype(BF16)
    r1 = v - hi.astype(F32)
    mid = r1.astype(BF16)
    lo = (r1 - mid.astype(F32)).astype(BF16)
    return _dot(m, hi) + _dot(m, mid) + _dot(m, lo)


def _pre(x, g, shift, scale):
    r = lax.rsqrt(jnp.mean(x * x, axis=-1, keepdims=True) + EPS)
    return (x * r * g) * (1.0 + scale) + shift


def _group_mean_matrix(width, group):
    idx = np.arange(width) // group
    return jnp.asarray((idx[:, None] == idx[None, :]).astype(np.float32) / group, dtype=BF16)


def _rope_partner(x):
    n = x.shape[-1]
    lane = lax.broadcasted_iota(jnp.int32, x.shape, x.ndim - 1)
    up = pltpu.roll(x, n - 16, axis=x.ndim - 1)
    dn = pltpu.roll(x, 16, axis=x.ndim - 1)
    return jnp.where((lane % 32) < 16, up, dn)


def _mod_kernel(c_ref, w_ref, b_ref, o_ref):
    sc = _silu(c_ref[...]).astype(BF16)
    o_ref[...] = _dot(sc, w_ref[...].astype(BF16)) + b_ref[...]


def _mod_call(cc, w_ada, b_ada):
    depth, d, _ = w_ada.shape
    rows = cc.shape[0]
    return pl.pallas_call(
        _mod_kernel,
        grid=(depth, N_MOD),
        in_specs=[pl.BlockSpec((rows, d), lambda i, j: (0, 0)),
                  pl.BlockSpec((None, d, d), lambda i, j: (i, 0, j)),
                  pl.BlockSpec((None, 1, d), lambda i, j: (i, 0, j))],
        out_specs=pl.BlockSpec((None, None, rows, d), lambda i, j: (i, j, 0, 0)),
        out_shape=jax.ShapeDtypeStruct((depth, N_MOD, rows, d), F32),
        compiler_params=_cparams(2),
        name="mod",
    )(cc, w_ada, b_ada.reshape(depth, 1, N_MOD * d))


def _ffn_half(x, mod_ref, j, g, w1_ref, w2_ref, acc_ref):
    d_ff = w2_ref.shape[0]
    shift = mod_ref[3 * j:3 * j + 1, :]
    scale = mod_ref[3 * j + 1:3 * j + 2, :]
    gate = mod_ref[3 * j + 2:3 * j + 3, :]
    h = _pre(x, g, shift, scale).astype(BF16)
    for c0 in range(0, d_ff, FFN_CHUNK):
        a = _dot(h, w1_ref[:, c0:c0 + FFN_CHUNK])
        u = _dot(h, w1_ref[:, d_ff + c0:d_ff + c0 + FFN_CHUNK])
        part = _dot((_silu(a) * u).astype(BF16), w2_ref[c0:c0 + FFN_CHUNK, :])
        if c0 == 0:
            acc_ref[...] = part
        else:
            acc_ref[...] += part
    return x + (0.5 * gate) * acc_ref[...]


def _ffn_in_kernel(x_ref, mod_ref, g_ref, w1_ref, w2_ref, wc_ref, wg_ref, wa_ref,
                   xo_ref, pc_ref, pg_ref, pa_ref, acc_ref):
    x1 = _ffn_half(x_ref[...], mod_ref, 0, g_ref[0], w1_ref, w2_ref, acc_ref)
    xo_ref[...] = x1
    h = _pre(x1, g_ref[1], mod_ref[3:4, :], mod_ref[4:5, :]).astype(BF16)
    pc_ref[...] = _dot(h, wc_ref[...])
    pg_ref[...] = _dot(h, wg_ref[...])
    pa_ref[...] = _dot(h, wa_ref[...])


def _ffn_in_call(x, mod_l, mod_row, g_l, w1_all, w2_all, wc_all, wg_all, wa_all, li, tm):
    b, t, d = x.shape
    tok = lambda bi, i: (bi, i, 0)
    widths = (wc_all.shape[2], wg_all.shape[2], wa_all.shape[2])
    return pl.pallas_call(
        _ffn_in_kernel,
        grid=(b, t // tm),
        in_specs=[pl.BlockSpec((None, tm, d), tok),
                  pl.BlockSpec((None, N_MOD, d), lambda bi, i: (mod_row(bi), 0, 0)),
                  _const_spec(g_l.shape), _layer_spec(w1_all, li, 0), _layer_spec(w2_all, li, 0),
                  _layer_spec(wc_all, li), _layer_spec(wg_all, li), _layer_spec(wa_all, li)],
        out_specs=[pl.BlockSpec((None, tm, d), tok)] + [pl.BlockSpec((None, tm, w), tok) for w in widths],
        out_shape=[jax.ShapeDtypeStruct(x.shape, F32)]
        + [jax.ShapeDtypeStruct((b, t, w), F32) for w in widths],
        scratch_shapes=[pltpu.VMEM((tm, d), F32)],
        compiler_params=_cparams(2),
        name="ffn_in",
    )(x, mod_l, g_l, w1_all, w2_all, wc_all, wg_all, wa_all)


def _conv_kernel(t, cw, pc_ref, wdw_ref, bdw_ref, cg_ref, cb_ref, gm_ref, o_ref, hbuf, h2_scr):
    r = CONV_ROWS
    halo = CONV_HALO
    hbuf[0:halo, :] = jnp.zeros((halo, cw), F32)
    hbuf[t + halo:t + 2 * halo, :] = jnp.zeros((halo, cw), F32)

    def glu(i, carry):
        r0 = pl.multiple_of(i * r, r)
        a = pc_ref[pl.ds(r0, r), 0:cw]
        gt = pc_ref[pl.ds(r0, r), cw:2 * cw]
        hbuf[pl.ds(r0 + halo, r), :] = a * _sigmoid(gt)
        return carry

    lax.fori_loop(0, t // r, glu, 0)
    gm = gm_ref[...]

    def tile(i, carry):
        r0 = pl.multiple_of(i * r, r)
        win = hbuf[pl.ds(r0, r + 2 * halo), :]
        acc = jnp.zeros((r, cw), F32)
        for res in range(SUBLANES):
            wr = win if res == 0 else pltpu.roll(win, r + 2 * halo - res, axis=0)
            for a in range(2 * halo // SUBLANES):
                s = SUBLANES * a + res
                if 1 <= s <= CONV_K:
                    acc = acc + wr[SUBLANES * a:SUBLANES * a + r, :] * wdw_ref[s - 1:s, :]
        h2_scr[pl.ds(r0, r), :] = acc + bdw_ref[...]
        return carry

    lax.fori_loop(0, t // r, tile, 0, unroll=2)

    nb = min(CONV_NORM_ROWS, t)

    def norm(i, carry):
        rows = pl.ds(pl.multiple_of(i * nb, nb), nb)
        h2 = h2_scr[rows, :]
        mu = _split_dot(h2, gm)
        dlt = h2 - mu
        var = _split_dot(dlt * dlt, gm)
        y = dlt * lax.rsqrt(var + EPS) * cg_ref[...] + cb_ref[...]
        o_ref[rows, :] = _silu(y).astype(o_ref.dtype)
        return carry

    lax.fori_loop(0, t // nb, norm, 0)


def _conv_call(pc, w_dw, b_dw, cn_g, cn_b, gm):
    b, t, two_cw = pc.shape
    cw = two_cw // 2
    return pl.pallas_call(
        functools.partial(_conv_kernel, t, cw),
        grid=(b,),
        in_specs=[pl.BlockSpec((None, t, two_cw), lambda bi: (bi, 0, 0)),
                  _const_spec(w_dw.shape), _const_spec(b_dw.shape), _const_spec(cn_g.shape),
                  _const_spec(cn_b.shape), _const_spec(gm.shape)],
        out_specs=pl.BlockSpec((None, t, cw), lambda bi: (bi, 0, 0)),
        out_shape=jax.ShapeDtypeStruct((b, t, cw), BF16),
        scratch_shapes=[pltpu.VMEM((t + 2 * CONV_HALO, cw), F32), pltpu.VMEM((t, cw), F32)],
        compiler_params=_cparams(1),
        name="conv",
    )(pc, w_dw, b_dw, cn_g, cn_b, gm)


def _gla_block(fwd, n_ch, dkw, dvw, pg_ref, wg_ref, bg_ref, o_ref, s_scr):
    c = GLA_CHUNK
    dk = dkw // GLA_HEADS
    dv = dvw // GLA_HEADS
    lr_off = 2 * dkw + 2 * dvw
    z = _dot(pg_ref[:, lr_off:lr_off + LANES], wg_ref[...]) + bg_ref[...]
    la_all = -(jnp.maximum(-z, 0.0) + jnp.log(1.0 + jnp.exp(-jnp.abs(z)))) * (1.0 / GLA_GATE_NORM)

    ri = lax.broadcasted_iota(jnp.int32, (c, c), 0)
    ci = lax.broadcasted_iota(jnp.int32, (c, c), 1)
    tri_m = jnp.where((ci <= ri) if fwd else (ci >= ri), 1.0, 0.0).astype(BF16)
    rih = lax.broadcasted_iota(jnp.int32, (c, GLA_HEADS * c), 0)
    cih = lax.broadcasted_iota(jnp.int32, (c, GLA_HEADS * c), 1) % c
    tri_h = (cih <= rih) if fwd else (cih >= rih)
    klane = lax.broadcasted_iota(jnp.int32, (c, dkw), 1) // dk
    vlane = lax.broadcasted_iota(jnp.int32, (c, dvw), 1) // dv
    srow = lax.broadcasted_iota(jnp.int32, (dvw, dkw), 0) // dv
    scol = lax.broadcasted_iota(jnp.int32, (dvw, dkw), 1) // dk
    smask = srow == scol

    chunks = range(n_ch)
    rows = [slice(j * c, (j + 1) * c) for j in chunks]
    bcum = [_split3_dot_left(tri_m, la_all[rows[j], :]) for j in chunks]
    q_e, k_end, kblk, vblk, vb, decay = [], [], [], [], [], []
    for j in chunks:
        q = pg_ref[rows[j], 0:dkw] * (dk ** -0.5)
        k = pg_ref[rows[j], dkw:2 * dkw]
        v = pg_ref[rows[j], 2 * dkw:2 * dkw + dvw]
        btot = bcum[j][c - 1:c, :] if fwd else bcum[j][0:1, :]
        q_e.append((q * jnp.exp(bcum[j])).astype(BF16))
        k_e = k * jnp.exp(-bcum[j])
        k_end.append((k * jnp.exp(btot - bcum[j])).astype(BF16))
        kblk.append(jnp.concatenate([jnp.where(klane == h, k_e, 0.0) for h in range(GLA_HEADS)],
                                    axis=0).astype(BF16))
        vblk.append(jnp.concatenate([jnp.where(vlane == h, v, 0.0) for h in range(GLA_HEADS)],
                                    axis=0).astype(BF16))
        vb.append(v.astype(BF16))
        decay.append(jnp.exp(btot))
    att = [jnp.where(tri_h, _dot_nt(q_e[j], kblk[j]), 0.0).astype(BF16) for j in chunks]
    o_intra = [_dot(att[j], vblk[j]) for j in chunks]
    upd = [jnp.where(smask, _dot_tn(vb[j], k_end[j]), 0.0) for j in chunks]
    s_t = s_scr[...]
    s_start = [None] * n_ch
    for j in (chunks if fwd else reversed(chunks)):
        s_start[j] = s_t.astype(BF16)
        s_t = s_t * decay[j] + upd[j]
    s_scr[...] = s_t
    for j in chunks:
        o_ref[rows[j], :] = o_intra[j] + _dot_nt(q_e[j], s_start[j])


def _gla_kernel(n_ch, dkw, dvw, pg_ref, wg_ref, bg_ref, s0_ref, o_ref, sfin_ref, s_scr):
    d = pl.program_id(1)
    i = pl.program_id(2)

    @pl.when(i == 0)
    def _():
        s_scr[...] = s0_ref[...]

    @pl.when(d == 0)
    def _():
        _gla_block(True, n_ch, dkw, dvw, pg_ref, wg_ref, bg_ref, o_ref, s_scr)

    @pl.when(d != 0)
    def _():
        _gla_block(False, n_ch, dkw, dvw, pg_ref, wg_ref, bg_ref, o_ref, s_scr)

    @pl.when(i == pl.num_programs(2) - 1)
    def _():
        sfin_ref[...] = s_scr[...]


def _gla_call(pg, wg2, bg2, s0, dkw, dvw):
    b, t, w = pg.shape
    blk = min(GLA_BLOCK, t)
    nb = t // blk
    n_ch = blk // GLA_CHUNK
    seq = lambda bi, d, i: (bi, jnp.where(d == 0, i, nb - 1 - i), 0)
    return pl.pallas_call(
        functools.partial(_gla_kernel, n_ch, dkw, dvw),
        grid=(b, 2, nb),
        in_specs=[pl.BlockSpec((None, blk, w), seq),
                  pl.BlockSpec((None, LANES, dkw), lambda bi, d, i: (d, 0, 0)),
                  pl.BlockSpec((None, 1, dkw), lambda bi, d, i: (d, 0, 0)),
                  pl.BlockSpec((None, None, dvw, dkw), lambda bi, d, i: (bi, d, 0, 0))],
        out_specs=[pl.BlockSpec((None, None, blk, dvw),
                                lambda bi, d, i: (d, bi, jnp.where(d == 0, i, nb - 1 - i), 0)),
                   pl.BlockSpec((None, None, dvw, dkw), lambda bi, d, i: (bi, d, 0, 0))],
        out_shape=[jax.ShapeDtypeStruct((2, b, t, dvw), F32),
                   jax.ShapeDtypeStruct((b, 2, dvw, dkw), F32)],
        scratch_shapes=[pltpu.VMEM((dvw, dkw), F32)],
        compiler_params=_cparams(3),
        name="gla",
    )(pg, wg2, bg2, s0)


def _kvprep_kernel(aw, pa_ref, cos_ref, sin_ref, qg_ref, kg_ref, gm_ref, q_ref, k_ref, vt_ref):
    hd = HEAD_DIM
    cos = cos_ref[...]
    sin = sin_ref[...]

    def norm_rope(x, g):
        ms = _split_dot(x * x, gm_ref[...])
        xn = x * lax.rsqrt(ms + EPS) * g
        return xn * cos + _rope_partner(xn) * sin

    for t0 in range(0, aw, LANES):
        q_ref[:, t0:t0 + LANES] = (norm_rope(pa_ref[:, t0:t0 + LANES], qg_ref[...])
                                   * (hd ** -0.5 * LOG2E)).astype(BF16)
    vv = pa_ref[:, aw + LANES:aw + 2 * LANES]
    kr = norm_rope(pa_ref[:, aw:aw + LANES], kg_ref[...])
    lane = lax.broadcasted_iota(jnp.int32, kr.shape, 1)
    lo = lane < hd
    kswap = pltpu.roll(kr, hd, axis=1)
    k0 = jnp.where(lo, kr, kswap).astype(BF16)
    k1 = jnp.where(lo, kswap, kr).astype(BF16)
    k_ref[...] = jnp.concatenate([k0, k0, k1, k1], axis=1)
    vt_ref[0] = jnp.where(lo, vv, 1.0).T[0:VT_ROWS, :].astype(BF16)
    vt_ref[1] = jnp.where(lo, pltpu.roll(vv, hd, axis=1), 1.0).T[0:VT_ROWS, :].astype(BF16)


def _kvprep_call(pa, cos_t, sin_t, qg, kg, gm, tt):
    b, t, w = pa.shape
    aw = w - 2 * LANES
    return pl.pallas_call(
        functools.partial(_kvprep_kernel, aw),
        grid=(b, t // tt),
        in_specs=[pl.BlockSpec((None, tt, w), lambda bi, i: (bi, i, 0)),
                  pl.BlockSpec((tt, LANES), lambda bi, i: (i, 0)),
                  pl.BlockSpec((tt, LANES), lambda bi, i: (i, 0)),
                  _const_spec(qg.shape), _const_spec(kg.shape), _const_spec(gm.shape)],
        out_specs=[pl.BlockSpec((None, tt, aw), lambda bi, i: (bi, i, 0)),
                   pl.BlockSpec((None, tt, 4 * LANES), lambda bi, i: (bi, i, 0)),
                   pl.BlockSpec((None, N_KV_HEADS, VT_ROWS, tt), lambda bi, i: (bi, 0, 0, i))],
        out_shape=[jax.ShapeDtypeStruct((b, t, aw), BF16),
                   jax.ShapeDtypeStruct((b, t, 4 * LANES), BF16),
                   jax.ShapeDtypeStruct((b, N_KV_HEADS, VT_ROWS, t), BF16)],
        compiler_params=_cparams(2),
        name="kvprep",
    )(pa, cos_t, sin_t, qg, kg, gm)


def _flash_kernel(seg_lens, q_ref, *rest):
    n_seg = len(seg_lens)
    kv_refs = rest[:2 * n_seg]
    o_ref, qs_scr = rest[2 * n_seg:]
    hd = HEAD_DIM
    g_heads = 2 * LANES // hd
    qr = q_ref[...]
    lane_head = lax.broadcasted_iota(jnp.int32, qr.shape, 1) // hd
    for h in range(g_heads):
        qs_scr[h] = jnp.where(lane_head == h, qr, jnp.zeros_like(qr))

    tasks = []
    for h in range(g_heads):
        first = True
        for si in range(n_seg):
            k_ref, vt_ref = kv_refs[2 * si], kv_refs[2 * si + 1]
            tk = min(ATT_TK, seg_lens[si])
            for c0 in range(0, seg_lens[si], tk):
                tasks.append((h, k_ref, vt_ref, c0, tk, first))
                first = False

    def scores(task):
        h, k_ref, _, c0, tk, _ = task
        return _dot_nt(k_ref[c0:c0 + tk, :], qs_scr[h])

    m_run = [None] * g_heads
    acc = [None] * g_heads
    pending = [scores(t) for t in tasks[:ATT_LOOKAHEAD]]
    for n, (h, _, vt_ref, c0, tk, first) in enumerate(tasks):
        if n + ATT_LOOKAHEAD < len(tasks):
            pending.append(scores(tasks[n + ATT_LOOKAHEAD]))
        s = pending.pop(0)
        m_cur = jnp.max(s, axis=0, keepdims=True)
        if first:
            m_new = m_cur
            p = jnp.exp2(s - m_new).astype(BF16)
            acc[h] = _dot(vt_ref[:, c0:c0 + tk], p)
        else:
            m_new = jnp.maximum(m_run[h], m_cur)
            p = jnp.exp2(s - m_new).astype(BF16)
            acc[h] = acc[h] * jnp.exp2(m_run[h] - m_new) + _dot(vt_ref[:, c0:c0 + tk], p)
        m_run[h] = m_new

    o_t = jnp.concatenate([acc[h][0:hd, :] * (1.0 / acc[h][hd:hd + 1, :]) for h in range(g_heads)],
                          axis=0)
    o_ref[...] = o_t.T.astype(o_ref.dtype)


def _flash_call(qr, segs, tq):
    b, t, _ = qr.shape
    tq = min(tq, t)
    seg_lens = tuple(k.shape[1] for k, _ in segs)
    in_specs = [pl.BlockSpec((None, tq, 2 * LANES), lambda bi, j, i: (bi, i, j))]
    args = [qr]
    for k, v in segs:
        in_specs.append(pl.BlockSpec((None, k.shape[1], 2 * LANES), lambda bi, j, i: (bi, 0, j)))
        in_specs.append(pl.BlockSpec((None, None, VT_ROWS, v.shape[3]), lambda bi, j, i: (bi, j, 0, 0)))
        args += [k, v]
    g_heads = 2 * LANES // HEAD_DIM
    return pl.pallas_call(
        functools.partial(_flash_kernel, seg_lens),
        grid=(b, N_KV_HEADS, t // tq),
        in_specs=in_specs,
        out_specs=pl.BlockSpec((None, tq, 2 * LANES), lambda bi, j, i: (bi, i, j)),
        out_shape=jax.ShapeDtypeStruct((b, t, N_KV_HEADS * 2 * LANES), BF16),
        scratch_shapes=[pltpu.VMEM((g_heads, tq, 2 * LANES), BF16)],
        compiler_params=_cparams(3),
        name="flash",
    )(*args)


def _out_ffn_kernel(cw, dvw, x_ref, mod_ref, conv_ref, of_ref, ob_ref, r_ref, att_ref, gg_ref, gm_ref,
                    wo_ref, g_ref, w1_ref, w2_ref, o_ref, acc_ref):
    o = of_ref[...] + ob_ref[...]
    ms = _split_dot(o * o, gm_ref[...])
    gl = (o * lax.rsqrt(ms + EPS) * gg_ref[...]) * _silu(r_ref[...])
    y = (_dot(conv_ref[...], wo_ref[0:cw, :])
         + _dot(gl.astype(BF16), wo_ref[cw:cw + dvw, :])
         + _dot(att_ref[...], wo_ref[cw + dvw:, :]))
    xm = x_ref[...] + mod_ref[5:6, :] * y
    o_ref[...] = _ffn_half(xm, mod_ref, 2, g_ref[2], w1_ref, w2_ref, acc_ref)


def _out_ffn_call(x, mod_l, mod_row, conv, o_gla, pg, att, gg, gm, wo_all, g_l, w1_all, w2_all, li, tm):
    b, t, d = x.shape
    cw = conv.shape[2]
    dvw = o_gla.shape[3]
    aw = att.shape[2]
    tok = lambda bi, i: (bi, i, 0)
    return pl.pallas_call(
        functools.partial(_out_ffn_kernel, cw, dvw),
        grid=(b, t // tm),
        in_specs=[pl.BlockSpec((None, tm, d), tok),
                  pl.BlockSpec((None, N_MOD, d), lambda bi, i: (mod_row(bi), 0, 0)),
                  pl.BlockSpec((None, tm, cw), tok),
                  pl.BlockSpec((None, None, tm, dvw), lambda bi, i: (0, bi, i, 0)),
                  pl.BlockSpec((None, None, tm, dvw), lambda bi, i: (1, bi, i, 0)),
                  pl.BlockSpec((None, tm, dvw), lambda bi, i: (bi, i, 2)),
                  pl.BlockSpec((None, tm, aw), tok),
                  _const_spec(gg.shape), _const_spec(gm.shape), _layer_spec(wo_all, li),
                  _const_spec(g_l.shape), _layer_spec(w1_all, li, 1), _layer_spec(w2_all, li, 1)],
        out_specs=pl.BlockSpec((None, tm, d), tok),
        out_shape=jax.ShapeDtypeStruct(x.shape, F32),
        scratch_shapes=[pltpu.VMEM((tm, d), F32)],
        compiler_params=_cparams(2),
        name="out_ffn",
    )(x, mod_l, conv, o_gla, o_gla, pg, att, gg, gm, wo_all, g_l, w1_all, w2_all)


def _rope_tables(s):
    nf = HEAD_DIM // 4
    rows_n = s // GRID_W
    row = jnp.repeat(jnp.arange(rows_n), GRID_W).astype(F32)
    col = jnp.tile(jnp.arange(GRID_W), rows_n).astype(F32)
    freqs = ROPE_BASE ** (-jnp.arange(nf, dtype=F32) / nf)
    ang_r = row[:, None] * freqs
    ang_c = col[:, None] * freqs
    cos = jnp.concatenate([jnp.cos(ang_r)] * 2 + [jnp.cos(ang_c)] * 2, axis=1)
    sin = jnp.concatenate([-jnp.sin(ang_r), jnp.sin(ang_r), -jnp.sin(ang_c), jnp.sin(ang_c)], axis=1)
    return cos, sin


def kernel(x, c, ctx, c_ctx, w_ada, b_ada, g_norm, w_ffn_in, w_ffn_out, w_in, w_out, w_dw, b_dw,
           conv_norm_g, conv_norm_b, w_gla_gate, b_gla_gate, gla_norm_g, q_norm_g, k_norm_g):
    b, s, d = x.shape
    l = ctx.shape[1]
    depth = w_ada.shape[0]
    d_ff = w_ffn_out.shape[2]
    cw = w_dw.shape[2]
    dkw = w_gla_gate.shape[3]
    dvw = gla_norm_g.shape[1] * gla_norm_g.shape[2]
    n_kv = N_KV_HEADS * HEAD_DIM
    aw = d - cw - dvw
    tm_c = min(TOKEN_TILE, b * l)
    assert d_ff % FFN_CHUNK == 0 and s % TOKEN_TILE == 0 and l % GLA_CHUNK == 0 and (b * l) % tm_c == 0
    assert dkw == LANES and n_kv == LANES and cw == 2 * LANES and dvw == 2 * LANES and aw == 4 * LANES

    rows = -(-(b + 1) // SUBLANES) * SUBLANES
    cc = jnp.zeros((rows, d), F32).at[:b].set(c).at[b].set(c_ctx)
    mod = jnp.transpose(_mod_call(cc, w_ada, b_ada), (0, 2, 1, 3))
    row_x = lambda bi: bi
    row_c = lambda bi: b

    cos64, sin64 = _rope_tables(s)
    cos_k, sin_k = jnp.tile(cos64, (1, 2)), jnp.tile(sin64, (1, 2))
    one_k, zero_k = jnp.ones((l, LANES), F32), jnp.zeros((l, LANES), F32)

    gm64_128 = _group_mean_matrix(LANES, HEAD_DIM)
    gm_conv = _group_mean_matrix(cw, cw // CONV_GROUPS)
    gm_gla = _group_mean_matrix(dvw, dvw // GLA_HEADS)

    sizes = (cw, cw, dkw, dkw, dvw, dvw, GLA_RANK, GLA_RANK, aw, n_kv, n_kv)
    offs = np.concatenate([[0], np.cumsum(sizes)])

    w1_all = w_ffn_in.astype(BF16)
    w2_all = w_ffn_out.astype(BF16)
    wo_all = w_out.astype(BF16)
    wi = w_in.astype(BF16)
    wc_all = wi[:, :, offs[0]:offs[2]]
    wg_all = jnp.concatenate([wi[:, :, offs[2]:offs[8]],
                              jnp.zeros((depth, d, LANES - 2 * GLA_RANK), BF16)], axis=2)
    wa_all = wi[:, :, offs[8]:offs[11]]
    wg2_all = jnp.zeros((depth, 2, LANES, dkw), F32)
    wg2_all = wg2_all.at[:, 0, 0:GLA_RANK].set(w_gla_gate[:, 0])
    wg2_all = wg2_all.at[:, 1, GLA_RANK:2 * GLA_RANK].set(w_gla_gate[:, 1])
    weights = (w1_all, w2_all, wc_all, wg_all, wa_all)
    flat = lambda a: a.reshape(1, b * l, a.shape[2])

    for i in range(depth):
        last = i == depth - 1
        mod_l = mod[i]
        g_l = g_norm[i].reshape(3, 1, d)
        wg2 = wg2_all[i]
        bg2 = b_gla_gate[i].reshape(2, 1, dkw)
        qg = jnp.tile(q_norm_g[i], 2).reshape(1, LANES)
        kg = jnp.tile(k_norm_g[i], 2).reshape(1, LANES)
        gg = gla_norm_g[i].reshape(1, dvw)
        wdw, bdw = w_dw[i], b_dw[i].reshape(1, cw)
        cng, cnb = conv_norm_g[i].reshape(1, cw), conv_norm_b[i].reshape(1, cw)

        x, pc_x, pg_x, pa_x = _ffn_in_call(x, mod_l, row_x, g_l, *weights, i, TOKEN_TILE)
        ctx_f, pc_c, pg_c, pa_c = _ffn_in_call(flat(ctx), mod_l, row_c, g_l, *weights, i, tm_c)
        ctx = ctx_f.reshape(b, l, d)
        pc_c, pg_c, pa_c = (p.reshape(b, l, p.shape[2]) for p in (pc_c, pg_c, pa_c))
        conv_x = _conv_call(pc_x, wdw, bdw, cng, cnb, gm_conv)
        o_c, s_c = _gla_call(pg_c, wg2, bg2, jnp.zeros((b, 2, dvw, dkw), F32), dkw, dvw)
        o_x, _ = _gla_call(pg_x, wg2, bg2, s_c, dkw, dvw)
        q_c, k_c, v_c = _kvprep_call(pa_c, one_k, zero_k, qg, kg, gm64_128, min(TOKEN_TILE, l))
        q_x, k_x, v_x = _kvprep_call(pa_x, cos_k, sin_k, qg, kg, gm64_128, TOKEN_TILE)
        att_x = _flash_call(q_x, [(k_c, v_c), (k_x, v_x)], ATT_TQ)
        x = _out_ffn_call(x, mod_l, row_x, conv_x, o_x, pg_x, att_x, gg, gm_gla, wo_all, g_l,
                          w1_all, w2_all, i, TOKEN_TILE)
        if not last:
            conv_c = _conv_call(pc_c, wdw, bdw, cng, cnb, gm_conv)
            att_c = _flash_call(q_c, [(k_c, v_c)], ATT_TQ)
            ctx = _out_ffn_call(flat(ctx), mod_l, row_c, flat(conv_c), o_c.reshape(2, 1, b * l, dvw),
                                flat(pg_c), flat(att_c), gg, gm_gla, wo_all, g_l,
                                w1_all, w2_all, i, tm_c).reshape(b, l, d)
    return x
```

```python
import functools

import numpy as np
import jax
import jax.numpy as jnp
from jax import lax
from jax.experimental import pallas as pl
from jax.experimental.pallas import tpu as pltpu

F32 = jnp.float32
BF16 = jnp.bfloat16

CONV_GROUPS = 4
CONV_K = 31
GLA_HEADS = 4
GLA_RANK = 16
GLA_GATE_NORM = 16.0
GLA_CHUNK = 64
HEAD_DIM = 64
N_KV_HEADS = 2
GRID_W = 64
ROPE_BASE = 10000.0
N_MOD = 9
EPS = 1e-6
LOG2E = 1.4426950408889634

LANES = 128
SUBLANES = 8
VMEM_LIMIT_BYTES = 56 * 1024 * 1024

TOKEN_TILE = 512
CONV_ROWS = 32
CONV_HALO = 16
CONV_NORM_ROWS = 1024
GLA_BLOCK = 512
VT_ROWS = 80
LAYER_TILING = ((256, 512, 256, 512, 4),
                (2816, 512, 256, 512, 2),
                (1408, 512, 256, 256, 4),
                (256, 1024, 256, 1024, 2))


def _cparams(n_axes):
    return pltpu.CompilerParams(dimension_semantics=("arbitrary",) * n_axes,
                                vmem_limit_bytes=VMEM_LIMIT_BYTES)


def _const_spec(shape):
    nd = len(shape)
    return pl.BlockSpec(shape, lambda *_: (0,) * nd, pipeline_mode=pl.Buffered(1))


def _layer_spec(arr, *idx):
    lead = len(idx)
    blk = (None,) * lead + tuple(arr.shape[lead:])
    tail = (0,) * (arr.ndim - lead)
    return pl.BlockSpec(blk, lambda *_: tuple(idx) + tail, pipeline_mode=pl.Buffered(1))


def _dot(a, b):
    return jnp.dot(a, b, preferred_element_type=F32)


def _dot_nt(a, b):
    return lax.dot_general(a, b, (((1,), (1,)), ((), ())), preferred_element_type=F32)


def _dot_tn(a, b):
    return lax.dot_general(a, b, (((0,), (0,)), ((), ())), preferred_element_type=F32)


def _sigmoid(x):
    return 1.0 / (1.0 + jnp.exp(-x))


def _silu(x):
    return x * _sigmoid(x)


def _split_dot(v, m):
    hi = v.astype(BF16)
    lo = (v - hi.astype(F32)).astype(BF16)
    return _dot(hi, m) + _dot(lo, m)


def _split3_dot_left(m, v):
    hi = v.astype(BF16)
    r1 = v - hi.astype(F32)
    mid = r1.astype(BF16)
    lo = (r1 - mid.astype(F32)).astype(BF16)
    return _dot(m, hi) + _dot(m, mid) + _dot(m, lo)


def _pre(x, g, shift, scale):
    r = lax.rsqrt(jnp.mean(x * x, axis=-1, keepdims=True) + EPS)
    return (x * r * g) * (1.0 + scale) + shift


def _group_mean_matrix(width, group):
    idx = np.arange(width) // group
    return jnp.asarray((idx[:, None] == idx[None, :]).astype(np.float32) / group, dtype=BF16)


def _rope_partner(x):
    n = x.shape[-1]
    lane = lax.broadcasted_iota(jnp.int32, x.shape, x.ndim - 1)
    up = pltpu.roll(x, n - 16, axis=x.ndim - 1)
    dn = pltpu.roll(x, 16, axis=x.ndim - 1)
    return jnp.where((lane % 32) < 16, up, dn)


def _mod_kernel(c_ref, w_ref, b_ref, o_ref):
    sc = _silu(c_ref[...]).astype(BF16)
    o_ref[...] = _dot(sc, w_ref[...].astype(BF16)) + b_ref[...]


def _mod_call(cc, w_ada, b_ada):
    depth, d, _ = w_ada.shape
    rows = cc.shape[0]
    return pl.pallas_call(
        _mod_kernel,
        grid=(depth, N_MOD),
        in_specs=[pl.BlockSpec((rows, d), lambda i, j: (0, 0)),
                  pl.BlockSpec((None, d, d), lambda i, j: (i, 0, j)),
                  pl.BlockSpec((None, 1, d), lambda i, j: (i, 0, j))],
        out_specs=pl.BlockSpec((None, None, rows, d), lambda i, j: (i, j, 0, 0)),
        out_shape=jax.ShapeDtypeStruct((depth, N_MOD, rows, d), F32),
        compiler_params=_cparams(2),
        name="mod",
    )(cc, w_ada, b_ada.reshape(depth, 1, N_MOD * d))


def _ffn_half(x, mod_ref, j, g, w1_ref, w2_ref, acc_ref, chunk):
    d_ff = w2_ref.shape[0]
    shift = mod_ref[3 * j:3 * j + 1, :]
    scale = mod_ref[3 * j + 1:3 * j + 2, :]
    gate = mod_ref[3 * j + 2:3 * j + 3, :]
    h = _pre(x, g, shift, scale).astype(BF16)
    for c0 in range(0, d_ff, chunk):
        a = _dot(h, w1_ref[:, c0:c0 + chunk])
        u = _dot(h, w1_ref[:, d_ff + c0:d_ff + c0 + chunk])
        part = _dot((_silu(a) * u).astype(BF16), w2_ref[c0:c0 + chunk, :])
        if c0 == 0:
            acc_ref[...] = part
        else:
            acc_ref[...] += part
    return x + (0.5 * gate) * acc_ref[...]


def _ffn_in_kernel(chunk, x_ref, mod_ref, g_ref, w1_ref, w2_ref, wc_ref, wg_ref, wa_ref,
                   xo_ref, pc_ref, pg_ref, pa_ref, acc_ref):
    x1 = _ffn_half(x_ref[...], mod_ref, 0, g_ref[0], w1_ref, w2_ref, acc_ref, chunk)
    xo_ref[...] = x1
    h = _pre(x1, g_ref[1], mod_ref[3:4, :], mod_ref[4:5, :]).astype(BF16)
    pc_ref[...] = _dot(h, wc_ref[...])
    pg_ref[...] = _dot(h, wg_ref[...])
    pa_ref[...] = _dot(h, wa_ref[...])


def _ffn_in_call(x, mod_l, mod_row, g_l, w1_all, w2_all, wc_all, wg_all, wa_all, li, tm, chunk):
    b, t, d = x.shape
    tok = lambda bi, i: (bi, i, 0)
    widths = (wc_all.shape[2], wg_all.shape[2], wa_all.shape[2])
    return pl.pallas_call(
        functools.partial(_ffn_in_kernel, chunk),
        grid=(b, t // tm),
        in_specs=[pl.BlockSpec((None, tm, d), tok),
                  pl.BlockSpec((None, N_MOD, d), lambda bi, i: (mod_row(bi), 0, 0)),
                  _const_spec(g_l.shape), _layer_spec(w1_all, li, 0), _layer_spec(w2_all, li, 0),
                  _layer_spec(wc_all, li), _layer_spec(wg_all, li), _layer_spec(wa_all, li)],
        out_specs=[pl.BlockSpec((None, tm, d), tok)] + [pl.BlockSpec((None, tm, w), tok) for w in widths],
        out_shape=[jax.ShapeDtypeStruct(x.shape, F32)]
        + [jax.ShapeDtypeStruct((b, t, w), F32) for w in widths],
        scratch_shapes=[pltpu.VMEM((tm, d), F32)],
        compiler_params=_cparams(2),
        name="ffn_in",
    )(x, mod_l, g_l, w1_all, w2_all, wc_all, wg_all, wa_all)


def _conv_kernel(t, cw, pc_ref, wdw_ref, bdw_ref, cg_ref, cb_ref, gm_ref, o_ref, hbuf, h2_scr):
    r = CONV_ROWS
    halo = CONV_HALO
    hbuf[0:halo, :] = jnp.zeros((halo, cw), F32)
    hbuf[t + halo:t + 2 * halo, :] = jnp.zeros((halo, cw), F32)

    def glu(i, carry):
        r0 = pl.multiple_of(i * r, r)
        a = pc_ref[pl.ds(r0, r), 0:cw]
        gt = pc_ref[pl.ds(r0, r), cw:2 * cw]
        hbuf[pl.ds(r0 + halo, r), :] = a * _sigmoid(gt)
        return carry

    lax.fori_loop(0, t // r, glu, 0)
    gm = gm_ref[...]

    def tile(i, carry):
        r0 = pl.multiple_of(i * r, r)
        win = hbuf[pl.ds(r0, r + 2 * halo), :]
        acc = jnp.zeros((r, cw), F32)
        for res in range(SUBLANES):
            wr = win if res == 0 else pltpu.roll(win, r + 2 * halo - res, axis=0)
            for a in range(2 * halo // SUBLANES):
                s = SUBLANES * a + res
                if 1 <= s <= CONV_K:
                    acc = acc + wr[SUBLANES * a:SUBLANES * a + r, :] * wdw_ref[s - 1:s, :]
        h2_scr[pl.ds(r0, r), :] = acc + bdw_ref[...]
        return carry

    lax.fori_loop(0, t // r, tile, 0, unroll=2)

    nb = min(CONV_NORM_ROWS, t)

    def norm(i, carry):
        rows = pl.ds(pl.multiple_of(i * nb, nb), nb)
        h2 = h2_scr[rows, :]
        mu = _split_dot(h2, gm)
        dlt = h2 - mu
        var = _split_dot(dlt * dlt, gm)
        y = dlt * lax.rsqrt(var + EPS) * cg_ref[...] + cb_ref[...]
        o_ref[rows, :] = _silu(y).astype(o_ref.dtype)
        return carry

    lax.fori_loop(0, t // nb, norm, 0)


def _conv_call(pc, w_dw, b_dw, cn_g, cn_b, gm):
    b, t, two_cw = pc.shape
    cw = two_cw // 2
    return pl.pallas_call(
        functools.partial(_conv_kernel, t, cw),
        grid=(b,),
        in_specs=[pl.BlockSpec((None, t, two_cw), lambda bi: (bi, 0, 0)),
                  _const_spec(w_dw.shape), _const_spec(b_dw.shape), _const_spec(cn_g.shape),
                  _const_spec(cn_b.shape), _const_spec(gm.shape)],
        out_specs=pl.BlockSpec((None, t, cw), lambda bi: (bi, 0, 0)),
        out_shape=jax.ShapeDtypeStruct((b, t, cw), BF16),
        scratch_shapes=[pltpu.VMEM((t + 2 * CONV_HALO, cw), F32), pltpu.VMEM((t, cw), F32)],
        compiler_params=_cparams(1),
        name="conv",
    )(pc, w_dw, b_dw, cn_g, cn_b, gm)


def _gla_block(fwd, n_ch, dkw, dvw, pg_ref, wg_ref, bg_ref, o_ref, s_scr):
    c = GLA_CHUNK
    dk = dkw // GLA_HEADS
    dv = dvw // GLA_HEADS
    lr_off = 2 * dkw + 2 * dvw
    z = _dot(pg_ref[:, lr_off:lr_off + LANES], wg_ref[...]) + bg_ref[...]
    la_all = -(jnp.maximum(-z, 0.0) + jnp.log(1.0 + jnp.exp(-jnp.abs(z)))) * (1.0 / GLA_GATE_NORM)

    ri = lax.broadcasted_iota(jnp.int32, (c, c), 0)
    ci = lax.broadcasted_iota(jnp.int32, (c, c), 1)
    tri_m = jnp.where((ci <= ri) if fwd else (ci >= ri), 1.0, 0.0).astype(BF16)
    rih = lax.broadcasted_iota(jnp.int32, (c, GLA_HEADS * c), 0)
    cih = lax.broadcasted_iota(jnp.int32, (c, GLA_HEADS * c), 1) % c
    tri_h = (cih <= rih) if fwd else (cih >= rih)
    klane = lax.broadcasted_iota(jnp.int32, (c, dkw), 1) // dk
    vlane = lax.broadcasted_iota(jnp.int32, (c, dvw), 1) // dv
    srow = lax.broadcasted_iota(jnp.int32, (dvw, dkw), 0) // dv
    scol = lax.broadcasted_iota(jnp.int32, (dvw, dkw), 1) // dk
    smask = srow == scol

    chunks = range(n_ch)
    rows = [slice(j * c, (j + 1) * c) for j in chunks]
    bcum = [_split3_dot_left(tri_m, la_all[rows[j], :]) for j in chunks]
    q_e, k_end, kblk, vblk, vb, decay = [], [], [], [], [], []
    for j in chunks:
        q = pg_ref[rows[j], 0:dkw] * (dk ** -0.5)
        k = pg_ref[rows[j], dkw:2 * dkw]
        v = pg_ref[rows[j], 2 * dkw:2 * dkw + dvw]
        btot = bcum[j][c - 1:c, :] if fwd else bcum[j][0:1, :]
        q_e.append((q * jnp.exp(bcum[j])).astype(BF16))
        k_e = k * jnp.exp(-bcum[j])
        k_end.append((k * jnp.exp(btot - bcum[j])).astype(BF16))
        kblk.append(jnp.concatenate([jnp.where(klane == h, k_e, 0.0) for h in range(GLA_HEADS)],
                                    axis=0).astype(BF16))
        vblk.append(jnp.concatenate([jnp.where(vlane == h, v, 0.0) for h in range(GLA_HEADS)],
                                    axis=0).astype(BF16))
        vb.append(v.astype(BF16))
        decay.append(jnp.exp(btot))
    att = [jnp.where(tri_h, _dot_nt(q_e[j], kblk[j]), 0.0).astype(BF16) for j in chunks]
    o_intra = [_dot(att[j], vblk[j]) for j in chunks]
    upd = [jnp.where(smask, _dot_tn(vb[j], k_end[j]), 0.0) for j in chunks]
    s_t = s_scr[...]
    s_start = [None] * n_ch
    for j in (chunks if fwd else reversed(chunks)):
        s_start[j] = s_t.astype(BF16)
        s_t = s_t * decay[j] + upd[j]
    s_scr[...] = s_t
    for j in chunks:
        o_ref[rows[j], :] = o_intra[j] + _dot_nt(q_e[j], s_start[j])


def _gla_kernel(n_ch, dkw, dvw, pg_ref, wg_ref, bg_ref, s0_ref, o_ref, sfin_ref, s_scr):
    d = pl.program_id(1)
    i = pl.program_id(2)

    @pl.when(i == 0)
    def _():
        s_scr[...] = s0_ref[...]

    @pl.when(d == 0)
    def _():
        _gla_block(True, n_ch, dkw, dvw, pg_ref, wg_ref, bg_ref, o_ref, s_scr)

    @pl.when(d != 0)
    def _():
        _gla_block(False, n_ch, dkw, dvw, pg_ref, wg_ref, bg_ref, o_ref, s_scr)

    @pl.when(i == pl.num_programs(2) - 1)
    def _():
        sfin_ref[...] = s_scr[...]


def _gla_call(pg, wg2, bg2, s0, dkw, dvw):
    b, t, w = pg.shape
    blk = min(GLA_BLOCK, t)
    nb = t // blk
    n_ch = blk // GLA_CHUNK
    seq = lambda bi, d, i: (bi, jnp.where(d == 0, i, nb - 1 - i), 0)
    return pl.pallas_call(
        functools.partial(_gla_kernel, n_ch, dkw, dvw),
        grid=(b, 2, nb),
        in_specs=[pl.BlockSpec((None, blk, w), seq),
                  pl.BlockSpec((None, LANES, dkw), lambda bi, d, i: (d, 0, 0)),
                  pl.BlockSpec((None, 1, dkw), lambda bi, d, i: (d, 0, 0)),
                  pl.BlockSpec((None, None, dvw, dkw), lambda bi, d, i: (bi, d, 0, 0))],
        out_specs=[pl.BlockSpec((None, None, blk, dvw),
                                lambda bi, d, i: (d, bi, jnp.where(d == 0, i, nb - 1 - i), 0)),
                   pl.BlockSpec((None, None, dvw, dkw), lambda bi, d, i: (bi, d, 0, 0))],
        out_shape=[jax.ShapeDtypeStruct((2, b, t, dvw), F32),
                   jax.ShapeDtypeStruct((b, 2, dvw, dkw), F32)],
        scratch_shapes=[pltpu.VMEM((dvw, dkw), F32)],
        compiler_params=_cparams(3),
        name="gla",
    )(pg, wg2, bg2, s0)


def _kvprep_kernel(aw, pa_ref, cos_ref, sin_ref, qg_ref, kg_ref, gm_ref, q_ref, k_ref, vt_ref):
    hd = HEAD_DIM
    cos = cos_ref[...]
    sin = sin_ref[...]

    def norm_rope(x, g):
        ms = _split_dot(x * x, gm_ref[...])
        xn = x * lax.rsqrt(ms + EPS) * g
        return xn * cos + _rope_partner(xn) * sin

    for t0 in range(0, aw, LANES):
        q_ref[:, t0:t0 + LANES] = (norm_rope(pa_ref[:, t0:t0 + LANES], qg_ref[...])
                                   * (hd ** -0.5 * LOG2E)).astype(BF16)
    vv = pa_ref[:, aw + LANES:aw + 2 * LANES]
    kr = norm_rope(pa_ref[:, aw:aw + LANES], kg_ref[...])
    lane = lax.broadcasted_iota(jnp.int32, kr.shape, 1)
    lo = lane < hd
    kswap = pltpu.roll(kr, hd, axis=1)
    k0 = jnp.where(lo, kr, kswap).astype(BF16)
    k1 = jnp.where(lo, kswap, kr).astype(BF16)
    k_ref[...] = jnp.concatenate([k0, k0, k1, k1], axis=1)
    vt_ref[0] = jnp.where(lo, vv, 1.0).T[0:VT_ROWS, :].astype(BF16)
    vt_ref[1] = jnp.where(lo, pltpu.roll(vv, hd, axis=1), 1.0).T[0:VT_ROWS, :].astype(BF16)


def _kvprep_call(pa, cos_t, sin_t, qg, kg, gm, tt):
    b, t, w = pa.shape
    aw = w - 2 * LANES
    return pl.pallas_call(
        functools.partial(_kvprep_kernel, aw),
        grid=(b, t // tt),
        in_specs=[pl.BlockSpec((None, tt, w), lambda bi, i: (bi, i, 0)),
                  pl.BlockSpec((tt, LANES), lambda bi, i: (i, 0)),
                  pl.BlockSpec((tt, LANES), lambda bi, i: (i, 0)),
                  _const_spec(qg.shape), _const_spec(kg.shape), _const_spec(gm.shape)],
        out_specs=[pl.BlockSpec((None, tt, aw), lambda bi, i: (bi, i, 0)),
                   pl.BlockSpec((None, tt, 4 * LANES), lambda bi, i: (bi, i, 0)),
                   pl.BlockSpec((None, N_KV_HEADS, VT_ROWS, tt), lambda bi, i: (bi, 0, 0, i))],
        out_shape=[jax.ShapeDtypeStruct((b, t, aw), BF16),
                   jax.ShapeDtypeStruct((b, t, 4 * LANES), BF16),
                   jax.ShapeDtypeStruct((b, N_KV_HEADS, VT_ROWS, t), BF16)],
        compiler_params=_cparams(2),
        name="kvprep",
    )(pa, cos_t, sin_t, qg, kg, gm)


def _flash_kernel(seg_lens, tk_max, lookahead, q_ref, *rest):
    n_seg = len(seg_lens)
    kv_refs = rest[:2 * n_seg]
    o_ref, qs_scr = rest[2 * n_seg:]
    hd = HEAD_DIM
    g_heads = 2 * LANES // hd
    qr = q_ref[...]
    lane_head = lax.broadcasted_iota(jnp.int32, qr.shape, 1) // hd
    for h in range(g_heads):
        qs_scr[h] = jnp.where(lane_head == h, qr, jnp.zeros_like(qr))

    tasks = []
    for h in range(g_heads):
        first = True
        for si in range(n_seg):
            k_ref, vt_ref = kv_refs[2 * si], kv_refs[2 * si + 1]
            tk = min(tk_max, seg_lens[si])
            for c0 in range(0, seg_lens[si], tk):
                tasks.append((h, k_ref, vt_ref, c0, tk, first))
                first = False

    def scores(task):
        h, k_ref, _, c0, tk, _ = task
        return _dot_nt(k_ref[c0:c0 + tk, :], qs_scr[h])

    m_run = [None] * g_heads
    acc = [None] * g_heads
    pending = [scores(t) for t in tasks[:lookahead]]
    for n, (h, _, vt_ref, c0, tk, first) in enumerate(tasks):
        if n + lookahead < len(tasks):
            pending.append(scores(tasks[n + lookahead]))
        s = pending.pop(0)
        m_cur = jnp.max(s, axis=0, keepdims=True)
        if first:
            m_new = m_cur
            p = jnp.exp2(s - m_new).astype(BF16)
            acc[h] = _dot(vt_ref[:, c0:c0 + tk], p)
        else:
            m_new = jnp.maximum(m_run[h], m_cur)
            p = jnp.exp2(s - m_new).astype(BF16)
            acc[h] = acc[h] * jnp.exp2(m_run[h] - m_new) + _dot(vt_ref[:, c0:c0 + tk], p)
        m_run[h] = m_new

    o_t = jnp.concatenate([acc[h][0:hd, :] * (1.0 / acc[h][hd:hd + 1, :]) for h in range(g_heads)],
                          axis=0)
    o_ref[...] = o_t.T.astype(o_ref.dtype)


def _flash_call(qr, segs, tq, tk, lookahead):
    b, t, _ = qr.shape
    tq = min(tq, t)
    seg_lens = tuple(k.shape[1] for k, _ in segs)
    in_specs = [pl.BlockSpec((None, tq, 2 * LANES), lambda bi, j, i: (bi, i, j))]
    args = [qr]
    for k, v in segs:
        in_specs.append(pl.BlockSpec((None, k.shape[1], 2 * LANES), lambda bi, j, i: (bi, 0, j)))
        in_specs.append(pl.BlockSpec((None, None, VT_ROWS, v.shape[3]), lambda bi, j, i: (bi, j, 0, 0)))
        args += [k, v]
    g_heads = 2 * LANES // HEAD_DIM
    return pl.pallas_call(
        functools.partial(_flash_kernel, seg_lens, tk, lookahead),
        grid=(b, N_KV_HEADS, t // tq),
        in_specs=in_specs,
        out_specs=pl.BlockSpec((None, tq, 2 * LANES), lambda bi, j, i: (bi, i, j)),
        out_shape=jax.ShapeDtypeStruct((b, t, N_KV_HEADS * 2 * LANES), BF16),
        scratch_shapes=[pltpu.VMEM((g_heads, tq, 2 * LANES), BF16)],
        compiler_params=_cparams(3),
        name="flash",
    )(*args)


def _out_kernel(cw, dvw, x_ref, mod_ref, conv_ref, of_ref, ob_ref, r_ref, att_ref, gg_ref, gm_ref,
                wo_ref, o_ref):
    o = of_ref[...] + ob_ref[...]
    ms = _split_dot(o * o, gm_ref[...])
    gl = (o * lax.rsqrt(ms + EPS) * gg_ref[...]) * _silu(r_ref[...])
    y = (_dot(conv_ref[...], wo_ref[0:cw, :])
         + _dot(gl.astype(BF16), wo_ref[cw:cw + dvw, :])
         + _dot(att_ref[...], wo_ref[cw + dvw:, :]))
    o_ref[...] = x_ref[...] + mod_ref[5:6, :] * y


def _out_call(x, mod_l, mod_row, conv, o_gla, pg, att, gg, gm, wo_all, li, tm):
    b, t, d = x.shape
    cw = conv.shape[2]
    dvw = o_gla.shape[3]
    aw = att.shape[2]
    tok = lambda bi, i: (bi, i, 0)
    return pl.pallas_call(
        functools.partial(_out_kernel, cw, dvw),
        grid=(b, t // tm),
        in_specs=[pl.BlockSpec((None, tm, d), tok),
                  pl.BlockSpec((None, N_MOD, d), lambda bi, i: (mod_row(bi), 0, 0)),
                  pl.BlockSpec((None, tm, cw), tok),
                  pl.BlockSpec((None, None, tm, dvw), lambda bi, i: (0, bi, i, 0)),
                  pl.BlockSpec((None, None, tm, dvw), lambda bi, i: (1, bi, i, 0)),
                  pl.BlockSpec((None, tm, dvw), lambda bi, i: (bi, i, 2)),
                  pl.BlockSpec((None, tm, aw), tok),
                  _const_spec(gg.shape), _const_spec(gm.shape), _layer_spec(wo_all, li)],
        out_specs=pl.BlockSpec((None, tm, d), tok),
        out_shape=jax.ShapeDtypeStruct(x.shape, F32),
        compiler_params=_cparams(2),
        name="outproj",
    )(x, mod_l, conv, o_gla, o_gla, pg, att, gg, gm, wo_all)


def _ffn_kernel(j, chunk, x_ref, mod_ref, g_ref, w1_ref, w2_ref, o_ref, acc_ref):
    o_ref[...] = _ffn_half(x_ref[...], mod_ref, j, g_ref[j], w1_ref, w2_ref, acc_ref, chunk)


def _ffn_call(x, mod_l, mod_row, g_l, w1_all, w2_all, li, which, j, tm, chunk):
    b, t, d = x.shape
    tok = lambda bi, i: (bi, i, 0)
    return pl.pallas_call(
        functools.partial(_ffn_kernel, j, chunk),
        grid=(b, t // tm),
        in_specs=[pl.BlockSpec((None, tm, d), tok),
                  pl.BlockSpec((None, N_MOD, d), lambda bi, i: (mod_row(bi), 0, 0)),
                  _const_spec(g_l.shape), _layer_spec(w1_all, li, which), _layer_spec(w2_all, li, which)],
        out_specs=pl.BlockSpec((None, tm, d), tok),
        out_shape=jax.ShapeDtypeStruct(x.shape, F32),
        scratch_shapes=[pltpu.VMEM((tm, d), F32)],
        compiler_params=_cparams(2),
        name="ffn",
    )(x, mod_l, g_l, w1_all, w2_all)


def _rope_tables(s):
    nf = HEAD_DIM // 4
    rows_n = s // GRID_W
    row = jnp.repeat(jnp.arange(rows_n), GRID_W).astype(F32)
    col = jnp.tile(jnp.arange(GRID_W), rows_n).astype(F32)
    freqs = ROPE_BASE ** (-jnp.arange(nf, dtype=F32) / nf)
    ang_r = row[:, None] * freqs
    ang_c = col[:, None] * freqs
    cos = jnp.concatenate([jnp.cos(ang_r)] * 2 + [jnp.cos(ang_c)] * 2, axis=1)
    sin = jnp.concatenate([-jnp.sin(ang_r), jnp.sin(ang_r), -jnp.sin(ang_c), jnp.sin(ang_c)], axis=1)
    return cos, sin


def kernel(x, c, ctx, c_ctx, w_ada, b_ada, g_norm, w_ffn_in, w_ffn_out, w_in, w_out, w_dw, b_dw,
           conv_norm_g, conv_norm_b, w_gla_gate, b_gla_gate, gla_norm_g, q_norm_g, k_norm_g):
    b, s, d = x.shape
    l = ctx.shape[1]
    depth = w_ada.shape[0]
    d_ff = w_ffn_out.shape[2]
    cw = w_dw.shape[2]
    dkw = w_gla_gate.shape[3]
    dvw = gla_norm_g.shape[1] * gla_norm_g.shape[2]
    n_kv = N_KV_HEADS * HEAD_DIM
    aw = d - cw - dvw
    tm_c = min(TOKEN_TILE, b * l)
    assert all(d_ff % t[0] == 0 for t in LAYER_TILING) and s % TOKEN_TILE == 0 and l % GLA_CHUNK == 0 and (b * l) % tm_c == 0
    assert dkw == LANES and n_kv == LANES and cw == 2 * LANES and dvw == 2 * LANES and aw == 4 * LANES

    rows = -(-(b + 1) // SUBLANES) * SUBLANES
    cc = jnp.zeros((rows, d), F32).at[:b].set(c).at[b].set(c_ctx)
    mod = jnp.transpose(_mod_call(cc, w_ada, b_ada), (0, 2, 1, 3))
    row_x = lambda bi: bi
    row_c = lambda bi: b

    cos64, sin64 = _rope_tables(s)
    cos_k, sin_k = jnp.tile(cos64, (1, 2)), jnp.tile(sin64, (1, 2))
    one_k, zero_k = jnp.ones((l, LANES), F32), jnp.zeros((l, LANES), F32)

    gm64_128 = _group_mean_matrix(LANES, HEAD_DIM)
    gm_conv = _group_mean_matrix(cw, cw // CONV_GROUPS)
    gm_gla = _group_mean_matrix(dvw, dvw // GLA_HEADS)

    sizes = (cw, cw, dkw, dkw, dvw, dvw, GLA_RANK, GLA_RANK, aw, n_kv, n_kv)
    offs = np.concatenate([[0], np.cumsum(sizes)])

    w1_all = w_ffn_in.astype(BF16)
    w2_all = w_ffn_out.astype(BF16)
    wo_all = w_out.astype(BF16)
    wi = w_in.astype(BF16)
    wc_all = wi[:, :, offs[0]:offs[2]]
    wg_all = jnp.concatenate([wi[:, :, offs[2]:offs[8]],
                              jnp.zeros((depth, d, LANES - 2 * GLA_RANK), BF16)], axis=2)
    wa_all = wi[:, :, offs[8]:offs[11]]
    wg2_all = jnp.zeros((depth, 2, LANES, dkw), F32)
    wg2_all = wg2_all.at[:, 0, 0:GLA_RANK].set(w_gla_gate[:, 0])
    wg2_all = wg2_all.at[:, 1, GLA_RANK:2 * GLA_RANK].set(w_gla_gate[:, 1])
    weights = (w1_all, w2_all, wc_all, wg_all, wa_all)
    flat = lambda a: a.reshape(1, b * l, a.shape[2])

    for i in range(depth):
        last = i == depth - 1
        mod_l = mod[i]
        g_l = g_norm[i].reshape(3, 1, d)
        wg2 = wg2_all[i]
        bg2 = b_gla_gate[i].reshape(2, 1, dkw)
        qg = jnp.tile(q_norm_g[i], 2).reshape(1, LANES)
        kg = jnp.tile(k_norm_g[i], 2).reshape(1, LANES)
        gg = gla_norm_g[i].reshape(1, dvw)
        wdw, bdw = w_dw[i], b_dw[i].reshape(1, cw)
        cng, cnb = conv_norm_g[i].reshape(1, cw), conv_norm_b[i].reshape(1, cw)

        fchunk, ftile, tq, tk, la = LAYER_TILING[i % len(LAYER_TILING)]
        x, pc_x, pg_x, pa_x = _ffn_in_call(x, mod_l, row_x, g_l, *weights, i, TOKEN_TILE, fchunk)
        ctx_f, pc_c, pg_c, pa_c = _ffn_in_call(flat(ctx), mod_l, row_c, g_l, *weights, i, tm_c, fchunk)
        ctx = ctx_f.reshape(b, l, d)
        pc_c, pg_c, pa_c = (p.reshape(b, l, p.shape[2]) for p in (pc_c, pg_c, pa_c))
        conv_x = _conv_call(pc_x, wdw, bdw, cng, cnb, gm_conv)
        o_c, s_c = _gla_call(pg_c, wg2, bg2, jnp.zeros((b, 2, dvw, dkw), F32), dkw, dvw)
        o_x, _ = _gla_call(pg_x, wg2, bg2, s_c, dkw, dvw)
        q_c, k_c, v_c = _kvprep_call(pa_c, one_k, zero_k, qg, kg, gm64_128, min(TOKEN_TILE, l))
        q_x, k_x, v_x = _kvprep_call(pa_x, cos_k, sin_k, qg, kg, gm64_128, TOKEN_TILE)
        att_x = _flash_call(q_x, [(k_c, v_c), (k_x, v_x)], tq, tk, la)
        x = _out_call(x, mod_l, row_x, conv_x, o_x, pg_x, att_x, gg, gm_gla, wo_all, i, TOKEN_TILE)
        x = _ffn_call(x, mod_l, row_x, g_l, w1_all, w2_all, i, 1, 2, ftile, fchunk)
        if not last:
            conv_c = _conv_call(pc_c, wdw, bdw, cng, cnb, gm_conv)
            att_c = _flash_call(q_c, [(k_c, v_c)], tq, tk, la)
            ctx_f = _out_call(flat(ctx), mod_l, row_c, flat(conv_c), o_c.reshape(2, 1, b * l, dvw),
                              flat(pg_c), flat(att_c), gg, gm_gla, wo_all, i, tm_c)
            ctx = _ffn_call(ctx_f, mod_l, row_c, g_l, w1_all, w2_all, i, 1, 2, tm_c, fchunk).reshape(b, l, d)
    return x
```

```python
import functools

import numpy as np
import jax
import jax.numpy as jnp
from jax import lax
from jax.experimental import pallas as pl
from jax.experimental.pallas import tpu as pltpu

F32 = jnp.float32
BF16 = jnp.bfloat16

CONV_GROUPS = 4
CONV_K = 31
GLA_HEADS = 4
GLA_RANK = 16
GLA_GATE_NORM = 16.0
GLA_CHUNK = 64
HEAD_DIM = 64
N_KV_HEADS = 2
GRID_W = 64
ROPE_BASE = 10000.0
N_MOD = 9
EPS = 1e-6
LOG2E = 1.4426950408889634

LANES = 128
SUBLANES = 8
VMEM_LIMIT_BYTES = 56 * 1024 * 1024

TOKEN_TILE = 512
CONV_ROWS = 32
CONV_HALO = 16
CONV_NORM_ROWS = 1024
GLA_BLOCK = 512
VT_ROWS = 80
ATT_MAX_RISE = 60.0
LAYER_TILING = ((256, 1024, 256, 512, 4, True),
                (256, 1024, 256, 512, 2, True),
                (256, 1024, 256, 1024, 2, True),
                (256, 1024, 256, 512, 4, False))


def _cparams(n_axes):
    return pltpu.CompilerParams(dimension_semantics=("arbitrary",) * n_axes,
                                vmem_limit_bytes=VMEM_LIMIT_BYTES)


def _const_spec(shape):
    nd = len(shape)
    return pl.BlockSpec(shape, lambda *_: (0,) * nd, pipeline_mode=pl.Buffered(1))


def _layer_spec(arr, *idx):
    lead = len(idx)
    blk = (None,) * lead + tuple(arr.shape[lead:])
    tail = (0,) * (arr.ndim - lead)
    return pl.BlockSpec(blk, lambda *_: tuple(idx) + tail, pipeline_mode=pl.Buffered(1))


def _dot(a, b):
    return jnp.dot(a, b, preferred_element_type=F32)


def _dot_nt(a, b):
    return lax.dot_general(a, b, (((1,), (1,)), ((), ())), preferred_element_type=F32)


def _dot_tn(a, b):
    return lax.dot_general(a, b, (((0,), (0,)), ((), ())), preferred_element_type=F32)


def _sigmoid(x):
    return 1.0 / (1.0 + jnp.exp(-x))


def _silu(x):
    return x * _sigmoid(x)


def _split_dot(v, m):
    hi = v.astype(BF16)
    lo = (v - hi.astype(F32)).astype(BF16)
    return _dot(hi, m) + _dot(lo, m)


def _split3_dot_left(m, v):
    hi = v.astype(BF16)
    r1 = v - hi.astype(F32)
    mid = r1.astype(BF16)
    lo = (r1 - mid.astype(F32)).astype(BF16)
    return _dot(m, hi) + _dot(m, mid) + _dot(m, lo)


def _pre(x, g, shift, scale):
    r = lax.rsqrt(jnp.mean(x * x, axis=-1, keepdims=True) + EPS)
    return (x * r * g) * (1.0 + scale) + shift


def _group_mean_matrix(width, group):
    idx = np.arange(width) // group
    return jnp.asarray((idx[:, None] == idx[None, :]).astype(np.float32) / group, dtype=BF16)


def _rope_partner(x):
    n = x.shape[-1]
    lane = lax.broadcasted_iota(jnp.int32, x.shape, x.ndim - 1)
    up = pltpu.roll(x, n - 16, axis=x.ndim - 1)
    dn = pltpu.roll(x, 16, axis=x.ndim - 1)
    return jnp.where((lane % 32) < 16, up, dn)


def _mod_kernel(c_ref, w_ref, b_ref, o_ref):
    sc = _silu(c_ref[...]).astype(BF16)
    o_ref[...] = _dot(sc, w_ref[...].astype(BF16)) + b_ref[...]


def _mod_call(cc, w_ada, b_ada):
    depth, d, _ = w_ada.shape
    rows = cc.shape[0]
    return pl.pallas_call(
        _mod_kernel,
        grid=(depth, N_MOD),
        in_specs=[pl.BlockSpec((rows, d), lambda i, j: (0, 0)),
                  pl.BlockSpec((None, d, d), lambda i, j: (i, 0, j)),
                  pl.BlockSpec((None, 1, d), lambda i, j: (i, 0, j))],
        out_specs=pl.BlockSpec((None, None, rows, d), lambda i, j: (i, j, 0, 0)),
        out_shape=jax.ShapeDtypeStruct((depth, N_MOD, rows, d), F32),
        compiler_params=_cparams(2),
        name="mod",
    )(cc, w_ada, b_ada.reshape(depth, 1, N_MOD * d))


def _ffn_half(x, mod_ref, j, g, w1_ref, w2_ref, acc_ref, chunk):
    d_ff = w2_ref.shape[0]
    shift = mod_ref[3 * j:3 * j + 1, :]
    scale = mod_ref[3 * j + 1:3 * j + 2, :]
    gate = mod_ref[3 * j + 2:3 * j + 3, :]
    h = _pre(x, g, shift, scale).astype(BF16)
    for c0 in range(0, d_ff, chunk):
        a = _dot(h, w1_ref[:, c0:c0 + chunk])
        u = _dot(h, w1_ref[:, d_ff + c0:d_ff + c0 + chunk])
        part = _dot((_silu(a) * u).astype(BF16), w2_ref[c0:c0 + chunk, :])
        if c0 == 0:
            acc_ref[...] = part
        else:
            acc_ref[...] += part
    return x + (0.5 * gate) * acc_ref[...]


def _ffn_in_kernel(chunk, x_ref, mod_ref, g_ref, w1_ref, w2_ref, wc_ref, wg_ref, wa_ref,
                   xo_ref, pc_ref, pg_ref, pa_ref, acc_ref):
    x1 = _ffn_half(x_ref[...], mod_ref, 0, g_ref[0], w1_ref, w2_ref, acc_ref, chunk)
    xo_ref[...] = x1
    h = _pre(x1, g_ref[1], mod_ref[3:4, :], mod_ref[4:5, :]).astype(BF16)
    pc_ref[...] = _dot(h, wc_ref[...])
    pg_ref[...] = _dot(h, wg_ref[...])
    pa_ref[...] = _dot(h, wa_ref[...])


def _ffn_in_call(x, mod_l, mod_row, g_l, w1_all, w2_all, wc_all, wg_all, wa_all, li, tm, chunk):
    b, t, d = x.shape
    tok = lambda bi, i: (bi, i, 0)
    widths = (wc_all.shape[2], wg_all.shape[2], wa_all.shape[2])
    return pl.pallas_call(
        functools.partial(_ffn_in_kernel, chunk),
        grid=(b, t // tm),
        in_specs=[pl.BlockSpec((None, tm, d), tok),
                  pl.BlockSpec((None, N_MOD, d), lambda bi, i: (mod_row(bi), 0, 0)),
                  _const_spec(g_l.shape), _layer_spec(w1_all, li, 0), _layer_spec(w2_all, li, 0),
                  _layer_spec(wc_all, li), _layer_spec(wg_all, li), _layer_spec(wa_all, li)],
        out_specs=[pl.BlockSpec((None, tm, d), tok)] + [pl.BlockSpec((None, tm, w), tok) for w in widths],
        out_shape=[jax.ShapeDtypeStruct(x.shape, F32)]
        + [jax.ShapeDtypeStruct((b, t, w), F32) for w in widths],
        scratch_shapes=[pltpu.VMEM((tm, d), F32)],
        compiler_params=_cparams(2),
        name="ffn_in",
    )(x, mod_l, g_l, w1_all, w2_all, wc_all, wg_all, wa_all)


def _conv_kernel(t, cw, pc_ref, wdw_ref, bdw_ref, cg_ref, cb_ref, gm_ref, o_ref, hbuf, h2_scr):
    r = CONV_ROWS
    halo = CONV_HALO
    hbuf[0:halo, :] = jnp.zeros((halo, cw), F32)
    hbuf[t + halo:t + 2 * halo, :] = jnp.zeros((halo, cw), F32)

    def glu(i, carry):
        r0 = pl.multiple_of(i * r, r)
        a = pc_ref[pl.ds(r0, r), 0:cw]
        gt = pc_ref[pl.ds(r0, r), cw:2 * cw]
        hbuf[pl.ds(r0 + halo, r), :] = a * _sigmoid(gt)
        return carry

    lax.fori_loop(0, t // r, glu, 0)
    gm = gm_ref[...]

    def tile(i, carry):
        r0 = pl.multiple_of(i * r, r)
        win = hbuf[pl.ds(r0, r + 2 * halo), :]
        acc = jnp.zeros((r, cw), F32)
        for res in range(SUBLANES):
            wr = win if res == 0 else pltpu.roll(win, r + 2 * halo - res, axis=0)
            for a in range(2 * halo // SUBLANES):
                s = SUBLANES * a + res
                if 1 <= s <= CONV_K:
                    acc = acc + wr[SUBLANES * a:SUBLANES * a + r, :] * wdw_ref[s - 1:s, :]
        h2_scr[pl.ds(r0, r), :] = acc + bdw_ref[...]
        return carry

    lax.fori_loop(0, t // r, tile, 0, unroll=2)

    nb = min(CONV_NORM_ROWS, t)

    def norm(i, carry):
        rows = pl.ds(pl.multiple_of(i * nb, nb), nb)
        h2 = h2_scr[rows, :]
        mu = _split_dot(h2, gm)
        dlt = h2 - mu
        var = _split_dot(dlt * dlt, gm)
        y = dlt * lax.rsqrt(var + EPS) * cg_ref[...] + cb_ref[...]
        o_ref[rows, :] = _silu(y).astype(o_ref.dtype)
        return carry

    lax.fori_loop(0, t // nb, norm, 0)


def _conv_call(pc, w_dw, b_dw, cn_g, cn_b, gm):
    b, t, two_cw = pc.shape
    cw = two_cw // 2
    return pl.pallas_call(
        functools.partial(_conv_kernel, t, cw),
        grid=(b,),
        in_specs=[pl.BlockSpec((None, t, two_cw), lambda bi: (bi, 0, 0)),
                  _const_spec(w_dw.shape), _const_spec(b_dw.shape), _const_spec(cn_g.shape),
                  _const_spec(cn_b.shape), _const_spec(gm.shape)],
        out_specs=pl.BlockSpec((None, t, cw), lambda bi: (bi, 0, 0)),
        out_shape=jax.ShapeDtypeStruct((b, t, cw), BF16),
        scratch_shapes=[pltpu.VMEM((t + 2 * CONV_HALO, cw), F32), pltpu.VMEM((t, cw), F32)],
        compiler_params=_cparams(1),
        name="conv",
    )(pc, w_dw, b_dw, cn_g, cn_b, gm)


def _gla_block(fwd, n_ch, dkw, dvw, pg_ref, wg_ref, bg_ref, o_ref, s_scr):
    c = GLA_CHUNK
    dk = dkw // GLA_HEADS
    dv = dvw // GLA_HEADS
    lr_off = 2 * dkw + 2 * dvw
    z = _dot(pg_ref[:, lr_off:lr_off + LANES], wg_ref[...]) + bg_ref[...]
    la_all = -(jnp.maximum(-z, 0.0) + jnp.log(1.0 + jnp.exp(-jnp.abs(z)))) * (1.0 / GLA_GATE_NORM)

    ri = lax.broadcasted_iota(jnp.int32, (c, c), 0)
    ci = lax.broadcasted_iota(jnp.int32, (c, c), 1)
    tri_m = jnp.where((ci <= ri) if fwd else (ci >= ri), 1.0, 0.0).astype(BF16)
    rih = lax.broadcasted_iota(jnp.int32, (c, GLA_HEADS * c), 0)
    cih = lax.broadcasted_iota(jnp.int32, (c, GLA_HEADS * c), 1) % c
    tri_h = (cih <= rih) if fwd else (cih >= rih)
    klane = lax.broadcasted_iota(jnp.int32, (c, dkw), 1) // dk
    vlane = lax.broadcasted_iota(jnp.int32, (c, dvw), 1) // dv
    srow = lax.broadcasted_iota(jnp.int32, (dvw, dkw), 0) // dv
    scol = lax.broadcasted_iota(jnp.int32, (dvw, dkw), 1) // dk
    smask = srow == scol

    chunks = range(n_ch)
    rows = [slice(j * c, (j + 1) * c) for j in chunks]
    bcum = [_split3_dot_left(tri_m, la_all[rows[j], :]) for j in chunks]
    q_e, k_end, kblk, vblk, vb, decay = [], [], [], [], [], []
    for j in chunks:
        q = pg_ref[rows[j], 0:dkw] * (dk ** -0.5)
        k = pg_ref[rows[j], dkw:2 * dkw]
        v = pg_ref[rows[j], 2 * dkw:2 * dkw + dvw]
        btot = bcum[j][c - 1:c, :] if fwd else bcum[j][0:1, :]
        q_e.append((q * jnp.exp(bcum[j])).astype(BF16))
        k_e = k * jnp.exp(-bcum[j])
        k_end.append((k * jnp.exp(btot - bcum[j])).astype(BF16))
        kblk.append(jnp.concatenate([jnp.where(klane == h, k_e, 0.0) for h in range(GLA_HEADS)],
                                    axis=0).astype(BF16))
        vblk.append(jnp.concatenate([jnp.where(vlane == h, v, 0.0) for h in range(GLA_HEADS)],
                                    axis=0).astype(BF16))
        vb.append(v.astype(BF16))
        decay.append(jnp.exp(btot))
    att = [jnp.where(tri_h, _dot_nt(q_e[j], kblk[j]), 0.0).astype(BF16) for j in chunks]
    o_intra = [_dot(att[j], vblk[j]) for j in chunks]
    upd = [jnp.where(smask, _dot_tn(vb[j], k_end[j]), 0.0) for j in chunks]
    s_t = s_scr[...]
    s_start = [None] * n_ch
    for j in (chunks if fwd else reversed(chunks)):
        s_start[j] = s_t.astype(BF16)
        s_t = s_t * decay[j] + upd[j]
    s_scr[...] = s_t
    for j in chunks:
        o_ref[rows[j], :] = o_intra[j] + _dot_nt(q_e[j], s_start[j])


def _gla_kernel(n_ch, dkw, dvw, pg_ref, wg_ref, bg_ref, s0_ref, o_ref, sfin_ref, s_scr):
    d = pl.program_id(1)
    i = pl.program_id(2)

    @pl.when(i == 0)
    def _():
        s_scr[...] = s0_ref[...]

    @pl.when(d == 0)
    def _():
        _gla_block(True, n_ch, dkw, dvw, pg_ref, wg_ref, bg_ref, o_ref, s_scr)

    @pl.when(d != 0)
    def _():
        _gla_block(False, n_ch, dkw, dvw, pg_ref, wg_ref, bg_ref, o_ref, s_scr)

    @pl.when(i == pl.num_programs(2) - 1)
    def _():
        sfin_ref[...] = s_scr[...]


def _gla_call(pg, wg2, bg2, s0, dkw, dvw):
    b, t, w = pg.shape
    blk = min(GLA_BLOCK, t)
    nb = t // blk
    n_ch = blk // GLA_CHUNK
    seq = lambda bi, d, i: (bi, jnp.where(d == 0, i, nb - 1 - i), 0)
    return pl.pallas_call(
        functools.partial(_gla_kernel, n_ch, dkw, dvw),
        grid=(b, 2, nb),
        in_specs=[pl.BlockSpec((None, blk, w), seq),
                  pl.BlockSpec((None, LANES, dkw), lambda bi, d, i: (d, 0, 0)),
                  pl.BlockSpec((None, 1, dkw), lambda bi, d, i: (d, 0, 0)),
                  pl.BlockSpec((None, None, dvw, dkw), lambda bi, d, i: (bi, d, 0, 0))],
        out_specs=[pl.BlockSpec((None, None, blk, dvw),
                                lambda bi, d, i: (d, bi, jnp.where(d == 0, i, nb - 1 - i), 0)),
                   pl.BlockSpec((None, None, dvw, dkw), lambda bi, d, i: (bi, d, 0, 0))],
        out_shape=[jax.ShapeDtypeStruct((2, b, t, dvw), F32),
                   jax.ShapeDtypeStruct((b, 2, dvw, dkw), F32)],
        scratch_shapes=[pltpu.VMEM((dvw, dkw), F32)],
        compiler_params=_cparams(3),
        name="gla",
    )(pg, wg2, bg2, s0)


def _kvprep_kernel(aw, pa_ref, cos_ref, sin_ref, qg_ref, kg_ref, gm_ref, q_ref, k_ref, vt_ref):
    hd = HEAD_DIM
    cos = cos_ref[...]
    sin = sin_ref[...]

    def norm_rope(x, g):
        ms = _split_dot(x * x, gm_ref[...])
        xn = x * lax.rsqrt(ms + EPS) * g
        return xn * cos + _rope_partner(xn) * sin

    for t0 in range(0, aw, LANES):
        q_ref[:, t0:t0 + LANES] = (norm_rope(pa_ref[:, t0:t0 + LANES], qg_ref[...])
                                   * (hd ** -0.5 * LOG2E)).astype(BF16)
    vv = pa_ref[:, aw + LANES:aw + 2 * LANES]
    kr = norm_rope(pa_ref[:, aw:aw + LANES], kg_ref[...])
    lane = lax.broadcasted_iota(jnp.int32, kr.shape, 1)
    lo = lane < hd
    kswap = pltpu.roll(kr, hd, axis=1)
    k0 = jnp.where(lo, kr, kswap).astype(BF16)
    k1 = jnp.where(lo, kswap, kr).astype(BF16)
    k_ref[...] = jnp.concatenate([k0, k0, k1, k1], axis=1)
    vt_ref[0] = jnp.where(lo, vv, 1.0).T[0:VT_ROWS, :].astype(BF16)
    vt_ref[1] = jnp.where(lo, pltpu.roll(vv, hd, axis=1), 1.0).T[0:VT_ROWS, :].astype(BF16)


def _kvprep_call(pa, cos_t, sin_t, qg, kg, gm, tt):
    b, t, w = pa.shape
    aw = w - 2 * LANES
    return pl.pallas_call(
        functools.partial(_kvprep_kernel, aw),
        grid=(b, t // tt),
        in_specs=[pl.BlockSpec((None, tt, w), lambda bi, i: (bi, i, 0)),
                  pl.BlockSpec((tt, LANES), lambda bi, i: (i, 0)),
                  pl.BlockSpec((tt, LANES), lambda bi, i: (i, 0)),
                  _const_spec(qg.shape), _const_spec(kg.shape), _const_spec(gm.shape)],
        out_specs=[pl.BlockSpec((None, tt, aw), lambda bi, i: (bi, i, 0)),
                   pl.BlockSpec((None, tt, 4 * LANES), lambda bi, i: (bi, i, 0)),
                   pl.BlockSpec((None, N_KV_HEADS, VT_ROWS, tt), lambda bi, i: (bi, 0, 0, i))],
        out_shape=[jax.ShapeDtypeStruct((b, t, aw), BF16),
                   jax.ShapeDtypeStruct((b, t, 4 * LANES), BF16),
                   jax.ShapeDtypeStruct((b, N_KV_HEADS, VT_ROWS, t), BF16)],
        compiler_params=_cparams(2),
        name="kvprep",
    )(pa, cos_t, sin_t, qg, kg, gm)


def _flash_kernel(seg_lens, tk_max, lookahead, stale_max, q_ref, *rest):
    n_seg = len(seg_lens)
    kv_refs = rest[:2 * n_seg]
    o_ref, qs_scr = rest[2 * n_seg:]
    hd = HEAD_DIM
    g_heads = 2 * LANES // hd
    qr = q_ref[...]
    lane_head = lax.broadcasted_iota(jnp.int32, qr.shape, 1) // hd
    for h in range(g_heads):
        qs_scr[h] = jnp.where(lane_head == h, qr, jnp.zeros_like(qr))

    tasks = []
    for h in range(g_heads):
        first = True
        for si in range(n_seg):
            k_ref, vt_ref = kv_refs[2 * si], kv_refs[2 * si + 1]
            tk = min(tk_max, seg_lens[si])
            for c0 in range(0, seg_lens[si], tk):
                tasks.append((h, k_ref, vt_ref, c0, tk, first))
                first = False

    def scores(task):
        h, k_ref, _, c0, tk, _ = task
        return _dot_nt(k_ref[c0:c0 + tk, :], qs_scr[h])

    def attend(stale_max):
        m_run = [None] * g_heads
        acc = [None] * g_heads
        rise = None
        pending = [scores(t) for t in tasks[:lookahead]]
        for n, (h, _, vt_ref, c0, tk, first) in enumerate(tasks):
            if n + lookahead < len(tasks):
                pending.append(scores(tasks[n + lookahead]))
            s = pending.pop(0)
            m_cur = jnp.max(s, axis=0, keepdims=True)
            if first:
                m_new = m_cur
                acc[h] = _dot(vt_ref[:, c0:c0 + tk], jnp.exp2(s - m_new).astype(BF16))
            elif stale_max:
                m_old = m_run[h]
                m_new = jnp.maximum(m_old, m_cur)
                pv = _dot(vt_ref[:, c0:c0 + tk], jnp.exp2(s - m_old).astype(BF16))
                acc[h] = (acc[h] + pv) * jnp.exp2(m_old - m_new)
                rise = m_cur - m_old if rise is None else jnp.maximum(rise, m_cur - m_old)
            else:
                m_new = jnp.maximum(m_run[h], m_cur)
                pv = _dot(vt_ref[:, c0:c0 + tk], jnp.exp2(s - m_new).astype(BF16))
                acc[h] = acc[h] * jnp.exp2(m_run[h] - m_new) + pv
            m_run[h] = m_new
        o_t = jnp.concatenate([acc[h][0:hd, :] * (1.0 / acc[h][hd:hd + 1, :]) for h in range(g_heads)],
                              axis=0)
        return o_t.T.astype(o_ref.dtype), rise

    if not stale_max:
        o_ref[...] = attend(False)[0]
    else:
        o_fast, rise = attend(True)
        o_ref[...] = o_fast
        if rise is not None:
            @pl.when(jnp.max(rise) > ATT_MAX_RISE)
            def _():
                o_ref[...] = attend(False)[0]


def _flash_call(qr, segs, tq, tk, lookahead, stale_max):
    b, t, _ = qr.shape
    tq = min(tq, t)
    seg_lens = tuple(k.shape[1] for k, _ in segs)
    in_specs = [pl.BlockSpec((None, tq, 2 * LANES), lambda bi, j, i: (bi, i, j))]
    args = [qr]
    for k, v in segs:
        in_specs.append(pl.BlockSpec((None, k.shape[1], 2 * LANES), lambda bi, j, i: (bi, 0, j)))
        in_specs.append(pl.BlockSpec((None, None, VT_ROWS, v.shape[3]), lambda bi, j, i: (bi, j, 0, 0)))
        args += [k, v]
    g_heads = 2 * LANES // HEAD_DIM
    return pl.pallas_call(
        functools.partial(_flash_kernel, seg_lens, tk, lookahead, stale_max),
        grid=(b, N_KV_HEADS, t // tq),
        in_specs=in_specs,
        out_specs=pl.BlockSpec((None, tq, 2 * LANES), lambda bi, j, i: (bi, i, j)),
        out_shape=jax.ShapeDtypeStruct((b, t, N_KV_HEADS * 2 * LANES), BF16),
        scratch_shapes=[pltpu.VMEM((g_heads, tq, 2 * LANES), BF16)],
        compiler_params=_cparams(3),
        name="flash",
    )(*args)


def _out_kernel(cw, dvw, x_ref, mod_ref, conv_ref, of_ref, ob_ref, r_ref, att_ref, gg_ref, gm_ref,
                wo_ref, o_ref):
    o = of_ref[...] + ob_ref[...]
    ms = _split_dot(o * o, gm_ref[...])
    gl = (o * lax.rsqrt(ms + EPS) * gg_ref[...]) * _silu(r_ref[...])
    y = (_dot(conv_ref[...], wo_ref[0:cw, :])
         + _dot(gl.astype(BF16), wo_ref[cw:cw + dvw, :])
         + _dot(att_ref[...], wo_ref[cw + dvw:, :]))
    o_ref[...] = x_ref[...] + mod_ref[5:6, :] * y


def _out_call(x, mod_l, mod_row, conv, o_gla, pg, att, gg, gm, wo_all, li, tm):
    b, t, d = x.shape
    cw = conv.shape[2]
    dvw = o_gla.shape[3]
    aw = att.shape[2]
    tok = lambda bi, i: (bi, i, 0)
    return pl.pallas_call(
        functools.partial(_out_kernel, cw, dvw),
        grid=(b, t // tm),
        in_specs=[pl.BlockSpec((None, tm, d), tok),
                  pl.BlockSpec((None, N_MOD, d), lambda bi, i: (mod_row(bi), 0, 0)),
                  pl.BlockSpec((None, tm, cw), tok),
                  pl.BlockSpec((None, None, tm, dvw), lambda bi, i: (0, bi, i, 0)),
                  pl.BlockSpec((None, None, tm, dvw), lambda bi, i: (1, bi, i, 0)),
                  pl.BlockSpec((None, tm, dvw), lambda bi, i: (bi, i, 2)),
                  pl.BlockSpec((None, tm, aw), tok),
                  _const_spec(gg.shape), _const_spec(gm.shape), _layer_spec(wo_all, li)],
        out_specs=pl.BlockSpec((None, tm, d), tok),
        out_shape=jax.ShapeDtypeStruct(x.shape, F32),
        compiler_params=_cparams(2),
        name="outproj",
    )(x, mod_l, conv, o_gla, o_gla, pg, att, gg, gm, wo_all)


def _ffn_kernel(j, chunk, x_ref, mod_ref, g_ref, w1_ref, w2_ref, o_ref, acc_ref):
    o_ref[...] = _ffn_half(x_ref[...], mod_ref, j, g_ref[j], w1_ref, w2_ref, acc_ref, chunk)


def _ffn_call(x, mod_l, mod_row, g_l, w1_all, w2_all, li, which, j, tm, chunk):
    b, t, d = x.shape
    tok = lambda bi, i: (bi, i, 0)
    return pl.pallas_call(
        functools.partial(_ffn_kernel, j, chunk),
        grid=(b, t // tm),
        in_specs=[pl.BlockSpec((None, tm, d), tok),
                  pl.BlockSpec((None, N_MOD, d), lambda bi, i: (mod_row(bi), 0, 0)),
                  _const_spec(g_l.shape), _layer_spec(w1_all, li, which), _layer_spec(w2_all, li, which)],
        out_specs=pl.BlockSpec((None, tm, d), tok),
        out_shape=jax.ShapeDtypeStruct(x.shape, F32),
        scratch_shapes=[pltpu.VMEM((tm, d), F32)],
        compiler_params=_cparams(2),
        name="ffn",
    )(x, mod_l, g_l, w1_all, w2_all)


def _rope_tables(s):
    nf = HEAD_DIM // 4
    rows_n = s // GRID_W
    row = jnp.repeat(jnp.arange(rows_n), GRID_W).astype(F32)
    col = jnp.tile(jnp.arange(GRID_W), rows_n).astype(F32)
    freqs = ROPE_BASE ** (-jnp.arange(nf, dtype=F32) / nf)
    ang_r = row[:, None] * freqs
    ang_c = col[:, None] * freqs
    cos = jnp.concatenate([jnp.cos(ang_r)] * 2 + [jnp.cos(ang_c)] * 2, axis=1)
    sin = jnp.concatenate([-jnp.sin(ang_r), jnp.sin(ang_r), -jnp.sin(ang_c), jnp.sin(ang_c)], axis=1)
    return cos, sin


def kernel(x, c, ctx, c_ctx, w_ada, b_ada, g_norm, w_ffn_in, w_ffn_out, w_in, w_out, w_dw, b_dw,
           conv_norm_g, conv_norm_b, w_gla_gate, b_gla_gate, gla_norm_g, q_norm_g, k_norm_g):
    b, s, d = x.shape
    l = ctx.shape[1]
    depth = w_ada.shape[0]
    d_ff = w_ffn_out.shape[2]
    cw = w_dw.shape[2]
    dkw = w_gla_gate.shape[3]
    dvw = gla_norm_g.shape[1] * gla_norm_g.shape[2]
    n_kv = N_KV_HEADS * HEAD_DIM
    aw = d - cw - dvw
    tm_c = min(TOKEN_TILE, b * l)
    assert all(d_ff % t[0] == 0 for t in LAYER_TILING) and s % TOKEN_TILE == 0 and l % GLA_CHUNK == 0 and (b * l) % tm_c == 0
    assert dkw == LANES and n_kv == LANES and cw == 2 * LANES and dvw == 2 * LANES and aw == 4 * LANES

    rows = -(-(b + 1) // SUBLANES) * SUBLANES
    cc = jnp.zeros((rows, d), F32).at[:b].set(c).at[b].set(c_ctx)
    mod = jnp.transpose(_mod_call(cc, w_ada, b_ada), (0, 2, 1, 3))
    row_x = lambda bi: bi
    row_c = lambda bi: b

    cos64, sin64 = _rope_tables(s)
    cos_k, sin_k = jnp.tile(cos64, (1, 2)), jnp.tile(sin64, (1, 2))
    one_k, zero_k = jnp.ones((l, LANES), F32), jnp.zeros((l, LANES), F32)

    gm64_128 = _group_mean_matrix(LANES, HEAD_DIM)
    gm_conv = _group_mean_matrix(cw, cw // CONV_GROUPS)
    gm_gla = _group_mean_matrix(dvw, dvw // GLA_HEADS)

    sizes = (cw, cw, dkw, dkw, dvw, dvw, GLA_RANK, GLA_RANK, aw, n_kv, n_kv)
    offs = np.concatenate([[0], np.cumsum(sizes)])

    w1_all = w_ffn_in.astype(BF16)
    w2_all = w_ffn_out.astype(BF16)
    wo_all = w_out.astype(BF16)
    wi = w_in.astype(BF16)
    wc_all = wi[:, :, offs[0]:offs[2]]
    wg_all = jnp.concatenate([wi[:, :, offs[2]:offs[8]],
                              jnp.zeros((depth, d, LANES - 2 * GLA_RANK), BF16)], axis=2)
    wa_all = wi[:, :, offs[8]:offs[11]]
    wg2_all = jnp.zeros((depth, 2, LANES, dkw), F32)
    wg2_all = wg2_all.at[:, 0, 0:GLA_RANK].set(w_gla_gate[:, 0])
    wg2_all = wg2_all.at[:, 1, GLA_RANK:2 * GLA_RANK].set(w_gla_gate[:, 1])
    weights = (w1_all, w2_all, wc_all, wg_all, wa_all)
    flat = lambda a: a.reshape(1, b * l, a.shape[2])

    for i in range(depth):
        last = i == depth - 1
        mod_l = mod[i]
        g_l = g_norm[i].reshape(3, 1, d)
        wg2 = wg2_all[i]
        bg2 = b_gla_gate[i].reshape(2, 1, dkw)
        qg = jnp.tile(q_norm_g[i], 2).reshape(1, LANES)
        kg = jnp.tile(k_norm_g[i], 2).reshape(1, LANES)
        gg = gla_norm_g[i].reshape(1, dvw)
        wdw, bdw = w_dw[i], b_dw[i].reshape(1, cw)
        cng, cnb = conv_norm_g[i].reshape(1, cw), conv_norm_b[i].reshape(1, cw)

        fchunk, ftile, tq, tk, la, stale = LAYER_TILING[i % len(LAYER_TILING)]
        x, pc_x, pg_x, pa_x = _ffn_in_call(x, mod_l, row_x, g_l, *weights, i, TOKEN_TILE, fchunk)
        ctx_f, pc_c, pg_c, pa_c = _ffn_in_call(flat(ctx), mod_l, row_c, g_l, *weights, i, tm_c, fchunk)
        ctx = ctx_f.reshape(b, l, d)
        pc_c, pg_c, pa_c = (p.reshape(b, l, p.shape[2]) for p in (pc_c, pg_c, pa_c))
        conv_x = _conv_call(pc_x, wdw, bdw, cng, cnb, gm_conv)
        o_c, s_c = _gla_call(pg_c, wg2, bg2, jnp.zeros((b, 2, dvw, dkw), F32), dkw, dvw)
        o_x, _ = _gla_call(pg_x, wg2, bg2, s_c, dkw, dvw)
        q_c, k_c, v_c = _kvprep_call(pa_c, one_k, zero_k, qg, kg, gm64_128, min(TOKEN_TILE, l))
        q_x, k_x, v_x = _kvprep_call(pa_x, cos_k, sin_k, qg, kg, gm64_128, TOKEN_TILE)
        att_x = _flash_call(q_x, [(k_c, v_c), (k_x, v_x)], tq, tk, la, stale)
        x = _out_call(x, mod_l, row_x, conv_x, o_x, pg_x, att_x, gg, gm_gla, wo_all, i, TOKEN_TILE)
        x = _ffn_call(x, mod_l, row_x, g_l, w1_all, w2_all, i, 1, 2, ftile, fchunk)
        if not last:
            conv_c = _conv_call(pc_c, wdw, bdw, cng, cnb, gm_conv)
            att_c = _flash_call(q_c, [(k_c, v_c)], tq, tk, la, False)
            ctx_f = _out_call(flat(ctx), mod_l, row_c, flat(conv_c), o_c.reshape(2, 1, b * l, dvw),
                              flat(pg_c), flat(att_c), gg, gm_gla, wo_all, i, tm_c)
            ctx = _ffn_call(ctx_f, mod_l, row_c, g_l, w1_all, w2_all, i, 1, 2, tm_c, fchunk).reshape(b, l, d)
    return x
```

```python
import functools

import numpy as np
import jax
import jax.numpy as jnp
from jax import lax
from jax.experimental import pallas as pl
from jax.experimental.pallas import tpu as pltpu

F32 = jnp.float32
BF16 = jnp.bfloat16

CONV_GROUPS = 4
CONV_K = 31
GLA_HEADS = 4
GLA_RANK = 16
GLA_GATE_NORM = 16.0
GLA_CHUNK = 64
HEAD_DIM = 64
N_KV_HEADS = 2
GRID_W = 64
ROPE_BASE = 10000.0
N_MOD = 9
EPS = 1e-6
LOG2E = 1.4426950408889634

LANES = 128
SUBLANES = 8
VMEM_LIMIT_BYTES = 56 * 1024 * 1024

TOKEN_TILE = 512
CONV_ROWS = 32
CONV_HALO = 16
CONV_NORM_ROWS = 1024
GLA_BLOCK = 512
VT_ROWS = 80
ATT_MAX_RISE = 60.0
LAYER_TILING = ((256, 1024, 512, 512, 4, True),
                (256, 1024, 512, 512, 2, True),
                (256, 1024, 512, 1024, 2, True),
                (256, 1024, 256, 512, 3, True))


def _cparams(n_axes):
    return pltpu.CompilerParams(dimension_semantics=("arbitrary",) * n_axes,
                                vmem_limit_bytes=VMEM_LIMIT_BYTES)


def _const_spec(shape):
    nd = len(shape)
    return pl.BlockSpec(shape, lambda *_: (0,) * nd, pipeline_mode=pl.Buffered(1))


def _layer_spec(arr, *idx):
    lead = len(idx)
    blk = (None,) * lead + tuple(arr.shape[lead:])
    tail = (0,) * (arr.ndim - lead)
    return pl.BlockSpec(blk, lambda *_: tuple(idx) + tail, pipeline_mode=pl.Buffered(1))


def _dot(a, b):
    return jnp.dot(a, b, preferred_element_type=F32)


def _dot_nt(a, b):
    return lax.dot_general(a, b, (((1,), (1,)), ((), ())), preferred_element_type=F32)


def _dot_tn(a, b):
    return lax.dot_general(a, b, (((0,), (0,)), ((), ())), preferred_element_type=F32)


def _sigmoid(x):
    return 1.0 / (1.0 + jnp.exp(-x))


def _silu(x):
    return x * _sigmoid(x)


def _split_dot(v, m):
    hi = v.astype(BF16)
    lo = (v - hi.astype(F32)).astype(BF16)
    return _dot(hi, m) + _dot(lo, m)


def _split3_dot_left(m, v):
    hi = v.astype(BF16)
    r1 = v - hi.astype(F32)
    mid = r1.astype(BF16)
    lo = (r1 - mid.astype(F32)).astype(BF16)
    return _dot(m, hi) + _dot(m, mid) + _dot(m, lo)


def _pre(x, g, shift, scale):
    r = lax.rsqrt(jnp.mean(x * x, axis=-1, keepdims=True) + EPS)
    return (x * r * g) * (1.0 + scale) + shift


def _group_mean_matrix(width, group):
    idx = np.arange(width) // group
    return jnp.asarray((idx[:, None] == idx[None, :]).astype(np.float32) / group, dtype=BF16)


def _rope_partner(x):
    n = x.shape[-1]
    lane = lax.broadcasted_iota(jnp.int32, x.shape, x.ndim - 1)
    up = pltpu.roll(x, n - 16, axis=x.ndim - 1)
    dn = pltpu.roll(x, 16, axis=x.ndim - 1)
    return jnp.where((lane % 32) < 16, up, dn)


def _mod_kernel(c_ref, w_ref, b_ref, o_ref):
    sc = _silu(c_ref[...]).astype(BF16)
    o_ref[...] = _dot(sc, w_ref[...].astype(BF16)) + b_ref[...]


def _mod_call(cc, w_ada, b_ada):
    depth, d, _ = w_ada.shape
    rows = cc.shape[0]
    return pl.pallas_call(
        _mod_kernel,
        grid=(depth, N_MOD),
        in_specs=[pl.BlockSpec((rows, d), lambda i, j: (0, 0)),
                  pl.BlockSpec((None, d, d), lambda i, j: (i, 0, j)),
                  pl.BlockSpec((None, 1, d), lambda i, j: (i, 0, j))],
        out_specs=pl.BlockSpec((None, None, rows, d), lambda i, j: (i, j, 0, 0)),
        out_shape=jax.ShapeDtypeStruct((depth, N_MOD, rows, d), F32),
        compiler_params=_cparams(2),
        name="mod",
    )(cc, w_ada, b_ada.reshape(depth, 1, N_MOD * d))


def _ffn_half(x, mod_ref, j, g, w1_ref, w2_ref, acc_ref, chunk):
    d_ff = w2_ref.shape[0]
    shift = mod_ref[3 * j:3 * j + 1, :]
    scale = mod_ref[3 * j + 1:3 * j + 2, :]
    gate = mod_ref[3 * j + 2:3 * j + 3, :]
    h = _pre(x, g, shift, scale).astype(BF16)
    for c0 in range(0, d_ff, chunk):
        a = _dot(h, w1_ref[:, c0:c0 + chunk])
        u = _dot(h, w1_ref[:, d_ff + c0:d_ff + c0 + chunk])
        part = _dot((_silu(a) * u).astype(BF16), w2_ref[c0:c0 + chunk, :])
        if c0 == 0:
            acc_ref[...] = part
        else:
            acc_ref[...] += part
    return x + (0.5 * gate) * acc_ref[...]


def _ffn_in_kernel(chunk, x_ref, mod_ref, g_ref, w1_ref, w2_ref, wc_ref, wg_ref, wa_ref,
                   xo_ref, pc_ref, pg_ref, pa_ref, acc_ref):
    x1 = _ffn_half(x_ref[...], mod_ref, 0, g_ref[0], w1_ref, w2_ref, acc_ref, chunk)
    xo_ref[...] = x1
    h = _pre(x1, g_ref[1], mod_ref[3:4, :], mod_ref[4:5, :]).astype(BF16)
    pc_ref[...] = _dot(h, wc_ref[...])
    pg_ref[...] = _dot(h, wg_ref[...])
    pa_ref[...] = _dot(h, wa_ref[...])


def _ffn_in_call(x, mod_l, mod_row, g_l, w1_all, w2_all, wc_all, wg_all, wa_all, li, tm, chunk):
    b, t, d = x.shape
    tok = lambda bi, i: (bi, i, 0)
    widths = (wc_all.shape[2], wg_all.shape[2], wa_all.shape[2])
    return pl.pallas_call(
        functools.partial(_ffn_in_kernel, chunk),
        grid=(b, t // tm),
        in_specs=[pl.BlockSpec((None, tm, d), tok),
                  pl.BlockSpec((None, N_MOD, d), lambda bi, i: (mod_row(bi), 0, 0)),
                  _const_spec(g_l.shape), _layer_spec(w1_all, li, 0), _layer_spec(w2_all, li, 0),
                  _layer_spec(wc_all, li), _layer_spec(wg_all, li), _layer_spec(wa_all, li)],
        out_specs=[pl.BlockSpec((None, tm, d), tok)] + [pl.BlockSpec((None, tm, w), tok) for w in widths],
        out_shape=[jax.ShapeDtypeStruct(x.shape, F32)]
        + [jax.ShapeDtypeStruct((b, t, w), F32) for w in widths],
        scratch_shapes=[pltpu.VMEM((tm, d), F32)],
        compiler_params=_cparams(2),
        name="ffn_in",
    )(x, mod_l, g_l, w1_all, w2_all, wc_all, wg_all, wa_all)


def _conv_kernel(t, cw, pc_ref, wdw_ref, bdw_ref, cg_ref, cb_ref, gm_ref, o_ref, hbuf, h2_scr):
    r = CONV_ROWS
    halo = CONV_HALO
    hbuf[0:halo, :] = jnp.zeros((halo, cw), F32)
    hbuf[t + halo:t + 2 * halo, :] = jnp.zeros((halo, cw), F32)

    def glu(i, carry):
        r0 = pl.multiple_of(i * r, r)
        a = pc_ref[pl.ds(r0, r), 0:cw]
        gt = pc_ref[pl.ds(r0, r), cw:2 * cw]
        hbuf[pl.ds(r0 + halo, r), :] = a * _sigmoid(gt)
        return carry

    lax.fori_loop(0, t // r, glu, 0)
    gm = gm_ref[...]

    def tile(i, carry):
        r0 = pl.multiple_of(i * r, r)
        win = hbuf[pl.ds(r0, r + 2 * halo), :]
        acc = jnp.zeros((r, cw), F32)
        for res in range(SUBLANES):
            wr = win if res == 0 else pltpu.roll(win, r + 2 * halo - res, axis=0)
            for a in range(2 * halo // SUBLANES):
                s = SUBLANES * a + res
                if 1 <= s <= CONV_K:
                    acc = acc + wr[SUBLANES * a:SUBLANES * a + r, :] * wdw_ref[s - 1:s, :]
        h2_scr[pl.ds(r0, r), :] = acc + bdw_ref[...]
        return carry

    lax.fori_loop(0, t // r, tile, 0, unroll=2)

    nb = min(CONV_NORM_ROWS, t)

    def norm(i, carry):
        rows = pl.ds(pl.multiple_of(i * nb, nb), nb)
        h2 = h2_scr[rows, :]
        mu = _split_dot(h2, gm)
        dlt = h2 - mu
        var = _split_dot(dlt * dlt, gm)
        y = dlt * lax.rsqrt(var + EPS) * cg_ref[...] + cb_ref[...]
        o_ref[rows, :] = _silu(y).astype(o_ref.dtype)
        return carry

    lax.fori_loop(0, t // nb, norm, 0)


def _conv_call(pc, w_dw, b_dw, cn_g, cn_b, gm):
    b, t, two_cw = pc.shape
    cw = two_cw // 2
    return pl.pallas_call(
        functools.partial(_conv_kernel, t, cw),
        grid=(b,),
        in_specs=[pl.BlockSpec((None, t, two_cw), lambda bi: (bi, 0, 0)),
                  _const_spec(w_dw.shape), _const_spec(b_dw.shape), _const_spec(cn_g.shape),
                  _const_spec(cn_b.shape), _const_spec(gm.shape)],
        out_specs=pl.BlockSpec((None, t, cw), lambda bi: (bi, 0, 0)),
        out_shape=jax.ShapeDtypeStruct((b, t, cw), BF16),
        scratch_shapes=[pltpu.VMEM((t + 2 * CONV_HALO, cw), F32), pltpu.VMEM((t, cw), F32)],
        compiler_params=_cparams(1),
        name="conv",
    )(pc, w_dw, b_dw, cn_g, cn_b, gm)


def _gla_blocks(n_ch, dkw, dvw, pgs, wg_ref, bg_ref, outs, s_scr):
    c = GLA_CHUNK
    dk = dkw // GLA_HEADS
    dv = dvw // GLA_HEADS
    lr_off = 2 * dkw + 2 * dvw
    ri = lax.broadcasted_iota(jnp.int32, (c, c), 0)
    ci = lax.broadcasted_iota(jnp.int32, (c, c), 1)
    rih = lax.broadcasted_iota(jnp.int32, (c, GLA_HEADS * c), 0)
    cih = lax.broadcasted_iota(jnp.int32, (c, GLA_HEADS * c), 1) % c
    tri_m = [jnp.where(ci <= ri, 1.0, 0.0).astype(BF16), jnp.where(ci >= ri, 1.0, 0.0).astype(BF16)]
    tri_h = [cih <= rih, cih >= rih]
    klane = lax.broadcasted_iota(jnp.int32, (c, dkw), 1) // dk
    vlane = lax.broadcasted_iota(jnp.int32, (c, dvw), 1) // dv
    srow = lax.broadcasted_iota(jnp.int32, (dvw, dkw), 0) // dv
    scol = lax.broadcasted_iota(jnp.int32, (dvw, dkw), 1) // dk
    smask = srow == scol

    units = [(d, j) for d in range(2) for j in range(n_ch)]
    rows = [slice(j * c, (j + 1) * c) for j in range(n_ch)]
    la = []
    for d in range(2):
        z = _dot(pgs[d][:, lr_off:lr_off + LANES], wg_ref[d]) + bg_ref[d]
        la.append(-(jnp.maximum(-z, 0.0) + jnp.log(1.0 + jnp.exp(-jnp.abs(z)))) * (1.0 / GLA_GATE_NORM))
    bcum = {u: _split3_dot_left(tri_m[u[0]], la[u[0]][rows[u[1]], :]) for u in units}
    q_e, k_end, kblk, vblk, vb, decay = {}, {}, {}, {}, {}, {}
    for u in units:
        d, j = u
        q = pgs[d][rows[j], 0:dkw] * (dk ** -0.5)
        k = pgs[d][rows[j], dkw:2 * dkw]
        v = pgs[d][rows[j], 2 * dkw:2 * dkw + dvw]
        btot = bcum[u][c - 1:c, :] if d == 0 else bcum[u][0:1, :]
        q_e[u] = (q * jnp.exp(bcum[u])).astype(BF16)
        k_e = k * jnp.exp(-bcum[u])
        k_end[u] = (k * jnp.exp(btot - bcum[u])).astype(BF16)
        kblk[u] = jnp.concatenate([jnp.where(klane == h, k_e, 0.0) for h in range(GLA_HEADS)],
                                  axis=0).astype(BF16)
        vblk[u] = jnp.concatenate([jnp.where(vlane == h, v, 0.0) for h in range(GLA_HEADS)],
                                  axis=0).astype(BF16)
        vb[u] = v.astype(BF16)
        decay[u] = jnp.exp(btot)
    att = {u: jnp.where(tri_h[u[0]], _dot_nt(q_e[u], kblk[u]), 0.0).astype(BF16) for u in units}
    o_intra = {u: _dot(att[u], vblk[u]) for u in units}
    upd = {u: jnp.where(smask, _dot_tn(vb[u], k_end[u]), 0.0) for u in units}
    s_start = {}
    for d in range(2):
        s_t = s_scr[d]
        for j in (range(n_ch) if d == 0 else reversed(range(n_ch))):
            s_start[(d, j)] = s_t.astype(BF16)
            s_t = s_t * decay[(d, j)] + upd[(d, j)]
        s_scr[d] = s_t
    for u in units:
        outs[u[0]][rows[u[1]], :] = o_intra[u] + _dot_nt(q_e[u], s_start[u])


def _gla_kernel(n_ch, dkw, dvw, pgf_ref, pgb_ref, wg_ref, bg_ref, s0_ref, of_ref, ob_ref, sfin_ref, s_scr):
    i = pl.program_id(1)

    @pl.when(i == 0)
    def _():
        s_scr[...] = s0_ref[...]

    _gla_blocks(n_ch, dkw, dvw, (pgf_ref, pgb_ref), wg_ref, bg_ref, (of_ref, ob_ref), s_scr)

    @pl.when(i == pl.num_programs(1) - 1)
    def _():
        sfin_ref[...] = s_scr[...]


def _gla_call(pg, wg2, bg2, s0, dkw, dvw):
    b, t, w = pg.shape
    blk = min(GLA_BLOCK, t)
    nb = t // blk
    n_ch = blk // GLA_CHUNK
    return pl.pallas_call(
        functools.partial(_gla_kernel, n_ch, dkw, dvw),
        grid=(b, nb),
        in_specs=[pl.BlockSpec((None, blk, w), lambda bi, i: (bi, i, 0)),
                  pl.BlockSpec((None, blk, w), lambda bi, i: (bi, nb - 1 - i, 0)),
                  _const_spec(wg2.shape), _const_spec(bg2.shape),
                  pl.BlockSpec((None, 2, dvw, dkw), lambda bi, i: (bi, 0, 0, 0))],
        out_specs=[pl.BlockSpec((None, blk, dvw), lambda bi, i: (bi, i, 0)),
                   pl.BlockSpec((None, blk, dvw), lambda bi, i: (bi, nb - 1 - i, 0)),
                   pl.BlockSpec((None, 2, dvw, dkw), lambda bi, i: (bi, 0, 0, 0))],
        out_shape=[jax.ShapeDtypeStruct((b, t, dvw), F32), jax.ShapeDtypeStruct((b, t, dvw), F32),
                   jax.ShapeDtypeStruct((b, 2, dvw, dkw), F32)],
        scratch_shapes=[pltpu.VMEM((2, dvw, dkw), F32)],
        compiler_params=_cparams(2),
        name="gla",
    )(pg, pg, wg2, bg2, s0)


def _kvprep_kernel(aw, pa_ref, cos_ref, sin_ref, qg_ref, kg_ref, gm_ref, q_ref, k_ref, vt_ref):
    hd = HEAD_DIM
    cos = cos_ref[...]
    sin = sin_ref[...]

    def norm_rope(x, g):
        ms = _dot((x * x).astype(BF16), gm_ref[...])
        xn = x * lax.rsqrt(ms + EPS) * g
        return xn * cos + _rope_partner(xn) * sin

    for t0 in range(0, aw, LANES):
        q_ref[:, t0:t0 + LANES] = (norm_rope(pa_ref[:, t0:t0 + LANES], qg_ref[...])
                                   * (hd ** -0.5 * LOG2E)).astype(BF16)
    vv = pa_ref[:, aw + LANES:aw + 2 * LANES]
    kr = norm_rope(pa_ref[:, aw:aw + LANES], kg_ref[...])
    lane = lax.broadcasted_iota(jnp.int32, kr.shape, 1)
    lo = lane < hd
    kswap = pltpu.roll(kr, hd, axis=1)
    k0 = jnp.where(lo, kr, kswap).astype(BF16)
    k1 = jnp.where(lo, kswap, kr).astype(BF16)
    k_ref[...] = jnp.concatenate([k0, k0, k1, k1], axis=1)
    vt_ref[0] = jnp.where(lo, vv, 1.0).T[0:VT_ROWS, :].astype(BF16)
    vt_ref[1] = jnp.where(lo, pltpu.roll(vv, hd, axis=1), 1.0).T[0:VT_ROWS, :].astype(BF16)


def _kvprep_call(pa, cos_t, sin_t, qg, kg, gm, tt):
    b, t, w = pa.shape
    aw = w - 2 * LANES
    return pl.pallas_call(
        functools.partial(_kvprep_kernel, aw),
        grid=(b, t // tt),
        in_specs=[pl.BlockSpec((None, tt, w), lambda bi, i: (bi, i, 0)),
                  pl.BlockSpec((tt, LANES), lambda bi, i: (i, 0)),
                  pl.BlockSpec((tt, LANES), lambda bi, i: (i, 0)),
                  _const_spec(qg.shape), _const_spec(kg.shape), _const_spec(gm.shape)],
        out_specs=[pl.BlockSpec((None, tt, aw), lambda bi, i: (bi, i, 0)),
                   pl.BlockSpec((None, tt, 4 * LANES), lambda bi, i: (bi, i, 0)),
                   pl.BlockSpec((None, N_KV_HEADS, VT_ROWS, tt), lambda bi, i: (bi, 0, 0, i))],
        out_shape=[jax.ShapeDtypeStruct((b, t, aw), BF16),
                   jax.ShapeDtypeStruct((b, t, 4 * LANES), BF16),
                   jax.ShapeDtypeStruct((b, N_KV_HEADS, VT_ROWS, t), BF16)],
        compiler_params=_cparams(2),
        name="kvprep",
    )(pa, cos_t, sin_t, qg, kg, gm)


def _flash_kernel(seg_lens, tk_max, lookahead, stale_max, q_ref, *rest):
    n_seg = len(seg_lens)
    kv_refs = rest[:2 * n_seg]
    o_ref, qs_scr = rest[2 * n_seg:]
    hd = HEAD_DIM
    g_heads = 2 * LANES // hd
    qr = q_ref[...]
    lane_head = lax.broadcasted_iota(jnp.int32, qr.shape, 1) // hd
    for h in range(g_heads):
        qs_scr[h] = jnp.where(lane_head == h, qr, jnp.zeros_like(qr))

    tasks = []
    for h in range(g_heads):
        first = True
        for si in range(n_seg):
            k_ref, vt_ref = kv_refs[2 * si], kv_refs[2 * si + 1]
            tk = min(tk_max, seg_lens[si])
            for c0 in range(0, seg_lens[si], tk):
                tasks.append((h, k_ref, vt_ref, c0, tk, first))
                first = False

    def scores(task):
        h, k_ref, _, c0, tk, _ = task
        return _dot_nt(k_ref[c0:c0 + tk, :], qs_scr[h])

    def attend(stale_max):
        m_run = [None] * g_heads
        acc = [None] * g_heads
        rise = None
        pending = [scores(t) for t in tasks[:lookahead]]
        for n, (h, _, vt_ref, c0, tk, first) in enumerate(tasks):
            if n + lookahead < len(tasks):
                pending.append(scores(tasks[n + lookahead]))
            s = pending.pop(0)
            m_cur = jnp.max(s, axis=0, keepdims=True)
            if first:
                m_new = m_cur
                acc[h] = _dot(vt_ref[:, c0:c0 + tk], jnp.exp2(s - m_new).astype(BF16))
            elif stale_max:
                m_old = m_run[h]
                m_new = jnp.maximum(m_old, m_cur)
                pv = _dot(vt_ref[:, c0:c0 + tk], jnp.exp2(s - m_old).astype(BF16))
                acc[h] = (acc[h] + pv) * jnp.exp2(m_old - m_new)
                rise = m_cur - m_old if rise is None else jnp.maximum(rise, m_cur - m_old)
            else:
                m_new = jnp.maximum(m_run[h], m_cur)
                pv = _dot(vt_ref[:, c0:c0 + tk], jnp.exp2(s - m_new).astype(BF16))
                acc[h] = acc[h] * jnp.exp2(m_run[h] - m_new) + pv
            m_run[h] = m_new
        o_t = jnp.concatenate([acc[h][0:hd, :] * (1.0 / acc[h][hd:hd + 1, :]) for h in range(g_heads)],
                              axis=0)
        return o_t.T.astype(o_ref.dtype), rise

    if not stale_max:
        o_ref[...] = attend(False)[0]
    else:
        o_fast, rise = attend(True)
        o_ref[...] = o_fast
        if rise is not None:
            @pl.when(jnp.max(rise) > ATT_MAX_RISE)
            def _():
                o_ref[...] = attend(False)[0]


def _flash_call(qr, segs, tq, tk, lookahead, stale_max):
    b, t, _ = qr.shape
    tq = min(tq, t)
    seg_lens = tuple(k.shape[1] for k, _ in segs)
    in_specs = [pl.BlockSpec((None, tq, 2 * LANES), lambda bi, j, i: (bi, i, j))]
    args = [qr]
    for k, v in segs:
        in_specs.append(pl.BlockSpec((None, k.shape[1], 2 * LANES), lambda bi, j, i: (bi, 0, j)))
        in_specs.append(pl.BlockSpec((None, None, VT_ROWS, v.shape[3]), lambda bi, j, i: (bi, j, 0, 0)))
        args += [k, v]
    g_heads = 2 * LANES // HEAD_DIM
    return pl.pallas_call(
        functools.partial(_flash_kernel, seg_lens, tk, lookahead, stale_max),
        grid=(b, N_KV_HEADS, t // tq),
        in_specs=in_specs,
        out_specs=pl.BlockSpec((None, tq, 2 * LANES), lambda bi, j, i: (bi, i, j)),
        out_shape=jax.ShapeDtypeStruct((b, t, N_KV_HEADS * 2 * LANES), BF16),
        scratch_shapes=[pltpu.VMEM((g_heads, tq, 2 * LANES), BF16)],
        compiler_params=_cparams(3),
        name="flash",
    )(*args)


def _out_kernel(cw, dvw, x_ref, mod_ref, conv_ref, of_ref, ob_ref, r_ref, att_ref, gg_ref, gm_ref,
                wo_ref, o_ref):
    o = of_ref[...] + ob_ref[...]
    ms = _split_dot(o * o, gm_ref[...])
    gl = (o * lax.rsqrt(ms + EPS) * gg_ref[...]) * _silu(r_ref[...])
    y = (_dot(conv_ref[...], wo_ref[0:cw, :])
         + _dot(gl.astype(BF16), wo_ref[cw:cw + dvw, :])
         + _dot(att_ref[...], wo_ref[cw + dvw:, :]))
    o_ref[...] = x_ref[...] + mod_ref[5:6, :] * y


def _out_call(x, mod_l, mod_row, conv, o_f, o_b, pg, att, gg, gm, wo_all, li, tm):
    b, t, d = x.shape
    cw = conv.shape[2]
    dvw = o_f.shape[2]
    aw = att.shape[2]
    tok = lambda bi, i: (bi, i, 0)
    return pl.pallas_call(
        functools.partial(_out_kernel, cw, dvw),
        grid=(b, t // tm),
        in_specs=[pl.BlockSpec((None, tm, d), tok),
                  pl.BlockSpec((None, N_MOD, d), lambda bi, i: (mod_row(bi), 0, 0)),
                  pl.BlockSpec((None, tm, cw), tok),
                  pl.BlockSpec((None, tm, dvw), tok),
                  pl.BlockSpec((None, tm, dvw), tok),
                  pl.BlockSpec((None, tm, dvw), lambda bi, i: (bi, i, 2)),
                  pl.BlockSpec((None, tm, aw), tok),
                  _const_spec(gg.shape), _const_spec(gm.shape), _layer_spec(wo_all, li)],
        out_specs=pl.BlockSpec((None, tm, d), tok),
        out_shape=jax.ShapeDtypeStruct(x.shape, F32),
        compiler_params=_cparams(2),
        name="outproj",
    )(x, mod_l, conv, o_f, o_b, pg, att, gg, gm, wo_all)


def _ffn_kernel(j, chunk, x_ref, mod_ref, g_ref, w1_ref, w2_ref, o_ref, acc_ref):
    o_ref[...] = _ffn_half(x_ref[...], mod_ref, j, g_ref[j], w1_ref, w2_ref, acc_ref, chunk)


def _ffn_call(x, mod_l, mod_row, g_l, w1_all, w2_all, li, which, j, tm, chunk):
    b, t, d = x.shape
    tok = lambda bi, i: (bi, i, 0)
    return pl.pallas_call(
        functools.partial(_ffn_kernel, j, chunk),
        grid=(b, t // tm),
        in_specs=[pl.BlockSpec((None, tm, d), tok),
                  pl.BlockSpec((None, N_MOD, d), lambda bi, i: (mod_row(bi), 0, 0)),
                  _const_spec(g_l.shape), _layer_spec(w1_all, li, which), _layer_spec(w2_all, li, which)],
        out_specs=pl.BlockSpec((None, tm, d), tok),
        out_shape=jax.ShapeDtypeStruct(x.shape, F32),
        scratch_shapes=[pltpu.VMEM((tm, d), F32)],
        compiler_params=_cparams(2),
        name="ffn",
    )(x, mod_l, g_l, w1_all, w2_all)


def _rope_tables(s):
    nf = HEAD_DIM // 4
    rows_n = s // GRID_W
    row = jnp.repeat(jnp.arange(rows_n), GRID_W).astype(F32)
    col = jnp.tile(jnp.arange(GRID_W), rows_n).astype(F32)
    freqs = ROPE_BASE ** (-jnp.arange(nf, dtype=F32) / nf)
    ang_r = row[:, None] * freqs
    ang_c = col[:, None] * freqs
    cos = jnp.concatenate([jnp.cos(ang_r)] * 2 + [jnp.cos(ang_c)] * 2, axis=1)
    sin = jnp.concatenate([-jnp.sin(ang_r), jnp.sin(ang_r), -jnp.sin(ang_c), jnp.sin(ang_c)], axis=1)
    return cos, sin


def kernel(x, c, ctx, c_ctx, w_ada, b_ada, g_norm, w_ffn_in, w_ffn_out, w_in, w_out, w_dw, b_dw,
           conv_norm_g, conv_norm_b, w_gla_gate, b_gla_gate, gla_norm_g, q_norm_g, k_norm_g):
    b, s, d = x.shape
    l = ctx.shape[1]
    depth = w_ada.shape[0]
    d_ff = w_ffn_out.shape[2]
    cw = w_dw.shape[2]
    dkw = w_gla_gate.shape[3]
    dvw = gla_norm_g.shape[1] * gla_norm_g.shape[2]
    n_kv = N_KV_HEADS * HEAD_DIM
    aw = d - cw - dvw
    tm_c = min(TOKEN_TILE, b * l)
    assert all(d_ff % t[0] == 0 for t in LAYER_TILING) and s % TOKEN_TILE == 0 and l % GLA_CHUNK == 0 and (b * l) % tm_c == 0
    assert dkw == LANES and n_kv == LANES and cw == 2 * LANES and dvw == 2 * LANES and aw == 4 * LANES

    rows = -(-(b + 1) // SUBLANES) * SUBLANES
    cc = jnp.zeros((rows, d), F32).at[:b].set(c).at[b].set(c_ctx)
    mod = jnp.transpose(_mod_call(cc, w_ada, b_ada), (0, 2, 1, 3))
    row_x = lambda bi: bi
    row_c = lambda bi: b

    cos64, sin64 = _rope_tables(s)
    cos_k, sin_k = jnp.tile(cos64, (1, 2)), jnp.tile(sin64, (1, 2))
    one_k, zero_k = jnp.ones((l, LANES), F32), jnp.zeros((l, LANES), F32)

    gm64_128 = _group_mean_matrix(LANES, HEAD_DIM)
    gm_conv = _group_mean_matrix(cw, cw // CONV_GROUPS)
    gm_gla = _group_mean_matrix(dvw, dvw // GLA_HEADS)

    sizes = (cw, cw, dkw, dkw, dvw, dvw, GLA_RANK, GLA_RANK, aw, n_kv, n_kv)
    offs = np.concatenate([[0], np.cumsum(sizes)])

    w1_all = w_ffn_in.astype(BF16)
    w2_all = w_ffn_out.astype(BF16)
    wo_all = w_out.astype(BF16)
    wi = w_in.astype(BF16)
    wc_all = wi[:, :, offs[0]:offs[2]]
    wg_all = jnp.concatenate([wi[:, :, offs[2]:offs[8]],
                              jnp.zeros((depth, d, LANES - 2 * GLA_RANK), BF16)], axis=2)
    wa_all = wi[:, :, offs[8]:offs[11]]
    wg2_all = jnp.zeros((depth, 2, LANES, dkw), F32)
    wg2_all = wg2_all.at[:, 0, 0:GLA_RANK].set(w_gla_gate[:, 0])
    wg2_all = wg2_all.at[:, 1, GLA_RANK:2 * GLA_RANK].set(w_gla_gate[:, 1])
    weights = (w1_all, w2_all, wc_all, wg_all, wa_all)
    flat = lambda a: a.reshape(1, b * l, a.shape[2])

    for i in range(depth):
        last = i == depth - 1
        mod_l = mod[i]
        g_l = g_norm[i].reshape(3, 1, d)
        wg2 = wg2_all[i]
        bg2 = b_gla_gate[i].reshape(2, 1, dkw)
        qg = jnp.tile(q_norm_g[i], 2).reshape(1, LANES)
        kg = jnp.tile(k_norm_g[i], 2).reshape(1, LANES)
        gg = gla_norm_g[i].reshape(1, dvw)
        wdw, bdw = w_dw[i], b_dw[i].reshape(1, cw)
        cng, cnb = conv_norm_g[i].reshape(1, cw), conv_norm_b[i].reshape(1, cw)

        fchunk, ftile, tq, tk, la, stale = LAYER_TILING[i % len(LAYER_TILING)]
        x, pc_x, pg_x, pa_x = _ffn_in_call(x, mod_l, row_x, g_l, *weights, i, TOKEN_TILE, fchunk)
        ctx_f, pc_c, pg_c, pa_c = _ffn_in_call(flat(ctx), mod_l, row_c, g_l, *weights, i, tm_c, fchunk)
        ctx = ctx_f.reshape(b, l, d)
        pc_c, pg_c, pa_c = (p.reshape(b, l, p.shape[2]) for p in (pc_c, pg_c, pa_c))
        conv_x = _conv_call(pc_x, wdw, bdw, cng, cnb, gm_conv)
        of_c, ob_c, s_c = _gla_call(pg_c, wg2, bg2, jnp.zeros((b, 2, dvw, dkw), F32), dkw, dvw)
        of_x, ob_x, _ = _gla_call(pg_x, wg2, bg2, s_c, dkw, dvw)
        q_c, k_c, v_c = _kvprep_call(pa_c, one_k, zero_k, qg, kg, gm64_128, min(TOKEN_TILE, l))
        q_x, k_x, v_x = _kvprep_call(pa_x, cos_k, sin_k, qg, kg, gm64_128, TOKEN_TILE)
        att_x = _flash_call(q_x, [(k_c, v_c), (k_x, v_x)], tq, tk, la, stale)
        x = _out_call(x, mod_l, row_x, conv_x, of_x, ob_x, pg_x, att_x, gg, gm_gla, wo_all, i, TOKEN_TILE)
        x = _ffn_call(x, mod_l, row_x, g_l, w1_all, w2_all, i, 1, 2, ftile, fchunk)
        if not last:
            conv_c = _conv_call(pc_c, wdw, bdw, cng, cnb, gm_conv)
            att_c = _flash_call(q_c, [(k_c, v_c)], tq, tk, la, False)
            ctx_f = _out_call(flat(ctx), mod_l, row_c, flat(conv_c), flat(of_c), flat(ob_c),
                              flat(pg_c), flat(att_c), gg, gm_gla, wo_all, i, tm_c)
            ctx = _ffn_call(ctx_f, mod_l, row_c, g_l, w1_all, w2_all, i, 1, 2, tm_c, fchunk).reshape(b, l, d)
    return x
```

```python
import functools

import numpy as np
import jax
import jax.numpy as jnp
from jax import lax
from jax.experimental import pallas as pl
from jax.experimental.pallas import tpu as pltpu

F32 = jnp.float32
BF16 = jnp.bfloat16

CONV_GROUPS = 4
CONV_K = 31
GLA_HEADS = 4
GLA_RANK = 16
GLA_GATE_NORM = 16.0
GLA_CHUNK = 64
HEAD_DIM = 64
N_KV_HEADS = 2
GRID_W = 64
ROPE_BASE = 10000.0
N_MOD = 9
EPS = 1e-6
LOG2E = 1.4426950408889634

LANES = 128
SUBLANES = 8
VMEM_LIMIT_BYTES = 56 * 1024 * 1024

TOKEN_TILE = 512
FFN2_TILE = 1024
FFN_CHUNK = 256
CONV_ROWS = 32
CONV_HALO = 16
CONV_NORM_ROWS = 1024
GLA_BLOCK = 512
VT_ROWS = 80
ATT_TQ = 256
ATT_TK = 512
ATT_LOOKAHEAD = 3
ATT_MAX_RISE = 60.0


def _cparams(n_axes):
    return pltpu.CompilerParams(dimension_semantics=("arbitrary",) * n_axes,
                                vmem_limit_bytes=VMEM_LIMIT_BYTES)


def _const_spec(shape):
    nd = len(shape)
    return pl.BlockSpec(shape, lambda *_: (0,) * nd, pipeline_mode=pl.Buffered(1))


def _layer_spec(arr, *idx):
    lead = len(idx)
    blk = (None,) * lead + tuple(arr.shape[lead:])
    tail = (0,) * (arr.ndim - lead)
    return pl.BlockSpec(blk, lambda *_: tuple(idx) + tail, pipeline_mode=pl.Buffered(1))


def _dot(a, b):
    return jnp.dot(a, b, preferred_element_type=F32)


def _dot_nt(a, b):
    return lax.dot_general(a, b, (((1,), (1,)), ((), ())), preferred_element_type=F32)


def _dot_tn(a, b):
    return lax.dot_general(a, b, (((0,), (0,)), ((), ())), preferred_element_type=F32)


def _sigmoid(x):
    return 1.0 / (1.0 + jnp.exp(-x))


def _silu(x):
    return x * _sigmoid(x)


def _split_dot(v, m):
    hi = v.astype(BF16)
    lo = (v - hi.astype(F32)).astype(BF16)
    return _dot(hi, m) + _dot(lo, m)


def _split3_dot_left(m, v):
    hi = v.astype(BF16)
    r1 = v - hi.astype(F32)
    mid = r1.astype(BF16)
    lo = (r1 - mid.astype(F32)).astype(BF16)
    return _dot(m, hi) + _dot(m, mid) + _dot(m, lo)


def _pre(x, g, shift, scale):
    r = lax.rsqrt(jnp.mean(x * x, axis=-1, keepdims=True) + EPS)
    return (x * r * g) * (1.0 + scale) + shift


def _group_mean_matrix(width, group):
    idx = np.arange(width) // group
    return jnp.asarray((idx[:, None] == idx[None, :]).astype(np.float32) / group, dtype=BF16)


def _rope_partner(x):
    n = x.shape[-1]
    lane = lax.broadcasted_iota(jnp.int32, x.shape, x.ndim - 1)
    up = pltpu.roll(x, n - 16, axis=x.ndim - 1)
    dn = pltpu.roll(x, 16, axis=x.ndim - 1)
    return jnp.where((lane % 32) < 16, up, dn)


def _mod_kernel(c_ref, w_ref, b_ref, o_ref):
    sc = _silu(c_ref[...]).astype(BF16)
    o_ref[...] = _dot(sc, w_ref[...].astype(BF16)) + b_ref[...]


def _mod_call(cc, w_ada, b_ada):
    depth, d, _ = w_ada.shape
    rows = cc.shape[0]
    return pl.pallas_call(
        _mod_kernel,
        grid=(depth, N_MOD),
        in_specs=[pl.BlockSpec((rows, d), lambda i, j: (0, 0)),
                  pl.BlockSpec((None, d, d), lambda i, j: (i, 0, j)),
                  pl.BlockSpec((None, 1, d), lambda i, j: (i, 0, j))],
        out_specs=pl.BlockSpec((None, None, rows, d), lambda i, j: (i, j, 0, 0)),
        out_shape=jax.ShapeDtypeStruct((depth, N_MOD, rows, d), F32),
        compiler_params=_cparams(2),
        name="mod",
    )(cc, w_ada, b_ada.reshape(depth, 1, N_MOD * d))


def _ffn_half(x, mod_ref, j, g, w1_ref, w2_ref, acc_ref):
    d_ff = w2_ref.shape[0]
    shift = mod_ref[3 * j:3 * j + 1, :]
    scale = mod_ref[3 * j + 1:3 * j + 2, :]
    gate = mod_ref[3 * j + 2:3 * j + 3, :]
    h = _pre(x, g, shift, scale).astype(BF16)
    for c0 in range(0, d_ff, FFN_CHUNK):
        a = _dot(h, w1_ref[:, c0:c0 + FFN_CHUNK])
        u = _dot(h, w1_ref[:, d_ff + c0:d_ff + c0 + FFN_CHUNK])
        part = _dot((_silu(a) * u).astype(BF16), w2_ref[c0:c0 + FFN_CHUNK, :])
        if c0 == 0:
            acc_ref[...] = part
        else:
            acc_ref[...] += part
    return x + (0.5 * gate) * acc_ref[...]


def _ffn_in_kernel(x_ref, mod_ref, g_ref, w1_ref, w2_ref, wc_ref, wg_ref, wa_ref,
                   xo_ref, pc_ref, pg_ref, pa_ref, acc_ref):
    x1 = _ffn_half(x_ref[...], mod_ref, 0, g_ref[0], w1_ref, w2_ref, acc_ref)
    xo_ref[...] = x1
    h = _pre(x1, g_ref[1], mod_ref[3:4, :], mod_ref[4:5, :]).astype(BF16)
    pc_ref[...] = _dot(h, wc_ref[...])
    pg_ref[...] = _dot(h, wg_ref[...])
    pa_ref[...] = _dot(h, wa_ref[...])


def _ffn_in_call(x, mod_l, mod_row, g_l, w1_all, w2_all, wc_all, wg_all, wa_all, li, tm):
    b, t, d = x.shape
    tok = lambda bi, i: (bi, i, 0)
    widths = (wc_all.shape[2], wg_all.shape[2], wa_all.shape[2])
    return pl.pallas_call(
        _ffn_in_kernel,
        grid=(b, t // tm),
        in_specs=[pl.BlockSpec((None, tm, d), tok),
                  pl.BlockSpec((None, N_MOD, d), lambda bi, i: (mod_row(bi), 0, 0)),
                  _const_spec(g_l.shape), _layer_spec(w1_all, li, 0), _layer_spec(w2_all, li, 0),
                  _layer_spec(wc_all, li), _layer_spec(wg_all, li), _layer_spec(wa_all, li)],
        out_specs=[pl.BlockSpec((None, tm, d), tok)] + [pl.BlockSpec((None, tm, w), tok) for w in widths],
        out_shape=[jax.ShapeDtypeStruct(x.shape, F32)]
        + [jax.ShapeDtypeStruct((b, t, w), F32) for w in widths],
        scratch_shapes=[pltpu.VMEM((tm, d), F32)],
        compiler_params=_cparams(2),
        name="ffn_in",
    )(x, mod_l, g_l, w1_all, w2_all, wc_all, wg_all, wa_all)


def _conv_kernel(t, cw, pc_ref, wdw_ref, bdw_ref, cg_ref, cb_ref, gm_ref, o_ref, hbuf, h2_scr):
    r = CONV_ROWS
    halo = CONV_HALO
    hbuf[0:halo, :] = jnp.zeros((halo, cw), F32)
    hbuf[t + halo:t + 2 * halo, :] = jnp.zeros((halo, cw), F32)

    def glu(i, carry):
        r0 = pl.multiple_of(i * r, r)
        a = pc_ref[pl.ds(r0, r), 0:cw]
        gt = pc_ref[pl.ds(r0, r), cw:2 * cw]
        hbuf[pl.ds(r0 + halo, r), :] = a * _sigmoid(gt)
        return carry

    lax.fori_loop(0, t // r, glu, 0)
    gm = gm_ref[...]

    def tile(i, carry):
        r0 = pl.multiple_of(i * r, r)
        win = hbuf[pl.ds(r0, r + 2 * halo), :]
        acc = jnp.zeros((r, cw), F32)
        for res in range(SUBLANES):
            wr = win if res == 0 else pltpu.roll(win, r + 2 * halo - res, axis=0)
            for a in range(2 * halo // SUBLANES):
                s = SUBLANES * a + res
                if 1 <= s <= CONV_K:
                    acc = acc + wr[SUBLANES * a:SUBLANES * a + r, :] * wdw_ref[s - 1:s, :]
        h2_scr[pl.ds(r0, r), :] = acc + bdw_ref[...]
        return carry

    lax.fori_loop(0, t // r, tile, 0, unroll=2)

    nb = min(CONV_NORM_ROWS, t)

    def norm(i, carry):
        rows = pl.ds(pl.multiple_of(i * nb, nb), nb)
        h2 = h2_scr[rows, :]
        mu = _split_dot(h2, gm)
        dlt = h2 - mu
        var = _split_dot(dlt * dlt, gm)
        y = dlt * lax.rsqrt(var + EPS) * cg_ref[...] + cb_ref[...]
        o_ref[rows, :] = _silu(y).astype(o_ref.dtype)
        return carry

    lax.fori_loop(0, t // nb, norm, 0)


def _conv_call(pc, w_dw, b_dw, cn_g, cn_b, gm):
    b, t, two_cw = pc.shape
    cw = two_cw // 2
    return pl.pallas_call(
        functools.partial(_conv_kernel, t, cw),
        grid=(b,),
        in_specs=[pl.BlockSpec((None, t, two_cw), lambda bi: (bi, 0, 0)),
                  _const_spec(w_dw.shape), _const_spec(b_dw.shape), _const_spec(cn_g.shape),
                  _const_spec(cn_b.shape), _const_spec(gm.shape)],
        out_specs=pl.BlockSpec((None, t, cw), lambda bi: (bi, 0, 0)),
        out_shape=jax.ShapeDtypeStruct((b, t, cw), BF16),
        scratch_shapes=[pltpu.VMEM((t + 2 * CONV_HALO, cw), F32), pltpu.VMEM((t, cw), F32)],
        compiler_params=_cparams(1),
        name="conv",
    )(pc, w_dw, b_dw, cn_g, cn_b, gm)


def _gla_blocks(n_ch, dkw, dvw, pgs, wg_ref, bg_ref, outs, s_scr):
    c = GLA_CHUNK
    dk = dkw // GLA_HEADS
    dv = dvw // GLA_HEADS
    lr_off = 2 * dkw + 2 * dvw
    ri = lax.broadcasted_iota(jnp.int32, (c, c), 0)
    ci = lax.broadcasted_iota(jnp.int32, (c, c), 1)
    rih = lax.broadcasted_iota(jnp.int32, (c, GLA_HEADS * c), 0)
    cih = lax.broadcasted_iota(jnp.int32, (c, GLA_HEADS * c), 1) % c
    tri_m = [jnp.where(ci <= ri, 1.0, 0.0).astype(BF16), jnp.where(ci >= ri, 1.0, 0.0).astype(BF16)]
    tri_h = [cih <= rih, cih >= rih]
    klane = lax.broadcasted_iota(jnp.int32, (c, dkw), 1) // dk
    vlane = lax.broadcasted_iota(jnp.int32, (c, dvw), 1) // dv
    srow = lax.broadcasted_iota(jnp.int32, (dvw, dkw), 0) // dv
    scol = lax.broadcasted_iota(jnp.int32, (dvw, dkw), 1) // dk
    smask = srow == scol

    units = [(d, j) for d in range(2) for j in range(n_ch)]
    rows = [slice(j * c, (j + 1) * c) for j in range(n_ch)]
    la = []
    for d in range(2):
        z = _dot(pgs[d][:, lr_off:lr_off + LANES], wg_ref[d]) + bg_ref[d]
        la.append(-(jnp.maximum(-z, 0.0) + jnp.log(1.0 + jnp.exp(-jnp.abs(z)))) * (1.0 / GLA_GATE_NORM))
    bcum = {u: _split3_dot_left(tri_m[u[0]], la[u[0]][rows[u[1]], :]) for u in units}
    q_e, k_end, kblk, vblk, vb, decay = {}, {}, {}, {}, {}, {}
    for u in units:
        d, j = u
        q = pgs[d][rows[j], 0:dkw] * (dk ** -0.5)
        k = pgs[d][rows[j], dkw:2 * dkw]
        v = pgs[d][rows[j], 2 * dkw:2 * dkw + dvw]
        btot = bcum[u][c - 1:c, :] if d == 0 else bcum[u][0:1, :]
        q_e[u] = (q * jnp.exp(bcum[u])).astype(BF16)
        k_e = k * jnp.exp(-bcum[u])
        k_end[u] = (k * jnp.exp(btot - bcum[u])).astype(BF16)
        kblk[u] = jnp.concatenate([jnp.where(klane == h, k_e, 0.0) for h in range(GLA_HEADS)],
                                  axis=0).astype(BF16)
        vblk[u] = jnp.concatenate([jnp.where(vlane == h, v, 0.0) for h in range(GLA_HEADS)],
                                  axis=0).astype(BF16)
        vb[u] = v.astype(BF16)
        decay[u] = jnp.exp(btot)
    att = {u: jnp.where(tri_h[u[0]], _dot_nt(q_e[u], kblk[u]), 0.0).astype(BF16) for u in units}
    o_intra = {u: _dot(att[u], vblk[u]) for u in units}
    upd = {u: jnp.where(smask, _dot_tn(vb[u], k_end[u]), 0.0) for u in units}
    s_start = {}
    for d in range(2):
        s_t = s_scr[d]
        for j in (range(n_ch) if d == 0 else reversed(range(n_ch))):
            s_start[(d, j)] = s_t.astype(BF16)
            s_t = s_t * decay[(d, j)] + upd[(d, j)]
        s_scr[d] = s_t
    for u in units:
        outs[u[0]][rows[u[1]], :] = (o_intra[u] + _dot_nt(q_e[u], s_start[u])).astype(outs[u[0]].dtype)


def _gla_kernel(n_ch, dkw, dvw, pgf_ref, pgb_ref, wg_ref, bg_ref, s0_ref, of_ref, ob_ref, sfin_ref, s_scr):
    i = pl.program_id(1)

    @pl.when(i == 0)
    def _():
        s_scr[...] = s0_ref[...]

    _gla_blocks(n_ch, dkw, dvw, (pgf_ref, pgb_ref), wg_ref, bg_ref, (of_ref, ob_ref), s_scr)

    @pl.when(i == pl.num_programs(1) - 1)
    def _():
        sfin_ref[...] = s_scr[...]


def _gla_call(pg, wg2, bg2, s0, dkw, dvw):
    b, t, w = pg.shape
    blk = min(GLA_BLOCK, t)
    nb = t // blk
    n_ch = blk // GLA_CHUNK
    return pl.pallas_call(
        functools.partial(_gla_kernel, n_ch, dkw, dvw),
        grid=(b, nb),
        in_specs=[pl.BlockSpec((None, blk, w), lambda bi, i: (bi, i, 0)),
                  pl.BlockSpec((None, blk, w), lambda bi, i: (bi, nb - 1 - i, 0)),
                  _const_spec(wg2.shape), _const_spec(bg2.shape),
                  pl.BlockSpec((None, 2, dvw, dkw), lambda bi, i: (bi, 0, 0, 0))],
        out_specs=[pl.BlockSpec((None, blk, dvw), lambda bi, i: (bi, i, 0)),
                   pl.BlockSpec((None, blk, dvw), lambda bi, i: (bi, nb - 1 - i, 0)),
                   pl.BlockSpec((None, 2, dvw, dkw), lambda bi, i: (bi, 0, 0, 0))],
        out_shape=[jax.ShapeDtypeStruct((b, t, dvw), BF16), jax.ShapeDtypeStruct((b, t, dvw), BF16),
                   jax.ShapeDtypeStruct((b, 2, dvw, dkw), F32)],
        scratch_shapes=[pltpu.VMEM((2, dvw, dkw), F32)],
        compiler_params=_cparams(2),
        name="gla",
    )(pg, pg, wg2, bg2, s0)


def _kvprep_kernel(aw, pa_ref, cos_ref, sin_ref, qg_ref, kg_ref, gm_ref, q_ref, k_ref, vt_ref):
    hd = HEAD_DIM
    cos = cos_ref[...]
    sin = sin_ref[...]

    def norm_rope(x, g):
        ms = _dot((x * x).astype(BF16), gm_ref[...])
        xn = x * lax.rsqrt(ms + EPS) * g
        return xn * cos + _rope_partner(xn) * sin

    for t0 in range(0, aw, LANES):
        q_ref[:, t0:t0 + LANES] = (norm_rope(pa_ref[:, t0:t0 + LANES], qg_ref[...])
                                   * (hd ** -0.5 * LOG2E)).astype(BF16)
    vv = pa_ref[:, aw + LANES:aw + 2 * LANES]
    kr = norm_rope(pa_ref[:, aw:aw + LANES], kg_ref[...])
    lane = lax.broadcasted_iota(jnp.int32, kr.shape, 1)
    lo = lane < hd
    kswap = pltpu.roll(kr, hd, axis=1)
    k0 = jnp.where(lo, kr, kswap).astype(BF16)
    k1 = jnp.where(lo, kswap, kr).astype(BF16)
    k_ref[...] = jnp.concatenate([k0, k0, k1, k1], axis=1)
    vt_ref[0] = jnp.where(lo, vv, 1.0).T[0:VT_ROWS, :].astype(BF16)
    vt_ref[1] = jnp.where(lo, pltpu.roll(vv, hd, axis=1), 1.0).T[0:VT_ROWS, :].astype(BF16)


def _kvprep_call(pa, cos_t, sin_t, qg, kg, gm, tt):
    b, t, w = pa.shape
    aw = w - 2 * LANES
    return pl.pallas_call(
        functools.partial(_kvprep_kernel, aw),
        grid=(b, t // tt),
        in_specs=[pl.BlockSpec((None, tt, w), lambda bi, i: (bi, i, 0)),
                  pl.BlockSpec((tt, LANES), lambda bi, i: (i, 0)),
                  pl.BlockSpec((tt, LANES), lambda bi, i: (i, 0)),
                  _const_spec(qg.shape), _const_spec(kg.shape), _const_spec(gm.shape)],
        out_specs=[pl.BlockSpec((None, tt, aw), lambda bi, i: (bi, i, 0)),
                   pl.BlockSpec((None, tt, 4 * LANES), lambda bi, i: (bi, i, 0)),
                   pl.BlockSpec((None, N_KV_HEADS, VT_ROWS, tt), lambda bi, i: (bi, 0, 0, i))],
        out_shape=[jax.ShapeDtypeStruct((b, t, aw), BF16),
                   jax.ShapeDtypeStruct((b, t, 4 * LANES), BF16),
                   jax.ShapeDtypeStruct((b, N_KV_HEADS, VT_ROWS, t), BF16)],
        compiler_params=_cparams(2),
        name="kvprep",
    )(pa, cos_t, sin_t, qg, kg, gm)


def _flash_kernel(seg_lens, q_ref, *rest):
    n_seg = len(seg_lens)
    kv_refs = rest[:2 * n_seg]
    o_ref, qs_scr = rest[2 * n_seg:]
    hd = HEAD_DIM
    g_heads = 2 * LANES // hd
    qr = q_ref[...]
    lane_head = lax.broadcasted_iota(jnp.int32, qr.shape, 1) // hd
    for h in range(g_heads):
        qs_scr[h] = jnp.where(lane_head == h, qr, jnp.zeros_like(qr))

    tasks = []
    for h in range(g_heads):
        first = True
        for si in range(n_seg):
            k_ref, vt_ref = kv_refs[2 * si], kv_refs[2 * si + 1]
            tk = min(ATT_TK, seg_lens[si])
            for c0 in range(0, seg_lens[si], tk):
                tasks.append((h, k_ref, vt_ref, c0, tk, first))
                first = False

    def scores(task):
        h, k_ref, _, c0, tk, _ = task
        return _dot_nt(k_ref[c0:c0 + tk, :], qs_scr[h])

    def attend(stale_max):
        m_run = [None] * g_heads
        acc = [None] * g_heads
        rise = None
        pending = [scores(t) for t in tasks[:ATT_LOOKAHEAD]]
        for n, (h, _, vt_ref, c0, tk, first) in enumerate(tasks):
            if n + ATT_LOOKAHEAD < len(tasks):
                pending.append(scores(tasks[n + ATT_LOOKAHEAD]))
            s = pending.pop(0)
            m_cur = jnp.max(s, axis=0, keepdims=True)
            if first:
                m_new = m_cur
                acc[h] = _dot(vt_ref[:, c0:c0 + tk], jnp.exp2(s - m_new).astype(BF16))
            elif stale_max:
                m_old = m_run[h]
                m_new = jnp.maximum(m_old, m_cur)
                pv = _dot(vt_ref[:, c0:c0 + tk], jnp.exp2(s - m_old).astype(BF16))
                acc[h] = (acc[h] + pv) * jnp.exp2(m_old - m_new)
                rise = m_cur - m_old if rise is None else jnp.maximum(rise, m_cur - m_old)
            else:
                m_new = jnp.maximum(m_run[h], m_cur)
                pv = _dot(vt_ref[:, c0:c0 + tk], jnp.exp2(s - m_new).astype(BF16))
                acc[h] = acc[h] * jnp.exp2(m_run[h] - m_new) + pv
            m_run[h] = m_new
        o_t = jnp.concatenate([acc[h][0:hd, :] * (1.0 / acc[h][hd:hd + 1, :]) for h in range(g_heads)],
                              axis=0)
        return o_t.T.astype(o_ref.dtype), rise

    o_fast, rise = attend(True)
    o_ref[...] = o_fast
    if rise is not None:
        @pl.when(jnp.max(rise) > ATT_MAX_RISE)
        def _():
            o_ref[...] = attend(False)[0]


def _flash_call(qr, segs):
    b, t, _ = qr.shape
    tq = min(ATT_TQ, t)
    seg_lens = tuple(k.shape[1] for k, _ in segs)
    in_specs = [pl.BlockSpec((None, tq, 2 * LANES), lambda bi, j, i: (bi, i, j))]
    args = [qr]
    for k, v in segs:
        in_specs.append(pl.BlockSpec((None, k.shape[1], 2 * LANES), lambda bi, j, i: (bi, 0, j)))
        in_specs.append(pl.BlockSpec((None, None, VT_ROWS, v.shape[3]), lambda bi, j, i: (bi, j, 0, 0)))
        args += [k, v]
    g_heads = 2 * LANES // HEAD_DIM
    return pl.pallas_call(
        functools.partial(_flash_kernel, seg_lens),
        grid=(b, N_KV_HEADS, t // tq),
        in_specs=in_specs,
        out_specs=pl.BlockSpec((None, tq, 2 * LANES), lambda bi, j, i: (bi, i, j)),
        out_shape=jax.ShapeDtypeStruct((b, t, N_KV_HEADS * 2 * LANES), BF16),
        scratch_shapes=[pltpu.VMEM((g_heads, tq, 2 * LANES), BF16)],
        compiler_params=_cparams(3),
        name="flash",
    )(*args)


def _out_kernel(cw, dvw, x_ref, mod_ref, conv_ref, of_ref, ob_ref, r_ref, att_ref, gg_ref, gm_ref,
                wo_ref, o_ref):
    o = of_ref[...].astype(F32) + ob_ref[...].astype(F32)
    ms = _split_dot(o * o, gm_ref[...])
    gl = (o * lax.rsqrt(ms + EPS) * gg_ref[...]) * _silu(r_ref[...])
    y = (_dot(conv_ref[...], wo_ref[0:cw, :])
         + _dot(gl.astype(BF16), wo_ref[cw:cw + dvw, :])
         + _dot(att_ref[...], wo_ref[cw + dvw:, :]))
    o_ref[...] = x_ref[...] + mod_ref[5:6, :] * y


def _out_call(x, mod_l, mod_row, conv, o_f, o_b, pg, att, gg, gm, wo_all, li, tm):
    b, t, d = x.shape
    cw = conv.shape[2]
    dvw = o_f.shape[2]
    aw = att.shape[2]
    tok = lambda bi, i: (bi, i, 0)
    return pl.pallas_call(
        functools.partial(_out_kernel, cw, dvw),
        grid=(b, t // tm),
        in_specs=[pl.BlockSpec((None, tm, d), tok),
                  pl.BlockSpec((None, N_MOD, d), lambda bi, i: (mod_row(bi), 0, 0)),
                  pl.BlockSpec((None, tm, cw), tok),
                  pl.BlockSpec((None, tm, dvw), tok),
                  pl.BlockSpec((None, tm, dvw), tok),
                  pl.BlockSpec((None, tm, dvw), lambda bi, i: (bi, i, 2)),
                  pl.BlockSpec((None, tm, aw), tok),
                  _const_spec(gg.shape), _const_spec(gm.shape), _layer_spec(wo_all, li)],
        out_specs=pl.BlockSpec((None, tm, d), tok),
        out_shape=jax.ShapeDtypeStruct(x.shape, F32),
        compiler_params=_cparams(2),
        name="outproj",
    )(x, mod_l, conv, o_f, o_b, pg, att, gg, gm, wo_all)


def _ffn_kernel(j, x_ref, mod_ref, g_ref, w1_ref, w2_ref, o_ref, acc_ref):
    o_ref[...] = _ffn_half(x_ref[...], mod_ref, j, g_ref[j], w1_ref, w2_ref, acc_ref)


def _ffn_call(x, mod_l, mod_row, g_l, w1_all, w2_all, li, which, j, tm):
    b, t, d = x.shape
    tok = lambda bi, i: (bi, i, 0)
    return pl.pallas_call(
        functools.partial(_ffn_kernel, j),
        grid=(b, t // tm),
        in_specs=[pl.BlockSpec((None, tm, d), tok),
                  pl.BlockSpec((None, N_MOD, d), lambda bi, i: (mod_row(bi), 0, 0)),
                  _const_spec(g_l.shape), _layer_spec(w1_all, li, which), _layer_spec(w2_all, li, which)],
        out_specs=pl.BlockSpec((None, tm, d), tok),
        out_shape=jax.ShapeDtypeStruct(x.shape, F32),
        scratch_shapes=[pltpu.VMEM((tm, d), F32)],
        compiler_params=_cparams(2),
        name="ffn",
    )(x, mod_l, g_l, w1_all, w2_all)


def _rope_tables(s):
    nf = HEAD_DIM // 4
    rows_n = s // GRID_W
    row = jnp.repeat(jnp.arange(rows_n), GRID_W).astype(F32)
    col = jnp.tile(jnp.arange(GRID_W), rows_n).astype(F32)
    freqs = ROPE_BASE ** (-jnp.arange(nf, dtype=F32) / nf)
    ang_r = row[:, None] * freqs
    ang_c = col[:, None] * freqs
    cos = jnp.concatenate([jnp.cos(ang_r)] * 2 + [jnp.cos(ang_c)] * 2, axis=1)
    sin = jnp.concatenate([-jnp.sin(ang_r), jnp.sin(ang_r), -jnp.sin(ang_c), jnp.sin(ang_c)], axis=1)
    return cos, sin


def kernel(x, c, ctx, c_ctx, w_ada, b_ada, g_norm, w_ffn_in, w_ffn_out, w_in, w_out, w_dw, b_dw,
           conv_norm_g, conv_norm_b, w_gla_gate, b_gla_gate, gla_norm_g, q_norm_g, k_norm_g):
    b, s, d = x.shape
    l = ctx.shape[1]
    depth = w_ada.shape[0]
    d_ff = w_ffn_out.shape[2]
    cw = w_dw.shape[2]
    dkw = w_gla_gate.shape[3]
    dvw = gla_norm_g.shape[1] * gla_norm_g.shape[2]
    n_kv = N_KV_HEADS * HEAD_DIM
    aw = d - cw - dvw
    tm_c = min(TOKEN_TILE, b * l)
    assert d_ff % FFN_CHUNK == 0 and s % FFN2_TILE == 0 and l % GLA_CHUNK == 0 and (b * l) % tm_c == 0
    assert dkw == LANES and n_kv == LANES and cw == 2 * LANES and dvw == 2 * LANES and aw == 4 * LANES

    rows = -(-(b + 1) // SUBLANES) * SUBLANES
    cc = jnp.zeros((rows, d), F32).at[:b].set(c).at[b].set(c_ctx)
    mod = jnp.transpose(_mod_call(cc, w_ada, b_ada), (0, 2, 1, 3))
    row_x = lambda bi: bi
    row_c = lambda bi: b

    cos64, sin64 = _rope_tables(s)
    cos_k, sin_k = jnp.tile(cos64, (1, 2)), jnp.tile(sin64, (1, 2))
    one_k, zero_k = jnp.ones((l, LANES), F32), jnp.zeros((l, LANES), F32)

    gm64_128 = _group_mean_matrix(LANES, HEAD_DIM)
    gm_conv = _group_mean_matrix(cw, cw // CONV_GROUPS)
    gm_gla = _group_mean_matrix(dvw, dvw // GLA_HEADS)

    sizes = (cw, cw, dkw, dkw, dvw, dvw, GLA_RANK, GLA_RANK, aw, n_kv, n_kv)
    offs = np.concatenate([[0], np.cumsum(sizes)])

    w1_all = w_ffn_in.astype(BF16)
    w2_all = w_ffn_out.astype(BF16)
    wo_all = w_out.astype(BF16)
    wi = w_in.astype(BF16)
    wc_all = wi[:, :, offs[0]:offs[2]]
    wg_all = jnp.concatenate([wi[:, :, offs[2]:offs[8]],
                              jnp.zeros((depth, d, LANES - 2 * GLA_RANK), BF16)], axis=2)
    wa_all = wi[:, :, offs[8]:offs[11]]
    wg2_all = jnp.zeros((depth, 2, LANES, dkw), F32)
    wg2_all = wg2_all.at[:, 0, 0:GLA_RANK].set(w_gla_gate[:, 0])
    wg2_all = wg2_all.at[:, 1, GLA_RANK:2 * GLA_RANK].set(w_gla_gate[:, 1])
    weights = (w1_all, w2_all, wc_all, wg_all, wa_all)
    flat = lambda a: a.reshape(1, b * l, a.shape[2])

    for i in range(depth):
        last = i == depth - 1
        mod_l = mod[i]
        g_l = g_norm[i].reshape(3, 1, d)
        wg2 = wg2_all[i]
        bg2 = b_gla_gate[i].reshape(2, 1, dkw)
        qg = jnp.tile(q_norm_g[i], 2).reshape(1, LANES)
        kg = jnp.tile(k_norm_g[i], 2).reshape(1, LANES)
        gg = gla_norm_g[i].reshape(1, dvw)
        wdw, bdw = w_dw[i], b_dw[i].reshape(1, cw)
        cng, cnb = conv_norm_g[i].reshape(1, cw), conv_norm_b[i].reshape(1, cw)

        x, pc_x, pg_x, pa_x = _ffn_in_call(x, mod_l, row_x, g_l, *weights, i, TOKEN_TILE)
        ctx_f, pc_c, pg_c, pa_c = _ffn_in_call(flat(ctx), mod_l, row_c, g_l, *weights, i, tm_c)
        ctx = ctx_f.reshape(b, l, d)
        pc_c, pg_c, pa_c = (p.reshape(b, l, p.shape[2]) for p in (pc_c, pg_c, pa_c))
        conv_x = _conv_call(pc_x, wdw, bdw, cng, cnb, gm_conv)
        of_c, ob_c, s_c = _gla_call(pg_c, wg2, bg2, jnp.zeros((b, 2, dvw, dkw), F32), dkw, dvw)
        of_x, ob_x, _ = _gla_call(pg_x, wg2, bg2, s_c, dkw, dvw)
        q_c, k_c, v_c = _kvprep_call(pa_c, one_k, zero_k, qg, kg, gm64_128, min(TOKEN_TILE, l))
        q_x, k_x, v_x = _kvprep_call(pa_x, cos_k, sin_k, qg, kg, gm64_128, TOKEN_TILE)
        att_x = _flash_call(q_x, [(k_c, v_c), (k_x, v_x)])
        x = _out_call(x, mod_l, row_x, conv_x, of_x, ob_x, pg_x, att_x, gg, gm_gla, wo_all, i, TOKEN_TILE)
        x = _ffn_call(x, mod_l, row_x, g_l, w1_all, w2_all, i, 1, 2, FFN2_TILE)
        if not last:
            conv_c = _conv_call(pc_c, wdw, bdw, cng, cnb, gm_conv)
            att_c = _flash_call(q_c, [(k_c, v_c)])
            ctx_f = _out_call(flat(ctx), mod_l, row_c, flat(conv_c), flat(of_c), flat(ob_c),
                              flat(pg_c), flat(att_c), gg, gm_gla, wo_all, i, tm_c)
            ctx = _ffn_call(ctx_f, mod_l, row_c, g_l, w1_all, w2_all, i, 1, 2, tm_c).reshape(b, l, d)
    return x
```

```python
import functools

import numpy as np
import jax
import jax.numpy as jnp
from jax import lax
from jax.experimental import pallas as pl
from jax.experimental.pallas import tpu as pltpu

F32 = jnp.float32
BF16 = jnp.bfloat16

CONV_GROUPS = 4
CONV_K = 31
GLA_HEADS = 4
GLA_RANK = 16
GLA_GATE_NORM = 16.0
GLA_CHUNK = 64
HEAD_DIM = 64
N_KV_HEADS = 2
GRID_W = 64
ROPE_BASE = 10000.0
N_MOD = 9
EPS = 1e-6
LOG2E = 1.4426950408889634

LANES = 128
SUBLANES = 8
VMEM_LIMIT_BYTES = 56 * 1024 * 1024

TOKEN_TILE = 512
FFN2_TILE = 1024
FFN_CHUNK = 256
CONV_ROWS = 32
CONV_HALO = 16
CONV_NORM_ROWS = 1024
GLA_BLOCK = 512
VT_ROWS = 80
ATT_TQ = 256
ATT_TILING = ((512, 2), (512, 3), (1024, 2), (256, 3))
ATT_MAX_RISE = 60.0


def _cparams(n_axes):
    return pltpu.CompilerParams(dimension_semantics=("arbitrary",) * n_axes,
                                vmem_limit_bytes=VMEM_LIMIT_BYTES)


def _const_spec(shape):
    nd = len(shape)
    return pl.BlockSpec(shape, lambda *_: (0,) * nd, pipeline_mode=pl.Buffered(1))


def _layer_spec(arr, *idx):
    lead = len(idx)
    blk = (None,) * lead + tuple(arr.shape[lead:])
    tail = (0,) * (arr.ndim - lead)
    return pl.BlockSpec(blk, lambda *_: tuple(idx) + tail, pipeline_mode=pl.Buffered(1))


def _dot(a, b):
    return jnp.dot(a, b, preferred_element_type=F32)


def _dot_nt(a, b):
    return lax.dot_general(a, b, (((1,), (1,)), ((), ())), preferred_element_type=F32)


def _dot_tn(a, b):
    return lax.dot_general(a, b, (((0,), (0,)), ((), ())), preferred_element_type=F32)


def _sigmoid(x):
    return 1.0 / (1.0 + jnp.exp(-x))


def _silu(x):
    return x * _sigmoid(x)


def _split_dot(v, m):
    hi = v.astype(BF16)
    lo = (v - hi.astype(F32)).astype(BF16)
    return _dot(hi, m) + _dot(lo, m)


def _split3_dot_left(m, v):
    hi = v.astype(BF16)
    r1 = v - hi.astype(F32)
    mid = r1.astype(BF16)
    lo = (r1 - mid.astype(F32)).astype(BF16)
    return _dot(m, hi) + _dot(m, mid) + _dot(m, lo)


def _pre(x, g, shift, scale):
    r = lax.rsqrt(jnp.mean(x * x, axis=-1, keepdims=True) + EPS)
    return (x * r * g) * (1.0 + scale) + shift


def _group_mean_matrix(width, group):
    idx = np.arange(width) // group
    return jnp.asarray((idx[:, None] == idx[None, :]).astype(np.float32) / group, dtype=BF16)


def _rope_partner(x):
    n = x.shape[-1]
    lane = lax.broadcasted_iota(jnp.int32, x.shape, x.ndim - 1)
    up = pltpu.roll(x, n - 16, axis=x.ndim - 1)
    dn = pltpu.roll(x, 16, axis=x.ndim - 1)
    return jnp.where((lane % 32) < 16, up, dn)


def _mod_kernel(c_ref, w_ref, b_ref, o_ref):
    sc = _silu(c_ref[...]).astype(BF16)
    o_ref[...] = _dot(sc, w_ref[...].astype(BF16)) + b_ref[...]


def _mod_call(cc, w_ada, b_ada):
    depth, d, _ = w_ada.shape
    rows = cc.shape[0]
    return pl.pallas_call(
        _mod_kernel,
        grid=(depth, N_MOD),
        in_specs=[pl.BlockSpec((rows, d), lambda i, j: (0, 0)),
                  pl.BlockSpec((None, d, d), lambda i, j: (i, 0, j)),
                  pl.BlockSpec((None, 1, d), lambda i, j: (i, 0, j))],
        out_specs=pl.BlockSpec((None, None, rows, d), lambda i, j: (i, j, 0, 0)),
        out_shape=jax.ShapeDtypeStruct((depth, N_MOD, rows, d), F32),
        compiler_params=_cparams(2),
        name="mod",
    )(cc, w_ada, b_ada.reshape(depth, 1, N_MOD * d))


def _ffn_half(x, mod_ref, j, g, w1_ref, w2_ref, acc_ref):
    d_ff = w2_ref.shape[0]
    shift = mod_ref[3 * j:3 * j + 1, :]
    scale = mod_ref[3 * j + 1:3 * j + 2, :]
    gate = mod_ref[3 * j + 2:3 * j + 3, :]
    h = _pre(x, g, shift, scale).astype(BF16)
    for c0 in range(0, d_ff, FFN_CHUNK):
        a = _dot(h, w1_ref[:, c0:c0 + FFN_CHUNK])
        u = _dot(h, w1_ref[:, d_ff + c0:d_ff + c0 + FFN_CHUNK])
        part = _dot((_silu(a) * u).astype(BF16), w2_ref[c0:c0 + FFN_CHUNK, :])
        if c0 == 0:
            acc_ref[...] = part
        else:
            acc_ref[...] += part
    return x + (0.5 * gate) * acc_ref[...]


def _ffn_in_kernel(x_ref, mod_ref, g_ref, w1_ref, w2_ref, wc_ref, wg_ref, wa_ref,
                   xo_ref, pc_ref, pg_ref, pa_ref, acc_ref):
    x1 = _ffn_half(x_ref[...], mod_ref, 0, g_ref[0], w1_ref, w2_ref, acc_ref)
    xo_ref[...] = x1
    h = _pre(x1, g_ref[1], mod_ref[3:4, :], mod_ref[4:5, :]).astype(BF16)
    pc_ref[...] = _dot(h, wc_ref[...])
    pg_ref[...] = _dot(h, wg_ref[...])
    pa_ref[...] = _dot(h, wa_ref[...])


def _ffn_in_call(x, mod_l, mod_row, g_l, w1_all, w2_all, wc_all, wg_all, wa_all, li, tm):
    b, t, d = x.shape
    tok = lambda bi, i: (bi, i, 0)
    widths = (wc_all.shape[2], wg_all.shape[2], wa_all.shape[2])
    return pl.pallas_call(
        _ffn_in_kernel,
        grid=(b, t // tm),
        in_specs=[pl.BlockSpec((None, tm, d), tok),
                  pl.BlockSpec((None, N_MOD, d), lambda bi, i: (mod_row(bi), 0, 0)),
                  _const_spec(g_l.shape), _layer_spec(w1_all, li, 0), _layer_spec(w2_all, li, 0),
                  _layer_spec(wc_all, li), _layer_spec(wg_all, li), _layer_spec(wa_all, li)],
        out_specs=[pl.BlockSpec((None, tm, d), tok)] + [pl.BlockSpec((None, tm, w), tok) for w in widths],
        out_shape=[jax.ShapeDtypeStruct(x.shape, F32)]
        + [jax.ShapeDtypeStruct((b, t, w), F32) for w in widths],
        scratch_shapes=[pltpu.VMEM((tm, d), F32)],
        compiler_params=_cparams(2),
        name="ffn_in",
    )(x, mod_l, g_l, w1_all, w2_all, wc_all, wg_all, wa_all)


def _conv_kernel(t, cw, pc_ref, wdw_ref, bdw_ref, cg_ref, cb_ref, gm_ref, o_ref, hbuf, h2_scr):
    r = CONV_ROWS
    halo = CONV_HALO
    hbuf[0:halo, :] = jnp.zeros((halo, cw), F32)
    hbuf[t + halo:t + 2 * halo, :] = jnp.zeros((halo, cw), F32)

    def glu(i, carry):
        r0 = pl.multiple_of(i * r, r)
        a = pc_ref[pl.ds(r0, r), 0:cw]
        gt = pc_ref[pl.ds(r0, r), cw:2 * cw]
        hbuf[pl.ds(r0 + halo, r), :] = a * _sigmoid(gt)
        return carry

    lax.fori_loop(0, t // r, glu, 0)
    gm = gm_ref[...]

    def tile(i, carry):
        r0 = pl.multiple_of(i * r, r)
        win = hbuf[pl.ds(r0, r + 2 * halo), :]
        acc = jnp.zeros((r, cw), F32)
        for res in range(SUBLANES):
            wr = win if res == 0 else pltpu.roll(win, r + 2 * halo - res, axis=0)
            for a in range(2 * halo // SUBLANES):
                s = SUBLANES * a + res
                if 1 <= s <= CONV_K:
                    acc = acc + wr[SUBLANES * a:SUBLANES * a + r, :] * wdw_ref[s - 1:s, :]
        h2_scr[pl.ds(r0, r), :] = acc + bdw_ref[...]
        return carry

    lax.fori_loop(0, t // r, tile, 0, unroll=2)

    nb = min(CONV_NORM_ROWS, t)

    def norm(i, carry):
        rows = pl.ds(pl.multiple_of(i * nb, nb), nb)
        h2 = h2_scr[rows, :]
        mu = _split_dot(h2, gm)
        dlt = h2 - mu
        var = _split_dot(dlt * dlt, gm)
        y = dlt * lax.rsqrt(var + EPS) * cg_ref[...] + cb_ref[...]
        o_ref[rows, :] = _silu(y).astype(o_ref.dtype)
        return carry

    lax.fori_loop(0, t // nb, norm, 0)


def _conv_call(pc, w_dw, b_dw, cn_g, cn_b, gm):
    b, t, two_cw = pc.shape
    cw = two_cw // 2
    return pl.pallas_call(
        functools.partial(_conv_kernel, t, cw),
        grid=(b,),
        in_specs=[pl.BlockSpec((None, t, two_cw), lambda bi: (bi, 0, 0)),
                  _const_spec(w_dw.shape), _const_spec(b_dw.shape), _const_spec(cn_g.shape),
                  _const_spec(cn_b.shape), _const_spec(gm.shape)],
        out_specs=pl.BlockSpec((None, t, cw), lambda bi: (bi, 0, 0)),
        out_shape=jax.ShapeDtypeStruct((b, t, cw), BF16),
        scratch_shapes=[pltpu.VMEM((t + 2 * CONV_HALO, cw), F32), pltpu.VMEM((t, cw), F32)],
        compiler_params=_cparams(1),
        name="conv",
    )(pc, w_dw, b_dw, cn_g, cn_b, gm)


def _gla_blocks(n_ch, dkw, dvw, pgs, wg_ref, bg_ref, outs, s_scr):
    c = GLA_CHUNK
    dk = dkw // GLA_HEADS
    dv = dvw // GLA_HEADS
    lr_off = 2 * dkw + 2 * dvw
    ri = lax.broadcasted_iota(jnp.int32, (c, c), 0)
    ci = lax.broadcasted_iota(jnp.int32, (c, c), 1)
    rih = lax.broadcasted_iota(jnp.int32, (c, GLA_HEADS * c), 0)
    cih = lax.broadcasted_iota(jnp.int32, (c, GLA_HEADS * c), 1) % c
    tri_m = [jnp.where(ci <= ri, 1.0, 0.0).astype(BF16), jnp.where(ci >= ri, 1.0, 0.0).astype(BF16)]
    tri_h = [cih <= rih, cih >= rih]
    klane = lax.broadcasted_iota(jnp.int32, (c, dkw), 1) // dk
    vlane = lax.broadcasted_iota(jnp.int32, (c, dvw), 1) // dv
    srow = lax.broadcasted_iota(jnp.int32, (dvw, dkw), 0) // dv
    scol = lax.broadcasted_iota(jnp.int32, (dvw, dkw), 1) // dk
    smask = srow == scol

    units = [(d, j) for d in range(2) for j in range(n_ch)]
    rows = [slice(j * c, (j + 1) * c) for j in range(n_ch)]
    la = []
    for d in range(2):
        z = _dot(pgs[d][:, lr_off:lr_off + LANES], wg_ref[d]) + bg_ref[d]
        la.append(-(jnp.maximum(-z, 0.0) + jnp.log(1.0 + jnp.exp(-jnp.abs(z)))) * (1.0 / GLA_GATE_NORM))
    bcum = {u: _split3_dot_left(tri_m[u[0]], la[u[0]][rows[u[1]], :]) for u in units}
    q_e, k_end, kblk, vblk, vb, decay = {}, {}, {}, {}, {}, {}
    for u in units:
        d, j = u
        q = pgs[d][rows[j], 0:dkw] * (dk ** -0.5)
        k = pgs[d][rows[j], dkw:2 * dkw]
        v = pgs[d][rows[j], 2 * dkw:2 * dkw + dvw]
        btot = bcum[u][c - 1:c, :] if d == 0 else bcum[u][0:1, :]
        q_e[u] = (q * jnp.exp(bcum[u])).astype(BF16)
        k_e = k * jnp.exp(-bcum[u])
        k_end[u] = (k * jnp.exp(btot - bcum[u])).astype(BF16)
        kblk[u] = jnp.concatenate([jnp.where(klane == h, k_e, 0.0) for h in range(GLA_HEADS)],
                                  axis=0).astype(BF16)
        vblk[u] = jnp.concatenate([jnp.where(vlane == h, v, 0.0) for h in range(GLA_HEADS)],
                                  axis=0).astype(BF16)
        vb[u] = v.astype(BF16)
        decay[u] = jnp.exp(btot)
    att = {u: jnp.where(tri_h[u[0]], _dot_nt(q_e[u], kblk[u]), 0.0).astype(BF16) for u in units}
    o_intra = {u: _dot(att[u], vblk[u]) for u in units}
    upd = {u: jnp.where(smask, _dot_tn(vb[u], k_end[u]), 0.0) for u in units}
    s_start = {}
    for d in range(2):
        s_t = s_scr[d]
        for j in (range(n_ch) if d == 0 else reversed(range(n_ch))):
            s_start[(d, j)] = s_t.astype(BF16)
            s_t = s_t * decay[(d, j)] + upd[(d, j)]
        s_scr[d] = s_t
    for u in units:
        outs[u[0]][rows[u[1]], :] = (o_intra[u] + _dot_nt(q_e[u], s_start[u])).astype(outs[u[0]].dtype)


def _gla_kernel(n_ch, dkw, dvw, pgf_ref, pgb_ref, wg_ref, bg_ref, s0_ref, of_ref, ob_ref, sfin_ref, s_scr):
    i = pl.program_id(1)

    @pl.when(i == 0)
    def _():
        s_scr[...] = s0_ref[...]

    _gla_blocks(n_ch, dkw, dvw, (pgf_ref, pgb_ref), wg_ref, bg_ref, (of_ref, ob_ref), s_scr)

    @pl.when(i == pl.num_programs(1) - 1)
    def _():
        sfin_ref[...] = s_scr[...]


def _gla_call(pg, wg2, bg2, s0, dkw, dvw):
    b, t, w = pg.shape
    blk = min(GLA_BLOCK, t)
    nb = t // blk
    n_ch = blk // GLA_CHUNK
    return pl.pallas_call(
        functools.partial(_gla_kernel, n_ch, dkw, dvw),
        grid=(b, nb),
        in_specs=[pl.BlockSpec((None, blk, w), lambda bi, i: (bi, i, 0)),
                  pl.BlockSpec((None, blk, w), lambda bi, i: (bi, nb - 1 - i, 0)),
                  _const_spec(wg2.shape), _const_spec(bg2.shape),
                  pl.BlockSpec((None, 2, dvw, dkw), lambda bi, i: (bi, 0, 0, 0))],
        out_specs=[pl.BlockSpec((None, blk, dvw), lambda bi, i: (bi, i, 0)),
                   pl.BlockSpec((None, blk, dvw), lambda bi, i: (bi, nb - 1 - i, 0)),
                   pl.BlockSpec((None, 2, dvw, dkw), lambda bi, i: (bi, 0, 0, 0))],
        out_shape=[jax.ShapeDtypeStruct((b, t, dvw), BF16), jax.ShapeDtypeStruct((b, t, dvw), BF16),
                   jax.ShapeDtypeStruct((b, 2, dvw, dkw), F32)],
        scratch_shapes=[pltpu.VMEM((2, dvw, dkw), F32)],
        compiler_params=_cparams(2),
        name="gla",
    )(pg, pg, wg2, bg2, s0)


def _kvprep_kernel(aw, pa_ref, cos_ref, sin_ref, qg_ref, kg_ref, gm_ref, q_ref, k_ref, vt_ref):
    hd = HEAD_DIM
    cos = cos_ref[...]
    sin = sin_ref[...]

    def norm_rope(x, g):
        ms = _dot((x * x).astype(BF16), gm_ref[...])
        xn = x * lax.rsqrt(ms + EPS) * g
        return xn * cos + _rope_partner(xn) * sin

    for t0 in range(0, aw, LANES):
        q_ref[:, t0:t0 + LANES] = (norm_rope(pa_ref[:, t0:t0 + LANES], qg_ref[...])
                                   * (hd ** -0.5 * LOG2E)).astype(BF16)
    vv = pa_ref[:, aw + LANES:aw + 2 * LANES]
    kr = norm_rope(pa_ref[:, aw:aw + LANES], kg_ref[...])
    lane = lax.broadcasted_iota(jnp.int32, kr.shape, 1)
    lo = lane < hd
    kswap = pltpu.roll(kr, hd, axis=1)
    k0 = jnp.where(lo, kr, kswap).astype(BF16)
    k1 = jnp.where(lo, kswap, kr).astype(BF16)
    k_ref[...] = jnp.concatenate([k0, k0, k1, k1], axis=1)
    vt_ref[0] = jnp.where(lo, vv, 1.0).T[0:VT_ROWS, :].astype(BF16)
    vt_ref[1] = jnp.where(lo, pltpu.roll(vv, hd, axis=1), 1.0).T[0:VT_ROWS, :].astype(BF16)


def _kvprep_call(pa, cos_t, sin_t, qg, kg, gm, tt):
    b, t, w = pa.shape
    aw = w - 2 * LANES
    return pl.pallas_call(
        functools.partial(_kvprep_kernel, aw),
        grid=(b, t // tt),
        in_specs=[pl.BlockSpec((None, tt, w), lambda bi, i: (bi, i, 0)),
                  pl.BlockSpec((tt, LANES), lambda bi, i: (i, 0)),
                  pl.BlockSpec((tt, LANES), lambda bi, i: (i, 0)),
                  _const_spec(qg.shape), _const_spec(kg.shape), _const_spec(gm.shape)],
        out_specs=[pl.BlockSpec((None, tt, aw), lambda bi, i: (bi, i, 0)),
                   pl.BlockSpec((None, tt, 4 * LANES), lambda bi, i: (bi, i, 0)),
                   pl.BlockSpec((None, N_KV_HEADS, VT_ROWS, tt), lambda bi, i: (bi, 0, 0, i))],
        out_shape=[jax.ShapeDtypeStruct((b, t, aw), BF16),
                   jax.ShapeDtypeStruct((b, t, 4 * LANES), BF16),
                   jax.ShapeDtypeStruct((b, N_KV_HEADS, VT_ROWS, t), BF16)],
        compiler_params=_cparams(2),
        name="kvprep",
    )(pa, cos_t, sin_t, qg, kg, gm)


def _flash_kernel(seg_lens, tk_max, lookahead, q_ref, *rest):
    n_seg = len(seg_lens)
    kv_refs = rest[:2 * n_seg]
    o_ref, qs_scr = rest[2 * n_seg:]
    hd = HEAD_DIM
    g_heads = 2 * LANES // hd
    qr = q_ref[...]
    lane_head = lax.broadcasted_iota(jnp.int32, qr.shape, 1) // hd
    for h in range(g_heads):
        qs_scr[h] = jnp.where(lane_head == h, qr, jnp.zeros_like(qr))

    tasks = []
    for h in range(g_heads):
        first = True
        for si in range(n_seg):
            k_ref, vt_ref = kv_refs[2 * si], kv_refs[2 * si + 1]
            tk = min(tk_max, seg_lens[si])
            for c0 in range(0, seg_lens[si], tk):
                tasks.append((h, k_ref, vt_ref, c0, tk, first))
                first = False

    def scores(task):
        h, k_ref, _, c0, tk, _ = task
        return _dot_nt(k_ref[c0:c0 + tk, :], qs_scr[h])

    def attend(stale_max):
        m_made = [None] * g_heads
        base = [None] * g_heads
        acc = [None] * g_heads
        rise = [None]

        def make(task):
            h, _, _, _, _, first = task
            s = scores(task)
            m_cur = jnp.max(s, axis=0, keepdims=True)
            if first:
                ref = m_cur
            elif stale_max:
                ref = m_made[h]
                rise[0] = m_cur - ref if rise[0] is None else jnp.maximum(rise[0], m_cur - ref)
            else:
                ref = jnp.maximum(m_made[h], m_cur)
            m_made[h] = m_cur if first else jnp.maximum(m_made[h], m_cur)
            return jnp.exp2(s - ref).astype(BF16), ref

        pending = [make(t) for t in tasks[:lookahead]]
        for n, (h, _, vt_ref, c0, tk, first) in enumerate(tasks):
            if n + lookahead < len(tasks):
                pending.append(make(tasks[n + lookahead]))
            p, ref = pending.pop(0)
            pv = _dot(vt_ref[:, c0:c0 + tk], p)
            acc[h] = pv if first else acc[h] * jnp.exp2(base[h] - ref) + pv
            base[h] = ref
        o_t = jnp.concatenate([acc[h][0:hd, :] * (1.0 / acc[h][hd:hd + 1, :]) for h in range(g_heads)],
                              axis=0)
        return o_t.T.astype(o_ref.dtype), rise[0]

    o_fast, rise = attend(True)
    o_ref[...] = o_fast
    if rise is not None:
        @pl.when(jnp.max(rise) > ATT_MAX_RISE)
        def _():
            o_ref[...] = attend(False)[0]


def _flash_call(qr, segs, tk, lookahead):
    b, t, _ = qr.shape
    tq = min(ATT_TQ, t)
    seg_lens = tuple(k.shape[1] for k, _ in segs)
    in_specs = [pl.BlockSpec((None, tq, 2 * LANES), lambda bi, j, i: (bi, i, j))]
    args = [qr]
    for k, v in segs:
        in_specs.append(pl.BlockSpec((None, k.shape[1], 2 * LANES), lambda bi, j, i: (bi, 0, j)))
        in_specs.append(pl.BlockSpec((None, None, VT_ROWS, v.shape[3]), lambda bi, j, i: (bi, j, 0, 0)))
        args += [k, v]
    g_heads = 2 * LANES // HEAD_DIM
    return pl.pallas_call(
        functools.partial(_flash_kernel, seg_lens, tk, lookahead),
        grid=(b, N_KV_HEADS, t // tq),
        in_specs=in_specs,
        out_specs=pl.BlockSpec((None, tq, 2 * LANES), lambda bi, j, i: (bi, i, j)),
        out_shape=jax.ShapeDtypeStruct((b, t, N_KV_HEADS * 2 * LANES), BF16),
        scratch_shapes=[pltpu.VMEM((g_heads, tq, 2 * LANES), BF16)],
        compiler_params=_cparams(3),
        name="flash",
    )(*args)


def _out_kernel(cw, dvw, x_ref, mod_ref, conv_ref, of_ref, ob_ref, r_ref, att_ref, gg_ref, gm_ref,
                wo_ref, o_ref):
    o = of_ref[...].astype(F32) + ob_ref[...].astype(F32)
    ms = _split_dot(o * o, gm_ref[...])
    gl = (o * lax.rsqrt(ms + EPS) * gg_ref[...]) * _silu(r_ref[...])
    y = (_dot(conv_ref[...], wo_ref[0:cw, :])
         + _dot(gl.astype(BF16), wo_ref[cw:cw + dvw, :])
         + _dot(att_ref[...], wo_ref[cw + dvw:, :]))
    o_ref[...] = x_ref[...] + mod_ref[5:6, :] * y


def _out_call(x, mod_l, mod_row, conv, o_f, o_b, pg, att, gg, gm, wo_all, li, tm):
    b, t, d = x.shape
    cw = conv.shape[2]
    dvw = o_f.shape[2]
    aw = att.shape[2]
    tok = lambda bi, i: (bi, i, 0)
    return pl.pallas_call(
        functools.partial(_out_kernel, cw, dvw),
        grid=(b, t // tm),
        in_specs=[pl.BlockSpec((None, tm, d), tok),
                  pl.BlockSpec((None, N_MOD, d), lambda bi, i: (mod_row(bi), 0, 0)),
                  pl.BlockSpec((None, tm, cw), tok),
                  pl.BlockSpec((None, tm, dvw), tok),
                  pl.BlockSpec((None, tm, dvw), tok),
                  pl.BlockSpec((None, tm, dvw), lambda bi, i: (bi, i, 2)),
                  pl.BlockSpec((None, tm, aw), tok),
                  _const_spec(gg.shape), _const_spec(gm.shape), _layer_spec(wo_all, li)],
        out_specs=pl.BlockSpec((None, tm, d), tok),
        out_shape=jax.ShapeDtypeStruct(x.shape, F32),
        compiler_params=_cparams(2),
        name="outproj",
    )(x, mod_l, conv, o_f, o_b, pg, att, gg, gm, wo_all)


def _ffn_kernel(j, x_ref, mod_ref, g_ref, w1_ref, w2_ref, o_ref, acc_ref):
    o_ref[...] = _ffn_half(x_ref[...], mod_ref, j, g_ref[j], w1_ref, w2_ref, acc_ref)


def _ffn_call(x, mod_l, mod_row, g_l, w1_all, w2_all, li, which, j, tm):
    b, t, d = x.shape
    tok = lambda bi, i: (bi, i, 0)
    return pl.pallas_call(
        functools.partial(_ffn_kernel, j),
        grid=(b, t // tm),
        in_specs=[pl.BlockSpec((None, tm, d), tok),
                  pl.BlockSpec((None, N_MOD, d), lambda bi, i: (mod_row(bi), 0, 0)),
                  _const_spec(g_l.shape), _layer_spec(w1_all, li, which), _layer_spec(w2_all, li, which)],
        out_specs=pl.BlockSpec((None, tm, d), tok),
        out_shape=jax.ShapeDtypeStruct(x.shape, F32),
        scratch_shapes=[pltpu.VMEM((tm, d), F32)],
        compiler_params=_cparams(2),
        name="ffn",
    )(x, mod_l, g_l, w1_all, w2_all)


def _rope_tables(s):
    nf = HEAD_DIM // 4
    rows_n = s // GRID_W
    row = jnp.repeat(jnp.arange(rows_n), GRID_W).astype(F32)
    col = jnp.tile(jnp.arange(GRID_W), rows_n).astype(F32)
    freqs = ROPE_BASE ** (-jnp.arange(nf, dtype=F32) / nf)
    ang_r = row[:, None] * freqs
    ang_c = col[:, None] * freqs
    cos = jnp.concatenate([jnp.cos(ang_r)] * 2 + [jnp.cos(ang_c)] * 2, axis=1)
    sin = jnp.concatenate([-jnp.sin(ang_r), jnp.sin(ang_r), -jnp.sin(ang_c), jnp.sin(ang_c)], axis=1)
    return cos, sin


def kernel(x, c, ctx, c_ctx, w_ada, b_ada, g_norm, w_ffn_in, w_ffn_out, w_in, w_out, w_dw, b_dw,
           conv_norm_g, conv_norm_b, w_gla_gate, b_gla_gate, gla_norm_g, q_norm_g, k_norm_g):
    b, s, d = x.shape
    l = ctx.shape[1]
    depth = w_ada.shape[0]
    d_ff = w_ffn_out.shape[2]
    cw = w_dw.shape[2]
    dkw = w_gla_gate.shape[3]
    dvw = gla_norm_g.shape[1] * gla_norm_g.shape[2]
    n_kv = N_KV_HEADS * HEAD_DIM
    aw = d - cw - dvw
    tm_c = min(TOKEN_TILE, b * l)
    assert d_ff % FFN_CHUNK == 0 and s % FFN2_TILE == 0 and l % GLA_CHUNK == 0 and (b * l) % tm_c == 0
    assert dkw == LANES and n_kv == LANES and cw == 2 * LANES and dvw == 2 * LANES and aw == 4 * LANES

    rows = -(-(b + 1) // SUBLANES) * SUBLANES
    cc = jnp.zeros((rows, d), F32).at[:b].set(c).at[b].set(c_ctx)
    mod = jnp.transpose(_mod_call(cc, w_ada, b_ada), (0, 2, 1, 3))
    row_x = lambda bi: bi
    row_c = lambda bi: b

    cos64, sin64 = _rope_tables(s)
    cos_k, sin_k = jnp.tile(cos64, (1, 2)), jnp.tile(sin64, (1, 2))
    one_k, zero_k = jnp.ones((l, LANES), F32), jnp.zeros((l, LANES), F32)

    gm64_128 = _group_mean_matrix(LANES, HEAD_DIM)
    gm_conv = _group_mean_matrix(cw, cw // CONV_GROUPS)
    gm_gla = _group_mean_matrix(dvw, dvw // GLA_HEADS)

    sizes = (cw, cw, dkw, dkw, dvw, dvw, GLA_RANK, GLA_RANK, aw, n_kv, n_kv)
    offs = np.concatenate([[0], np.cumsum(sizes)])

    w1_all = w_ffn_in.astype(BF16)
    w2_all = w_ffn_out.astype(BF16)
    wo_all = w_out.astype(BF16)
    wi = w_in.astype(BF16)
    wc_all = wi[:, :, offs[0]:offs[2]]
    wg_all = jnp.concatenate([wi[:, :, offs[2]:offs[8]],
                              jnp.zeros((depth, d, LANES - 2 * GLA_RANK), BF16)], axis=2)
    wa_all = wi[:, :, offs[8]:offs[11]]
    wg2_all = jnp.zeros((depth, 2, LANES, dkw), F32)
    wg2_all = wg2_all.at[:, 0, 0:GLA_RANK].set(w_gla_gate[:, 0])
    wg2_all = wg2_all.at[:, 1, GLA_RANK:2 * GLA_RANK].set(w_gla_gate[:, 1])
    weights = (w1_all, w2_all, wc_all, wg_all, wa_all)
    flat = lambda a: a.reshape(1, b * l, a.shape[2])

    for i in range(depth):
        last = i == depth - 1
        mod_l = mod[i]
        g_l = g_norm[i].reshape(3, 1, d)
        wg2 = wg2_all[i]
        bg2 = b_gla_gate[i].reshape(2, 1, dkw)
        qg = jnp.tile(q_norm_g[i], 2).reshape(1, LANES)
        kg = jnp.tile(k_norm_g[i], 2).reshape(1, LANES)
        gg = gla_norm_g[i].reshape(1, dvw)
        wdw, bdw = w_dw[i], b_dw[i].reshape(1, cw)
        cng, cnb = conv_norm_g[i].reshape(1, cw), conv_norm_b[i].reshape(1, cw)

        x, pc_x, pg_x, pa_x = _ffn_in_call(x, mod_l, row_x, g_l, *weights, i, TOKEN_TILE)
        ctx_f, pc_c, pg_c, pa_c = _ffn_in_call(flat(ctx), mod_l, row_c, g_l, *weights, i, tm_c)
        ctx = ctx_f.reshape(b, l, d)
        pc_c, pg_c, pa_c = (p.reshape(b, l, p.shape[2]) for p in (pc_c, pg_c, pa_c))
        conv_x = _conv_call(pc_x, wdw, bdw, cng, cnb, gm_conv)
        of_c, ob_c, s_c = _gla_call(pg_c, wg2, bg2, jnp.zeros((b, 2, dvw, dkw), F32), dkw, dvw)
        of_x, ob_x, _ = _gla_call(pg_x, wg2, bg2, s_c, dkw, dvw)
        q_c, k_c, v_c = _kvprep_call(pa_c, one_k, zero_k, qg, kg, gm64_128, min(TOKEN_TILE, l))
        q_x, k_x, v_x = _kvprep_call(pa_x, cos_k, sin_k, qg, kg, gm64_128, TOKEN_TILE)
        att_x = _flash_call(q_x, [(k_c, v_c), (k_x, v_x)], *ATT_TILING[i % len(ATT_TILING)])
        x = _out_call(x, mod_l, row_x, conv_x, of_x, ob_x, pg_x, att_x, gg, gm_gla, wo_all, i, TOKEN_TILE)
        x = _ffn_call(x, mod_l, row_x, g_l, w1_all, w2_all, i, 1, 2, FFN2_TILE)
        if not last:
            conv_c = _conv_call(pc_c, wdw, bdw, cng, cnb, gm_conv)
            att_c = _flash_call(q_c, [(k_c, v_c)], *ATT_TILING[i % len(ATT_TILING)])
            ctx_f = _out_call(flat(ctx), mod_l, row_c, flat(conv_c), flat(of_c), flat(ob_c),
                              flat(pg_c), flat(att_c), gg, gm_gla, wo_all, i, tm_c)
            ctx = _ffn_call(ctx_f, mod_l, row_c, g_l, w1_all, w2_all, i, 1, 2, tm_c).reshape(b, l, d)
    return x
```

```python
import functools

import numpy as np
import jax
import jax.numpy as jnp
from jax import lax
from jax.experimental import pallas as pl
from jax.experimental.pallas import tpu as pltpu

F32 = jnp.float32
BF16 = jnp.bfloat16

CONV_GROUPS = 4
CONV_K = 31
GLA_HEADS = 4
GLA_RANK = 16
GLA_GATE_NORM = 16.0
GLA_CHUNK = 64
HEAD_DIM = 64
N_KV_HEADS = 2
GRID_W = 64
ROPE_BASE = 10000.0
N_MOD = 9
EPS = 1e-6
LOG2E = 1.4426950408889634

LANES = 128
SUBLANES = 8
VMEM_LIMIT_BYTES = 56 * 1024 * 1024

TOKEN_TILE = 512
FFN2_TILE = 1024
FFN_CHUNK = 256
CONV_ROWS = 128
CONV_HALO = 16
CONV_NORM_ROWS = 1024
GLA_BLOCK = 1024
VT_ROWS = 80
ATT_TQ = 256
ATT_TK = 1024
ATT_LOOKAHEAD = 2
ATT_MAX_RISE = 60.0


def _cparams(n_axes):
    return pltpu.CompilerParams(dimension_semantics=("arbitrary",) * n_axes,
                                vmem_limit_bytes=VMEM_LIMIT_BYTES)


def _const_spec(shape):
    nd = len(shape)
    return pl.BlockSpec(shape, lambda *_: (0,) * nd, pipeline_mode=pl.Buffered(1))


def _layer_spec(arr, *idx):
    lead = len(idx)
    blk = (None,) * lead + tuple(arr.shape[lead:])
    tail = (0,) * (arr.ndim - lead)
    return pl.BlockSpec(blk, lambda *_: tuple(idx) + tail, pipeline_mode=pl.Buffered(1))


def _dot(a, b):
    return jnp.dot(a, b, preferred_element_type=F32)


def _dot_nt(a, b):
    return lax.dot_general(a, b, (((1,), (1,)), ((), ())), preferred_element_type=F32)


def _dot_tn(a, b):
    return lax.dot_general(a, b, (((0,), (0,)), ((), ())), preferred_element_type=F32)


def _sigmoid(x):
    return 1.0 / (1.0 + jnp.exp(-x))


def _silu(x):
    return x * _sigmoid(x)


def _split_dot(v, m):
    hi = v.astype(BF16)
    lo = (v - hi.astype(F32)).astype(BF16)
    return _dot(hi, m) + _dot(lo, m)


def _split3_dot_left(m, v):
    hi = v.astype(BF16)
    r1 = v - hi.astype(F32)
    mid = r1.astype(BF16)
    lo = (r1 - mid.astype(F32)).astype(BF16)
    return _dot(m, hi) + _dot(m, mid) + _dot(m, lo)


def _pre(x, g, shift, scale):
    r = lax.rsqrt(jnp.mean(x * x, axis=-1, keepdims=True) + EPS)
    return (x * r * g) * (1.0 + scale) + shift


def _group_mean_matrix(width, group):
    idx = np.arange(width) // group
    return jnp.asarray((idx[:, None] == idx[None, :]).astype(np.float32) / group, dtype=BF16)


def _rope_partner(x):
    n = x.shape[-1]
    lane = lax.broadcasted_iota(jnp.int32, x.shape, x.ndim - 1)
    up = pltpu.roll(x, n - 16, axis=x.ndim - 1)
    dn = pltpu.roll(x, 16, axis=x.ndim - 1)
    return jnp.where((lane % 32) < 16, up, dn)


def _mod_kernel(c_ref, w_ref, b_ref, o_ref):
    sc = _silu(c_ref[...]).astype(BF16)
    o_ref[...] = _dot(sc, w_ref[...].astype(BF16)) + b_ref[...]


def _mod_call(cc, w_ada, b_ada):
    depth, d, _ = w_ada.shape
    rows = cc.shape[0]
    return pl.pallas_call(
        _mod_kernel,
        grid=(depth, N_MOD),
        in_specs=[pl.BlockSpec((rows, d), lambda i, j: (0, 0)),
                  pl.BlockSpec((None, d, d), lambda i, j: (i, 0, j)),
                  pl.BlockSpec((None, 1, d), lambda i, j: (i, 0, j))],
        out_specs=pl.BlockSpec((None, None, rows, d), lambda i, j: (i, j, 0, 0)),
        out_shape=jax.ShapeDtypeStruct((depth, N_MOD, rows, d), F32),
        compiler_params=_cparams(2),
        name="mod",
    )(cc, w_ada, b_ada.reshape(depth, 1, N_MOD * d))


def _ffn_half(x, mod_ref, j, g, w1_ref, w2_ref, acc_ref):
    d_ff = w2_ref.shape[0]
    shift = mod_ref[3 * j:3 * j + 1, :]
    scale = mod_ref[3 * j + 1:3 * j + 2, :]
    gate = mod_ref[3 * j + 2:3 * j + 3, :]
    h = _pre(x, g, shift, scale).astype(BF16)
    for c0 in range(0, d_ff, FFN_CHUNK):
        a = _dot(h, w1_ref[:, c0:c0 + FFN_CHUNK])
        u = _dot(h, w1_ref[:, d_ff + c0:d_ff + c0 + FFN_CHUNK])
        part = _dot((_silu(a) * u).astype(BF16), w2_ref[c0:c0 + FFN_CHUNK, :])
        if c0 == 0:
            acc_ref[...] = part
        else:
            acc_ref[...] += part
    return x + (0.5 * gate) * acc_ref[...]


def _ffn_in_kernel(x_ref, mod_ref, g_ref, w1_ref, w2_ref, wc_ref, wg_ref, wa_ref,
                   xo_ref, pc_ref, pg_ref, pa_ref, acc_ref):
    x1 = _ffn_half(x_ref[...], mod_ref, 0, g_ref[0], w1_ref, w2_ref, acc_ref)
    xo_ref[...] = x1
    h = _pre(x1, g_ref[1], mod_ref[3:4, :], mod_ref[4:5, :]).astype(BF16)
    pc_ref[...] = _dot(h, wc_ref[...])
    pg_ref[...] = _dot(h, wg_ref[...])
    pa_ref[...] = _dot(h, wa_ref[...])


def _ffn_in_call(x, mod_l, mod_row, g_l, w1_all, w2_all, wc_all, wg_all, wa_all, li, tm):
    b, t, d = x.shape
    tok = lambda bi, i: (bi, i, 0)
    widths = (wc_all.shape[2], wg_all.shape[2], wa_all.shape[2])
    return pl.pallas_call(
        _ffn_in_kernel,
        grid=(b, t // tm),
        in_specs=[pl.BlockSpec((None, tm, d), tok),
                  pl.BlockSpec((None, N_MOD, d), lambda bi, i: (mod_row(bi), 0, 0)),
                  _const_spec(g_l.shape), _layer_spec(w1_all, li, 0), _layer_spec(w2_all, li, 0),
                  _layer_spec(wc_all, li), _layer_spec(wg_all, li), _layer_spec(wa_all, li)],
        out_specs=[pl.BlockSpec((None, tm, d), tok)] + [pl.BlockSpec((None, tm, w), tok) for w in widths],
        out_shape=[jax.ShapeDtypeStruct(x.shape, F32)]
        + [jax.ShapeDtypeStruct((b, t, w), F32) for w in widths],
        scratch_shapes=[pltpu.VMEM((tm, d), F32)],
        compiler_params=_cparams(2),
        name="ffn_in",
    )(x, mod_l, g_l, w1_all, w2_all, wc_all, wg_all, wa_all)


def _conv_kernel(t, cw, pc_ref, wdw_ref, bdw_ref, cg_ref, cb_ref, gm_ref, o_ref, hbuf, h2_scr):
    r = CONV_ROWS
    halo = CONV_HALO
    hbuf[0:halo, :] = jnp.zeros((halo, cw), F32)
    hbuf[t + halo:t + 2 * halo, :] = jnp.zeros((halo, cw), F32)

    def glu(i, carry):
        r0 = pl.multiple_of(i * r, r)
        a = pc_ref[pl.ds(r0, r), 0:cw]
        gt = pc_ref[pl.ds(r0, r), cw:2 * cw]
        hbuf[pl.ds(r0 + halo, r), :] = a * _sigmoid(gt)
        return carry

    lax.fori_loop(0, t // r, glu, 0)
    gm = gm_ref[...]

    def tile(i, carry):
        r0 = pl.multiple_of(i * r, r)
        win = hbuf[pl.ds(r0, r + 2 * halo), :]
        acc = jnp.zeros((r, cw), F32)
        for res in range(SUBLANES):
            wr = win if res == 0 else pltpu.roll(win, r + 2 * halo - res, axis=0)
            for a in range(2 * halo // SUBLANES):
                s = SUBLANES * a + res
                if 1 <= s <= CONV_K:
                    acc = acc + wr[SUBLANES * a:SUBLANES * a + r, :] * wdw_ref[s - 1:s, :]
        h2_scr[pl.ds(r0, r), :] = acc + bdw_ref[...]
        return carry

    lax.fori_loop(0, t // r, tile, 0)

    nb = min(CONV_NORM_ROWS, t)

    def norm(i, carry):
        rows = pl.ds(pl.multiple_of(i * nb, nb), nb)
        h2 = h2_scr[rows, :]
        mu = _split_dot(h2, gm)
        dlt = h2 - mu
        var = _split_dot(dlt * dlt, gm)
        y = dlt * lax.rsqrt(var + EPS) * cg_ref[...] + cb_ref[...]
        o_ref[rows, :] = _silu(y).astype(o_ref.dtype)
        return carry

    lax.fori_loop(0, t // nb, norm, 0)


def _conv_call(pc, w_dw, b_dw, cn_g, cn_b, gm):
    b, t, two_cw = pc.shape
    cw = two_cw // 2
    return pl.pallas_call(
        functools.partial(_conv_kernel, t, cw),
        grid=(b,),
        in_specs=[pl.BlockSpec((None, t, two_cw), lambda bi: (bi, 0, 0)),
                  _const_spec(w_dw.shape), _const_spec(b_dw.shape), _const_spec(cn_g.shape),
                  _const_spec(cn_b.shape), _const_spec(gm.shape)],
        out_specs=pl.BlockSpec((None, t, cw), lambda bi: (bi, 0, 0)),
        out_shape=jax.ShapeDtypeStruct((b, t, cw), BF16),
        scratch_shapes=[pltpu.VMEM((t + 2 * CONV_HALO, cw), F32), pltpu.VMEM((t, cw), F32)],
        compiler_params=_cparams(1),
        name="conv",
    )(pc, w_dw, b_dw, cn_g, cn_b, gm)


def _gla_blocks(n_ch, dkw, dvw, pgs, wg_ref, bg_ref, outs, s_scr):
    c = GLA_CHUNK
    dk = dkw // GLA_HEADS
    dv = dvw // GLA_HEADS
    lr_off = 2 * dkw + 2 * dvw
    ri = lax.broadcasted_iota(jnp.int32, (c, c), 0)
    ci = lax.broadcasted_iota(jnp.int32, (c, c), 1)
    rih = lax.broadcasted_iota(jnp.int32, (c, GLA_HEADS * c), 0)
    cih = lax.broadcasted_iota(jnp.int32, (c, GLA_HEADS * c), 1) % c
    tri_m = [jnp.where(ci <= ri, 1.0, 0.0).astype(BF16), jnp.where(ci >= ri, 1.0, 0.0).astype(BF16)]
    tri_h = [cih <= rih, cih >= rih]
    klane = lax.broadcasted_iota(jnp.int32, (c, dkw), 1) // dk
    vlane = lax.broadcasted_iota(jnp.int32, (c, dvw), 1) // dv
    srow = lax.broadcasted_iota(jnp.int32, (dvw, dkw), 0) // dv
    scol = lax.broadcasted_iota(jnp.int32, (dvw, dkw), 1) // dk
    smask = srow == scol

    units = [(d, j) for d in range(2) for j in range(n_ch)]
    rows = [slice(j * c, (j + 1) * c) for j in range(n_ch)]
    la = []
    for d in range(2):
        z = _dot(pgs[d][:, lr_off:lr_off + LANES], wg_ref[d]) + bg_ref[d]
        la.append(-(jnp.maximum(-z, 0.0) + jnp.log(1.0 + jnp.exp(-jnp.abs(z)))) * (1.0 / GLA_GATE_NORM))
    bcum = {u: _split3_dot_left(tri_m[u[0]], la[u[0]][rows[u[1]], :]) for u in units}
    q_e, k_end, kblk, vblk, vb, decay = {}, {}, {}, {}, {}, {}
    for u in units:
        d, j = u
        q = pgs[d][rows[j], 0:dkw] * (dk ** -0.5)
        k = pgs[d][rows[j], dkw:2 * dkw]
        v = pgs[d][rows[j], 2 * dkw:2 * dkw + dvw]
        btot = bcum[u][c - 1:c, :] if d == 0 else bcum[u][0:1, :]
        q_e[u] = (q * jnp.exp(bcum[u])).astype(BF16)
        k_e = k * jnp.exp(-bcum[u])
        k_end[u] = (k * jnp.exp(btot - bcum[u])).astype(BF16)
        kblk[u] = jnp.concatenate([jnp.where(klane == h, k_e, 0.0) for h in range(GLA_HEADS)],
                                  axis=0).astype(BF16)
        vblk[u] = jnp.concatenate([jnp.where(vlane == h, v, 0.0) for h in range(GLA_HEADS)],
                                  axis=0).astype(BF16)
        vb[u] = v.astype(BF16)
        decay[u] = jnp.exp(btot)
    att = {u: jnp.where(tri_h[u[0]], _dot_nt(q_e[u], kblk[u]), 0.0).astype(BF16) for u in units}
    o_intra = {u: _dot(att[u], vblk[u]) for u in units}
    upd = {u: jnp.where(smask, _dot_tn(vb[u], k_end[u]), 0.0) for u in units}
    s_start = {}
    for d in range(2):
        s_t = s_scr[d]
        for j in (range(n_ch) if d == 0 else reversed(range(n_ch))):
            s_start[(d, j)] = s_t.astype(BF16)
            s_t = s_t * decay[(d, j)] + upd[(d, j)]
        s_scr[d] = s_t
    for u in units:
        outs[u[0]][rows[u[1]], :] = (o_intra[u] + _dot_nt(q_e[u], s_start[u])).astype(outs[u[0]].dtype)


def _gla_kernel(n_ch, dkw, dvw, pgf_ref, pgb_ref, wg_ref, bg_ref, s0_ref, of_ref, ob_ref, sfin_ref, s_scr):
    i = pl.program_id(1)

    @pl.when(i == 0)
    def _():
        s_scr[...] = s0_ref[...]

    _gla_blocks(n_ch, dkw, dvw, (pgf_ref, pgb_ref), wg_ref, bg_ref, (of_ref, ob_ref), s_scr)

    @pl.when(i == pl.num_programs(1) - 1)
    def _():
        sfin_ref[...] = s_scr[...]


def _gla_call(pg, wg2, bg2, s0, dkw, dvw):
    b, t, w = pg.shape
    blk = min(GLA_BLOCK, t)
    nb = t // blk
    n_ch = blk // GLA_CHUNK
    return pl.pallas_call(
        functools.partial(_gla_kernel, n_ch, dkw, dvw),
        grid=(b, nb),
        in_specs=[pl.BlockSpec((None, blk, w), lambda bi, i: (bi, i, 0)),
                  pl.BlockSpec((None, blk, w), lambda bi, i: (bi, nb - 1 - i, 0)),
                  _const_spec(wg2.shape), _const_spec(bg2.shape),
                  pl.BlockSpec((None, 2, dvw, dkw), lambda bi, i: (bi, 0, 0, 0))],
        out_specs=[pl.BlockSpec((None, blk, dvw), lambda bi, i: (bi, i, 0)),
                   pl.BlockSpec((None, blk, dvw), lambda bi, i: (bi, nb - 1 - i, 0)),
                   pl.BlockSpec((None, 2, dvw, dkw), lambda bi, i: (bi, 0, 0, 0))],
        out_shape=[jax.ShapeDtypeStruct((b, t, dvw), BF16), jax.ShapeDtypeStruct((b, t, dvw), BF16),
                   jax.ShapeDtypeStruct((b, 2, dvw, dkw), F32)],
        scratch_shapes=[pltpu.VMEM((2, dvw, dkw), F32)],
        compiler_params=_cparams(2),
        name="gla",
    )(pg, pg, wg2, bg2, s0)


def _kvprep_kernel(aw, pa_ref, cos_ref, sin_ref, qg_ref, kg_ref, gm_ref, q_ref, k_ref, vt_ref):
    hd = HEAD_DIM
    cos = cos_ref[...]
    sin = sin_ref[...]

    def norm_rope(x, g):
        ms = _dot((x * x).astype(BF16), gm_ref[...])
        xn = x * lax.rsqrt(ms + EPS) * g
        return xn * cos + _rope_partner(xn) * sin

    for t0 in range(0, aw, LANES):
        q_ref[:, t0:t0 + LANES] = (norm_rope(pa_ref[:, t0:t0 + LANES], qg_ref[...])
                                   * (hd ** -0.5 * LOG2E)).astype(BF16)
    vv = pa_ref[:, aw + LANES:aw + 2 * LANES]
    kr = norm_rope(pa_ref[:, aw:aw + LANES], kg_ref[...])
    lane = lax.broadcasted_iota(jnp.int32, kr.shape, 1)
    lo = lane < hd
    kswap = pltpu.roll(kr, hd, axis=1)
    k0 = jnp.where(lo, kr, kswap).astype(BF16)
    k1 = jnp.where(lo, kswap, kr).astype(BF16)
    k_ref[...] = jnp.concatenate([k0, k0, k1, k1], axis=1)
    vt_ref[0] = jnp.where(lo, vv, 1.0).T[0:VT_ROWS, :].astype(BF16)
    vt_ref[1] = jnp.where(lo, pltpu.roll(vv, hd, axis=1), 1.0).T[0:VT_ROWS, :].astype(BF16)


def _kvprep_call(pa, cos_t, sin_t, qg, kg, gm, tt):
    b, t, w = pa.shape
    aw = w - 2 * LANES
    return pl.pallas_call(
        functools.partial(_kvprep_kernel, aw),
        grid=(b, t // tt),
        in_specs=[pl.BlockSpec((None, tt, w), lambda bi, i: (bi, i, 0)),
                  pl.BlockSpec((tt, LANES), lambda bi, i: (i, 0)),
                  pl.BlockSpec((tt, LANES), lambda bi, i: (i, 0)),
                  _const_spec(qg.shape), _const_spec(kg.shape), _const_spec(gm.shape)],
        out_specs=[pl.BlockSpec((None, tt, aw), lambda bi, i: (bi, i, 0)),
                   pl.BlockSpec((None, tt, 4 * LANES), lambda bi, i: (bi, i, 0)),
                   pl.BlockSpec((None, N_KV_HEADS, VT_ROWS, tt), lambda bi, i: (bi, 0, 0, i))],
        out_shape=[jax.ShapeDtypeStruct((b, t, aw), BF16),
                   jax.ShapeDtypeStruct((b, t, 4 * LANES), BF16),
                   jax.ShapeDtypeStruct((b, N_KV_HEADS, VT_ROWS, t), BF16)],
        compiler_params=_cparams(2),
        name="kvprep",
    )(pa, cos_t, sin_t, qg, kg, gm)


def _flash_kernel(seg_lens, q_ref, *rest):
    n_seg = len(seg_lens)
    kv_refs = rest[:2 * n_seg]
    o_ref, qs_scr = rest[2 * n_seg:]
    hd = HEAD_DIM
    g_heads = 2 * LANES // hd
    qr = q_ref[...]
    lane_head = lax.broadcasted_iota(jnp.int32, qr.shape, 1) // hd
    for h in range(g_heads):
        qs_scr[h] = jnp.where(lane_head == h, qr, jnp.zeros_like(qr))

    tasks = []
    for h in range(g_heads):
        first = True
        for si in range(n_seg):
            k_ref, vt_ref = kv_refs[2 * si], kv_refs[2 * si + 1]
            tk = min(ATT_TK, seg_lens[si])
            for c0 in range(0, seg_lens[si], tk):
                tasks.append((h, k_ref, vt_ref, c0, tk, first))
                first = False

    def scores(task):
        h, k_ref, _, c0, tk, _ = task
        return _dot_nt(k_ref[c0:c0 + tk, :], qs_scr[h])

    def attend(stale_max):
        m_made = [None] * g_heads
        base = [None] * g_heads
        acc = [None] * g_heads
        rise = [None]

        def make(task):
            h, _, _, _, _, first = task
            s = scores(task)
            m_cur = jnp.max(s, axis=0, keepdims=True)
            if first:
                ref = m_cur
            elif stale_max:
                ref = m_made[h]
                rise[0] = m_cur - ref if rise[0] is None else jnp.maximum(rise[0], m_cur - ref)
            else:
                ref = jnp.maximum(m_made[h], m_cur)
            m_made[h] = m_cur if first else jnp.maximum(m_made[h], m_cur)
            return jnp.exp2(s - ref).astype(BF16), ref

        pending = [make(t) for t in tasks[:ATT_LOOKAHEAD]]
        for n, (h, _, vt_ref, c0, tk, first) in enumerate(tasks):
            if n + ATT_LOOKAHEAD < len(tasks):
                pending.append(make(tasks[n + ATT_LOOKAHEAD]))
            p, ref = pending.pop(0)
            pv = _dot(vt_ref[:, c0:c0 + tk], p)
            acc[h] = pv if first else acc[h] * jnp.exp2(base[h] - ref) + pv
            base[h] = ref
        o_t = jnp.concatenate([acc[h][0:hd, :] * (1.0 / acc[h][hd:hd + 1, :]) for h in range(g_heads)],
                              axis=0)
        return o_t.T.astype(o_ref.dtype), rise[0]

    o_fast, rise = attend(True)
    o_ref[...] = o_fast
    if rise is not None:
        @pl.when(jnp.max(rise) > ATT_MAX_RISE)
        def _():
            o_ref[...] = attend(False)[0]


def _flash_call(qr, segs):
    b, t, _ = qr.shape
    tq = min(ATT_TQ, t)
    seg_lens = tuple(k.shape[1] for k, _ in segs)
    in_specs = [pl.BlockSpec((None, tq, 2 * LANES), lambda bi, j, i: (bi, i, j))]
    args = [qr]
    for k, v in segs:
        in_specs.append(pl.BlockSpec((None, k.shape[1], 2 * LANES), lambda bi, j, i: (bi, 0, j)))
        in_specs.append(pl.BlockSpec((None, None, VT_ROWS, v.shape[3]), lambda bi, j, i: (bi, j, 0, 0)))
        args += [k, v]
    g_heads = 2 * LANES // HEAD_DIM
    return pl.pallas_call(
        functools.partial(_flash_kernel, seg_lens),
        grid=(b, N_KV_HEADS, t // tq),
        in_specs=in_specs,
        out_specs=pl.BlockSpec((None, tq, 2 * LANES), lambda bi, j, i: (bi, i, j)),
        out_shape=jax.ShapeDtypeStruct((b, t, N_KV_HEADS * 2 * LANES), BF16),
        scratch_shapes=[pltpu.VMEM((g_heads, tq, 2 * LANES), BF16)],
        compiler_params=_cparams(3),
        name="flash",
    )(*args)


def _out_kernel(cw, dvw, x_ref, mod_ref, conv_ref, of_ref, ob_ref, r_ref, att_ref, gg_ref, gm_ref,
                wo_ref, o_ref):
    o = of_ref[...].astype(F32) + ob_ref[...].astype(F32)
    ms = _split_dot(o * o, gm_ref[...])
    gl = (o * lax.rsqrt(ms + EPS) * gg_ref[...]) * _silu(r_ref[...])
    y = (_dot(conv_ref[...], wo_ref[0:cw, :])
         + _dot(gl.astype(BF16), wo_ref[cw:cw + dvw, :])
         + _dot(att_ref[...], wo_ref[cw + dvw:, :]))
    o_ref[...] = x_ref[...] + mod_ref[5:6, :] * y


def _out_call(x, mod_l, mod_row, conv, o_f, o_b, pg, att, gg, gm, wo_all, li, tm):
    b, t, d = x.shape
    cw = conv.shape[2]
    dvw = o_f.shape[2]
    aw = att.shape[2]
    tok = lambda bi, i: (bi, i, 0)
    return pl.pallas_call(
        functools.partial(_out_kernel, cw, dvw),
        grid=(b, t // tm),
        in_specs=[pl.BlockSpec((None, tm, d), tok),
                  pl.BlockSpec((None, N_MOD, d), lambda bi, i: (mod_row(bi), 0, 0)),
                  pl.BlockSpec((None, tm, cw), tok),
                  pl.BlockSpec((None, tm, dvw), tok),
                  pl.BlockSpec((None, tm, dvw), tok),
                  pl.BlockSpec((None, tm, dvw), lambda bi, i: (bi, i, 2)),
                  pl.BlockSpec((None, tm, aw), tok),
                  _const_spec(gg.shape), _const_spec(gm.shape), _layer_spec(wo_all, li)],
        out_specs=pl.BlockSpec((None, tm, d), tok),
        out_shape=jax.ShapeDtypeStruct(x.shape, F32),
        compiler_params=_cparams(2),
        name="outproj",
    )(x, mod_l, conv, o_f, o_b, pg, att, gg, gm, wo_all)


def _ffn_kernel(j, x_ref, mod_ref, g_ref, w1_ref, w2_ref, o_ref, acc_ref):
    o_ref[...] = _ffn_half(x_ref[...], mod_ref, j, g_ref[j], w1_ref, w2_ref, acc_ref)


def _ffn_call(x, mod_l, mod_row, g_l, w1_all, w2_all, li, which, j, tm):
    b, t, d = x.shape
    tok = lambda bi, i: (bi, i, 0)
    return pl.pallas_call(
        functools.partial(_ffn_kernel, j),
        grid=(b, t // tm),
        in_specs=[pl.BlockSpec((None, tm, d), tok),
                  pl.BlockSpec((None, N_MOD, d), lambda bi, i: (mod_row(bi), 0, 0)),
                  _const_spec(g_l.shape), _layer_spec(w1_all, li, which), _layer_spec(w2_all, li, which)],
        out_specs=pl.BlockSpec((None, tm, d), tok),
        out_shape=jax.ShapeDtypeStruct(x.shape, F32),
        scratch_shapes=[pltpu.VMEM((tm, d), F32)],
        compiler_params=_cparams(2),
        name="ffn",
    )(x, mod_l, g_l, w1_all, w2_all)


def _rope_tables(s):
    nf = HEAD_DIM // 4
    rows_n = s // GRID_W
    row = jnp.repeat(jnp.arange(rows_n), GRID_W).astype(F32)
    col = jnp.tile(jnp.arange(GRID_W), rows_n).astype(F32)
    freqs = ROPE_BASE ** (-jnp.arange(nf, dtype=F32) / nf)
    ang_r = row[:, None] * freqs
    ang_c = col[:, None] * freqs
    cos = jnp.concatenate([jnp.cos(ang_r)] * 2 + [jnp.cos(ang_c)] * 2, axis=1)
    sin = jnp.concatenate([-jnp.sin(ang_r), jnp.sin(ang_r), -jnp.sin(ang_c), jnp.sin(ang_c)], axis=1)
    return cos, sin


def kernel(x, c, ctx, c_ctx, w_ada, b_ada, g_norm, w_ffn_in, w_ffn_out, w_in, w_out, w_dw, b_dw,
           conv_norm_g, conv_norm_b, w_gla_gate, b_gla_gate, gla_norm_g, q_norm_g, k_norm_g):
    b, s, d = x.shape
    l = ctx.shape[1]
    depth = w_ada.shape[0]
    d_ff = w_ffn_out.shape[2]
    cw = w_dw.shape[2]
    dkw = w_gla_gate.shape[3]
    dvw = gla_norm_g.shape[1] * gla_norm_g.shape[2]
    n_kv = N_KV_HEADS * HEAD_DIM
    aw = d - cw - dvw
    tm_c = min(TOKEN_TILE, b * l)
    assert d_ff % FFN_CHUNK == 0 and s % FFN2_TILE == 0 and l % GLA_CHUNK == 0 and (b * l) % tm_c == 0
    assert dkw == LANES and n_kv == LANES and cw == 2 * LANES and dvw == 2 * LANES and aw == 4 * LANES

    rows = -(-(b + 1) // SUBLANES) * SUBLANES
    cc = jnp.zeros((rows, d), F32).at[:b].set(c).at[b].set(c_ctx)
    mod = jnp.transpose(_mod_call(cc, w_ada, b_ada), (0, 2, 1, 3))
    row_x = lambda bi: bi
    row_c = lambda bi: b

    cos64, sin64 = _rope_tables(s)
    cos_k, sin_k = jnp.tile(cos64, (1, 2)), jnp.tile(sin64, (1, 2))
    one_k, zero_k = jnp.ones((l, LANES), F32), jnp.zeros((l, LANES), F32)

    gm64_128 = _group_mean_matrix(LANES, HEAD_DIM)
    gm_conv = _group_mean_matrix(cw, cw // CONV_GROUPS)
    gm_gla = _group_mean_matrix(dvw, dvw // GLA_HEADS)

    sizes = (cw, cw, dkw, dkw, dvw, dvw, GLA_RANK, GLA_RANK, aw, n_kv, n_kv)
    offs = np.concatenate([[0], np.cumsum(sizes)])

    w1_all = w_ffn_in.astype(BF16)
    w2_all = w_ffn_out.astype(BF16)
    wo_all = w_out.astype(BF16)
    wi = w_in.astype(BF16)
    wc_all = wi[:, :, offs[0]:offs[2]]
    wg_all = jnp.concatenate([wi[:, :, offs[2]:offs[8]],
                              jnp.zeros((depth, d, LANES - 2 * GLA_RANK), BF16)], axis=2)
    wa_all = wi[:, :, offs[8]:offs[11]]
    wg2_all = jnp.zeros((depth, 2, LANES, dkw), F32)
    wg2_all = wg2_all.at[:, 0, 0:GLA_RANK].set(w_gla_gate[:, 0])
    wg2_all = wg2_all.at[:, 1, GLA_RANK:2 * GLA_RANK].set(w_gla_gate[:, 1])
    weights = (w1_all, w2_all, wc_all, wg_all, wa_all)
    flat = lambda a: a.reshape(1, b * l, a.shape[2])

    for i in range(depth):
        last = i == depth - 1
        mod_l = mod[i]
        g_l = g_norm[i].reshape(3, 1, d)
        wg2 = wg2_all[i]
        bg2 = b_gla_gate[i].reshape(2, 1, dkw)
        qg = jnp.tile(q_norm_g[i], 2).reshape(1, LANES)
        kg = jnp.tile(k_norm_g[i], 2).reshape(1, LANES)
        gg = gla_norm_g[i].reshape(1, dvw)
        wdw, bdw = w_dw[i], b_dw[i].reshape(1, cw)
        cng, cnb = conv_norm_g[i].reshape(1, cw), conv_norm_b[i].reshape(1, cw)

        x, pc_x, pg_x, pa_x = _ffn_in_call(x, mod_l, row_x, g_l, *weights, i, TOKEN_TILE)
        ctx_f, pc_c, pg_c, pa_c = _ffn_in_call(flat(ctx), mod_l, row_c, g_l, *weights, i, tm_c)
        ctx = ctx_f.reshape(b, l, d)
        pc_c, pg_c, pa_c = (p.reshape(b, l, p.shape[2]) for p in (pc_c, pg_c, pa_c))
        conv_x = _conv_call(pc_x, wdw, bdw, cng, cnb, gm_conv)
        of_c, ob_c, s_c = _gla_call(pg_c, wg2, bg2, jnp.zeros((b, 2, dvw, dkw), F32), dkw, dvw)
        of_x, ob_x, _ = _gla_call(pg_x, wg2, bg2, s_c, dkw, dvw)
        q_c, k_c, v_c = _kvprep_call(pa_c, one_k, zero_k, qg, kg, gm64_128, min(TOKEN_TILE, l))
        q_x, k_x, v_x = _kvprep_call(pa_x, cos_k, sin_k, qg, kg, gm64_128, TOKEN_TILE)
        att_x = _flash_call(q_x, [(k_c, v_c), (k_x, v_x)])
        x = _out_call(x, mod_l, row_x, conv_x, of_x, ob_x, pg_x, att_x, gg, gm_gla, wo_all, i, TOKEN_TILE)
        x = _ffn_call(x, mod_l, row_x, g_l, w1_all, w2_all, i, 1, 2, FFN2_TILE)
        if not last:
            conv_c = _conv_call(pc_c, wdw, bdw, cng, cnb, gm_conv)
            att_c = _flash_call(q_c, [(k_c, v_c)])
            ctx_f = _out_call(flat(ctx), mod_l, row_c, flat(conv_c), flat(of_c), flat(ob_c),
                              flat(pg_c), flat(att_c), gg, gm_gla, wo_all, i, tm_c)
            ctx = _ffn_call(ctx_f, mod_l, row_c, g_l, w1_all, w2_all, i, 1, 2, tm_c).reshape(b, l, d)
    return x
```

```python
import functools

import numpy as np
import jax
import jax.numpy as jnp
from jax import lax
from jax.experimental import pallas as pl
from jax.experimental.pallas import tpu as pltpu

F32 = jnp.float32
BF16 = jnp.bfloat16

CONV_GROUPS = 4
CONV_K = 31
GLA_HEADS = 4
GLA_RANK = 16
GLA_GATE_NORM = 16.0
GLA_CHUNK = 64
HEAD_DIM = 64
N_KV_HEADS = 2
GRID_W = 64
ROPE_BASE = 10000.0
N_MOD = 9
EPS = 1e-6
LOG2E = 1.4426950408889634

LANES = 128
SUBLANES = 8
VMEM_LIMIT_BYTES = 56 * 1024 * 1024

TOKEN_TILE = 512
FFN2_TILE = 1024
FFN_CHUNK = 256
CONV_ROWS = 128
CONV_HALO = 16
CONV_NORM_ROWS = 1024
GLA_BLOCK = 2048
VT_ROWS = 80
ATT_TQ = 256
ATT_TK = 1024
ATT_LOOKAHEAD = 2
ATT_MAX_RISE = 60.0


def _cparams(n_axes):
    return pltpu.CompilerParams(dimension_semantics=("arbitrary",) * n_axes,
                                vmem_limit_bytes=VMEM_LIMIT_BYTES)


def _const_spec(shape):
    nd = len(shape)
    return pl.BlockSpec(shape, lambda *_: (0,) * nd, pipeline_mode=pl.Buffered(1))


def _layer_spec(arr, *idx):
    lead = len(idx)
    blk = (None,) * lead + tuple(arr.shape[lead:])
    tail = (0,) * (arr.ndim - lead)
    return pl.BlockSpec(blk, lambda *_: tuple(idx) + tail, pipeline_mode=pl.Buffered(1))


def _dot(a, b):
    return jnp.dot(a, b, preferred_element_type=F32)


def _dot_nt(a, b):
    return lax.dot_general(a, b, (((1,), (1,)), ((), ())), preferred_element_type=F32)


def _dot_tn(a, b):
    return lax.dot_general(a, b, (((0,), (0,)), ((), ())), preferred_element_type=F32)


def _sigmoid(x):
    return 1.0 / (1.0 + jnp.exp(-x))


def _silu(x):
    return x * _sigmoid(x)


def _split_dot(v, m):
    hi = v.astype(BF16)
    lo = (v - hi.astype(F32)).astype(BF16)
    return _dot(hi, m) + _dot(lo, m)


def _split3_dot_left(m, v):
    hi = v.astype(BF16)
    r1 = v - hi.astype(F32)
    mid = r1.astype(BF16)
    lo = (r1 - mid.astype(F32)).astype(BF16)
    return _dot(m, hi) + _dot(m, mid) + _dot(m, lo)


def _pre(x, g, shift, scale):
    r = lax.rsqrt(jnp.mean(x * x, axis=-1, keepdims=True) + EPS)
    return (x * r * g) * (1.0 + scale) + shift


def _group_mean_matrix(width, group):
    idx = np.arange(width) // group
    return jnp.asarray((idx[:, None] == idx[None, :]).astype(np.float32) / group, dtype=BF16)


def _rope_partner_matrix(width):
    lane = np.arange(width)
    partner = np.where(lane % 32 < 16, lane + 16, lane - 16)
    return jnp.asarray((lane[:, None] == partner[None, :]).astype(np.float32), dtype=BF16)


def _mod_kernel(c_ref, w_ref, b_ref, o_ref):
    sc = _silu(c_ref[...]).astype(BF16)
    o_ref[...] = _dot(sc, w_ref[...].astype(BF16)) + b_ref[...]


def _mod_call(cc, w_ada, b_ada):
    depth, d, _ = w_ada.shape
    rows = cc.shape[0]
    return pl.pallas_call(
        _mod_kernel,
        grid=(depth, N_MOD),
        in_specs=[pl.BlockSpec((rows, d), lambda i, j: (0, 0)),
                  pl.BlockSpec((None, d, d), lambda i, j: (i, 0, j)),
                  pl.BlockSpec((None, 1, d), lambda i, j: (i, 0, j))],
        out_specs=pl.BlockSpec((None, None, rows, d), lambda i, j: (i, j, 0, 0)),
        out_shape=jax.ShapeDtypeStruct((depth, N_MOD, rows, d), F32),
        compiler_params=_cparams(2),
        name="mod",
    )(cc, w_ada, b_ada.reshape(depth, 1, N_MOD * d))


def _ffn_half(x, mod_ref, j, g, w1_ref, w2_ref, acc_ref):
    d_ff = w2_ref.shape[0]
    shift = mod_ref[3 * j:3 * j + 1, :]
    scale = mod_ref[3 * j + 1:3 * j + 2, :]
    gate = mod_ref[3 * j + 2:3 * j + 3, :]
    h = _pre(x, g, shift, scale).astype(BF16)
    for c0 in range(0, d_ff, FFN_CHUNK):
        a = _dot(h, w1_ref[:, c0:c0 + FFN_CHUNK])
        u = _dot(h, w1_ref[:, d_ff + c0:d_ff + c0 + FFN_CHUNK])
        part = _dot((_silu(a) * u).astype(BF16), w2_ref[c0:c0 + FFN_CHUNK, :])
        if c0 == 0:
            acc_ref[...] = part
        else:
            acc_ref[...] += part
    return x + (0.5 * gate) * acc_ref[...]


def _ffn_in_kernel(x_ref, mod_ref, g_ref, w1_ref, w2_ref, wc_ref, wg_ref, wa_ref,
                   xo_ref, pc_ref, pg_ref, pa_ref, acc_ref):
    x1 = _ffn_half(x_ref[...], mod_ref, 0, g_ref[0], w1_ref, w2_ref, acc_ref)
    xo_ref[...] = x1
    h = _pre(x1, g_ref[1], mod_ref[3:4, :], mod_ref[4:5, :]).astype(BF16)
    pc_ref[...] = _dot(h, wc_ref[...])
    pg_ref[...] = _dot(h, wg_ref[...])
    pa_ref[...] = _dot(h, wa_ref[...])


def _ffn_in_call(x, mod_l, mod_row, g_l, w1_all, w2_all, wc_all, wg_all, wa_all, li, tm):
    b, t, d = x.shape
    tok = lambda bi, i: (bi, i, 0)
    widths = (wc_all.shape[2], wg_all.shape[2], wa_all.shape[2])
    return pl.pallas_call(
        _ffn_in_kernel,
        grid=(b, t // tm),
        in_specs=[pl.BlockSpec((None, tm, d), tok),
                  pl.BlockSpec((None, N_MOD, d), lambda bi, i: (mod_row(bi), 0, 0)),
                  _const_spec(g_l.shape), _layer_spec(w1_all, li, 0), _layer_spec(w2_all, li, 0),
                  _layer_spec(wc_all, li), _layer_spec(wg_all, li), _layer_spec(wa_all, li)],
        out_specs=[pl.BlockSpec((None, tm, d), tok)] + [pl.BlockSpec((None, tm, w), tok) for w in widths],
        out_shape=[jax.ShapeDtypeStruct(x.shape, F32)]
        + [jax.ShapeDtypeStruct((b, t, w), F32) for w in widths],
        scratch_shapes=[pltpu.VMEM((tm, d), F32)],
        compiler_params=_cparams(2),
        name="ffn_in",
    )(x, mod_l, g_l, w1_all, w2_all, wc_all, wg_all, wa_all)


def _conv_kernel(t, cw, pc_ref, wdw_ref, bdw_ref, cg_ref, cb_ref, gm_ref, o_ref, hbuf, h2_scr):
    r = CONV_ROWS
    halo = CONV_HALO
    hbuf[0:halo, :] = jnp.zeros((halo, cw), F32)
    hbuf[t + halo:t + 2 * halo, :] = jnp.zeros((halo, cw), F32)

    def glu(i, carry):
        r0 = pl.multiple_of(i * r, r)
        a = pc_ref[pl.ds(r0, r), 0:cw]
        gt = pc_ref[pl.ds(r0, r), cw:2 * cw]
        hbuf[pl.ds(r0 + halo, r), :] = a * _sigmoid(gt)
        return carry

    lax.fori_loop(0, t // r, glu, 0)
    gm = gm_ref[...]

    def tile(i, carry):
        r0 = pl.multiple_of(i * r, r)
        win = hbuf[pl.ds(r0, r + 2 * halo), :]
        acc = jnp.zeros((r, cw), F32)
        for res in range(SUBLANES):
            wr = win if res == 0 else pltpu.roll(win, r + 2 * halo - res, axis=0)
            for a in range(2 * halo // SUBLANES):
                s = SUBLANES * a + res
                if 1 <= s <= CONV_K:
                    acc = acc + wr[SUBLANES * a:SUBLANES * a + r, :] * wdw_ref[s - 1:s, :]
        h2_scr[pl.ds(r0, r), :] = acc + bdw_ref[...]
        return carry

    lax.fori_loop(0, t // r, tile, 0)

    nb = min(CONV_NORM_ROWS, t)
    blocks = [slice(r0, r0 + nb) for r0 in range(0, t, nb)]
    h2 = [h2_scr[rows, :] for rows in blocks]
    mu = [_split_dot(v, gm) for v in h2]
    dlt = [v - m for v, m in zip(h2, mu)]
    var = [_split_dot(dv * dv, gm) for dv in dlt]
    for rows, dv, vr in zip(blocks, dlt, var):
        y = dv * lax.rsqrt(vr + EPS) * cg_ref[...] + cb_ref[...]
        o_ref[rows, :] = _silu(y).astype(o_ref.dtype)


def _conv_call(pc, w_dw, b_dw, cn_g, cn_b, gm):
    b, t, two_cw = pc.shape
    cw = two_cw // 2
    return pl.pallas_call(
        functools.partial(_conv_kernel, t, cw),
        grid=(b,),
        in_specs=[pl.BlockSpec((None, t, two_cw), lambda bi: (bi, 0, 0)),
                  _const_spec(w_dw.shape), _const_spec(b_dw.shape), _const_spec(cn_g.shape),
                  _const_spec(cn_b.shape), _const_spec(gm.shape)],
        out_specs=pl.BlockSpec((None, t, cw), lambda bi: (bi, 0, 0)),
        out_shape=jax.ShapeDtypeStruct((b, t, cw), BF16),
        scratch_shapes=[pltpu.VMEM((t + 2 * CONV_HALO, cw), F32), pltpu.VMEM((t, cw), F32)],
        compiler_params=_cparams(1),
        name="conv",
    )(pc, w_dw, b_dw, cn_g, cn_b, gm)


def _gla_blocks(n_ch, dkw, dvw, pgs, wg_ref, bg_ref, outs, s_scr):
    c = GLA_CHUNK
    dk = dkw // GLA_HEADS
    dv = dvw // GLA_HEADS
    lr_off = 2 * dkw + 2 * dvw
    ri = lax.broadcasted_iota(jnp.int32, (c, c), 0)
    ci = lax.broadcasted_iota(jnp.int32, (c, c), 1)
    rih = lax.broadcasted_iota(jnp.int32, (c, GLA_HEADS * c), 0)
    cih = lax.broadcasted_iota(jnp.int32, (c, GLA_HEADS * c), 1) % c
    tri_m = [jnp.where(ci <= ri, 1.0, 0.0).astype(BF16), jnp.where(ci >= ri, 1.0, 0.0).astype(BF16)]
    tri_h = [cih <= rih, cih >= rih]
    klane = lax.broadcasted_iota(jnp.int32, (c, dkw), 1) // dk
    vlane = lax.broadcasted_iota(jnp.int32, (c, dvw), 1) // dv
    srow = lax.broadcasted_iota(jnp.int32, (dvw, dkw), 0) // dv
    scol = lax.broadcasted_iota(jnp.int32, (dvw, dkw), 1) // dk
    smask = srow == scol

    units = [(d, j) for d in range(2) for j in range(n_ch)]
    rows = [slice(j * c, (j + 1) * c) for j in range(n_ch)]
    la = []
    for d in range(2):
        z = _dot(pgs[d][:, lr_off:lr_off + LANES], wg_ref[d]) + bg_ref[d]
        la.append(-(jnp.maximum(-z, 0.0) + jnp.log(1.0 + jnp.exp(-jnp.abs(z)))) * (1.0 / GLA_GATE_NORM))
    bcum = {u: _split3_dot_left(tri_m[u[0]], la[u[0]][rows[u[1]], :]) for u in units}
    q_e, k_end, kblk, vblk, vb, decay = {}, {}, {}, {}, {}, {}
    for u in units:
        d, j = u
        q = pgs[d][rows[j], 0:dkw] * (dk ** -0.5)
        k = pgs[d][rows[j], dkw:2 * dkw]
        v = pgs[d][rows[j], 2 * dkw:2 * dkw + dvw]
        btot = bcum[u][c - 1:c, :] if d == 0 else bcum[u][0:1, :]
        q_e[u] = (q * jnp.exp(bcum[u])).astype(BF16)
        k_e = k * jnp.exp(-bcum[u])
        k_end[u] = (k * jnp.exp(btot - bcum[u])).astype(BF16)
        kblk[u] = jnp.concatenate([jnp.where(klane == h, k_e, 0.0) for h in range(GLA_HEADS)],
                                  axis=0).astype(BF16)
        vblk[u] = jnp.concatenate([jnp.where(vlane == h, v, 0.0) for h in range(GLA_HEADS)],
                                  axis=0).astype(BF16)
        vb[u] = v.astype(BF16)
        decay[u] = jnp.exp(btot)
    att = {u: jnp.where(tri_h[u[0]], _dot_nt(q_e[u], kblk[u]), 0.0).astype(BF16) for u in units}
    o_intra = {u: _dot(att[u], vblk[u]) for u in units}
    upd = {u: jnp.where(smask, _dot_tn(vb[u], k_end[u]), 0.0) for u in units}
    s_start = {}
    for d in range(2):
        s_t = s_scr[d]
        for j in (range(n_ch) if d == 0 else reversed(range(n_ch))):
            s_start[(d, j)] = s_t.astype(BF16)
            s_t = s_t * decay[(d, j)] + upd[(d, j)]
        s_scr[d] = s_t
    for u in units:
        outs[u[0]][rows[u[1]], :] = (o_intra[u] + _dot_nt(q_e[u], s_start[u])).astype(outs[u[0]].dtype)


def _gla_kernel(n_ch, dkw, dvw, pgf_ref, pgb_ref, wg_ref, bg_ref, s0_ref, of_ref, ob_ref, sfin_ref, s_scr):
    i = pl.program_id(1)

    @pl.when(i == 0)
    def _():
        s_scr[...] = s0_ref[...]

    _gla_blocks(n_ch, dkw, dvw, (pgf_ref, pgb_ref), wg_ref, bg_ref, (of_ref, ob_ref), s_scr)

    @pl.when(i == pl.num_programs(1) - 1)
    def _():
        sfin_ref[...] = s_scr[...]


def _gla_call(pg, wg2, bg2, s0, dkw, dvw):
    b, t, w = pg.shape
    blk = min(GLA_BLOCK, t)
    nb = t // blk
    n_ch = blk // GLA_CHUNK
    return pl.pallas_call(
        functools.partial(_gla_kernel, n_ch, dkw, dvw),
        grid=(b, nb),
        in_specs=[pl.BlockSpec((None, blk, w), lambda bi, i: (bi, i, 0)),
                  pl.BlockSpec((None, blk, w), lambda bi, i: (bi, nb - 1 - i, 0)),
                  _const_spec(wg2.shape), _const_spec(bg2.shape),
                  pl.BlockSpec((None, 2, dvw, dkw), lambda bi, i: (bi, 0, 0, 0))],
        out_specs=[pl.BlockSpec((None, blk, dvw), lambda bi, i: (bi, i, 0)),
                   pl.BlockSpec((None, blk, dvw), lambda bi, i: (bi, nb - 1 - i, 0)),
                   pl.BlockSpec((None, 2, dvw, dkw), lambda bi, i: (bi, 0, 0, 0))],
        out_shape=[jax.ShapeDtypeStruct((b, t, dvw), BF16), jax.ShapeDtypeStruct((b, t, dvw), BF16),
                   jax.ShapeDtypeStruct((b, 2, dvw, dkw), F32)],
        scratch_shapes=[pltpu.VMEM((2, dvw, dkw), F32)],
        compiler_params=_cparams(2),
        name="gla",
    )(pg, pg, wg2, bg2, s0)


def _kvprep_kernel(aw, pa_ref, cos_ref, sin_ref, qg_ref, kg_ref, gm_ref, perm_ref, q_ref, k_ref, vt_ref):
    hd = HEAD_DIM
    cos = cos_ref[...]
    sin = sin_ref[...]

    n_q = aw // LANES
    xs = [pa_ref[:, t * LANES:(t + 1) * LANES] for t in range(n_q + 1)]
    gains = [qg_ref[...]] * n_q + [kg_ref[...]]
    ms = [_dot((x * x).astype(BF16), gm_ref[...]) for x in xs]
    xn = [x * lax.rsqrt(m + EPS) * g for x, m, g in zip(xs, ms, gains)]
    partner = [_split_dot(v, perm_ref[...]) for v in xn]
    rot = [v * cos + pt * sin for v, pt in zip(xn, partner)]
    for t in range(n_q):
        q_ref[:, t * LANES:(t + 1) * LANES] = (rot[t] * (hd ** -0.5 * LOG2E)).astype(BF16)
    vv = pa_ref[:, aw + LANES:aw + 2 * LANES]
    kr = rot[n_q]
    lane = lax.broadcasted_iota(jnp.int32, kr.shape, 1)
    lo = lane < hd
    kswap = pltpu.roll(kr, hd, axis=1)
    k0 = jnp.where(lo, kr, kswap).astype(BF16)
    k1 = jnp.where(lo, kswap, kr).astype(BF16)
    k_ref[...] = jnp.concatenate([k0, k0, k1, k1], axis=1)
    vt_ref[0] = jnp.where(lo, vv, 1.0).T[0:VT_ROWS, :].astype(BF16)
    vt_ref[1] = jnp.where(lo, pltpu.roll(vv, hd, axis=1), 1.0).T[0:VT_ROWS, :].astype(BF16)


def _kvprep_call(pa, cos_t, sin_t, qg, kg, gm, perm, tt):
    b, t, w = pa.shape
    aw = w - 2 * LANES
    return pl.pallas_call(
        functools.partial(_kvprep_kernel, aw),
        grid=(b, t // tt),
        in_specs=[pl.BlockSpec((None, tt, w), lambda bi, i: (bi, i, 0)),
                  pl.BlockSpec((tt, LANES), lambda bi, i: (i, 0)),
                  pl.BlockSpec((tt, LANES), lambda bi, i: (i, 0)),
                  _const_spec(qg.shape), _const_spec(kg.shape), _const_spec(gm.shape),
                  _const_spec(perm.shape)],
        out_specs=[pl.BlockSpec((None, tt, aw), lambda bi, i: (bi, i, 0)),
                   pl.BlockSpec((None, tt, 4 * LANES), lambda bi, i: (bi, i, 0)),
                   pl.BlockSpec((None, N_KV_HEADS, VT_ROWS, tt), lambda bi, i: (bi, 0, 0, i))],
        out_shape=[jax.ShapeDtypeStruct((b, t, aw), BF16),
                   jax.ShapeDtypeStruct((b, t, 4 * LANES), BF16),
                   jax.ShapeDtypeStruct((b, N_KV_HEADS, VT_ROWS, t), BF16)],
        compiler_params=_cparams(2),
        name="kvprep",
    )(pa, cos_t, sin_t, qg, kg, gm, perm)


def _flash_kernel(seg_lens, q_ref, *rest):
    n_seg = len(seg_lens)
    kv_refs = rest[:2 * n_seg]
    o_ref, qs_scr = rest[2 * n_seg:]
    hd = HEAD_DIM
    kvw = 2 * LANES
    g_heads = kvw // hd
    n_heads = N_KV_HEADS * g_heads
    lane_head = lax.broadcasted_iota(jnp.int32, (q_ref.shape[0], kvw), 1) // hd
    for h in range(n_heads):
        j, g = divmod(h, g_heads)
        qj = q_ref[:, j * kvw:(j + 1) * kvw]
        qs_scr[h] = jnp.where(lane_head == g, qj, jnp.zeros_like(qj))

    tasks = []
    for h in range(n_heads):
        first = True
        for si in range(n_seg):
            k_ref, vt_ref = kv_refs[2 * si], kv_refs[2 * si + 1]
            tk = min(ATT_TK, seg_lens[si])
            for c0 in range(0, seg_lens[si], tk):
                tasks.append((h, k_ref, vt_ref, c0, tk, first))
                first = False

    def scores(task):
        h, k_ref, _, c0, tk, _ = task
        j = h // g_heads
        return _dot_nt(k_ref[c0:c0 + tk, j * kvw:(j + 1) * kvw], qs_scr[h])

    def attend(stale_max):
        m_made = [None] * n_heads
        base = [None] * n_heads
        acc = [None] * n_heads
        rise = [None]

        def make(task):
            h, _, _, _, _, first = task
            s = scores(task)
            m_cur = jnp.max(s, axis=0, keepdims=True)
            if first:
                ref = m_cur
            elif stale_max:
                ref = m_made[h]
                rise[0] = m_cur - ref if rise[0] is None else jnp.maximum(rise[0], m_cur - ref)
            else:
                ref = jnp.maximum(m_made[h], m_cur)
            m_made[h] = m_cur if first else jnp.maximum(m_made[h], m_cur)
            return jnp.exp2(s - ref).astype(BF16), ref

        pending = [make(t) for t in tasks[:ATT_LOOKAHEAD]]
        for n, (h, _, vt_ref, c0, tk, first) in enumerate(tasks):
            if n + ATT_LOOKAHEAD < len(tasks):
                pending.append(make(tasks[n + ATT_LOOKAHEAD]))
            p, ref = pending.pop(0)
            pv = _dot(vt_ref[h // g_heads, :, c0:c0 + tk], p)
            acc[h] = pv if first else acc[h] * jnp.exp2(base[h] - ref) + pv
            base[h] = ref
        o_t = jnp.concatenate([acc[h][0:hd, :] * (1.0 / acc[h][hd:hd + 1, :]) for h in range(n_heads)],
                              axis=0)
        return o_t.T.astype(o_ref.dtype), rise[0]

    o_fast, rise = attend(True)
    o_ref[...] = o_fast
    if rise is not None:
        @pl.when(jnp.max(rise) > ATT_MAX_RISE)
        def _():
            o_ref[...] = attend(False)[0]


def _flash_call(qr, segs):
    b, t, _ = qr.shape
    tq = min(ATT_TQ, t)
    seg_lens = tuple(k.shape[1] for k, _ in segs)
    w = qr.shape[2]
    in_specs = [pl.BlockSpec((None, tq, w), lambda bi, i: (bi, i, 0))]
    args = [qr]
    for k, v in segs:
        in_specs.append(pl.BlockSpec((None, k.shape[1], k.shape[2]), lambda bi, i: (bi, 0, 0)))
        in_specs.append(pl.BlockSpec((None, N_KV_HEADS, VT_ROWS, v.shape[3]), lambda bi, i: (bi, 0, 0, 0)))
        args += [k, v]
    return pl.pallas_call(
        functools.partial(_flash_kernel, seg_lens),
        grid=(b, t // tq),
        in_specs=in_specs,
        out_specs=pl.BlockSpec((None, tq, w), lambda bi, i: (bi, i, 0)),
        out_shape=jax.ShapeDtypeStruct((b, t, w), BF16),
        scratch_shapes=[pltpu.VMEM((w // HEAD_DIM, tq, 2 * LANES), BF16)],
        compiler_params=_cparams(2),
        name="flash",
    )(*args)


def _out_kernel(cw, dvw, x_ref, mod_ref, conv_ref, of_ref, ob_ref, r_ref, att_ref, gg_ref, gm_ref,
                wo_ref, o_ref):
    o = of_ref[...].astype(F32) + ob_ref[...].astype(F32)
    ms = _split_dot(o * o, gm_ref[...])
    gl = (o * lax.rsqrt(ms + EPS) * gg_ref[...]) * _silu(r_ref[...])
    y = (_dot(conv_ref[...], wo_ref[0:cw, :])
         + _dot(gl.astype(BF16), wo_ref[cw:cw + dvw, :])
         + _dot(att_ref[...], wo_ref[cw + dvw:, :]))
    o_ref[...] = x_ref[...] + mod_ref[5:6, :] * y


def _out_call(x, mod_l, mod_row, conv, o_f, o_b, pg, att, gg, gm, wo_all, li, tm):
    b, t, d = x.shape
    cw = conv.shape[2]
    dvw = o_f.shape[2]
    aw = att.shape[2]
    tok = lambda bi, i: (bi, i, 0)
    return pl.pallas_call(
        functools.partial(_out_kernel, cw, dvw),
        grid=(b, t // tm),
        in_specs=[pl.BlockSpec((None, tm, d), tok),
                  pl.BlockSpec((None, N_MOD, d), lambda bi, i: (mod_row(bi), 0, 0)),
                  pl.BlockSpec((None, tm, cw), tok),
                  pl.BlockSpec((None, tm, dvw), tok),
                  pl.BlockSpec((None, tm, dvw), tok),
                  pl.BlockSpec((None, tm, dvw), lambda bi, i: (bi, i, 2)),
                  pl.BlockSpec((None, tm, aw), tok),
                  _const_spec(gg.shape), _const_spec(gm.shape), _layer_spec(wo_all, li)],
        out_specs=pl.BlockSpec((None, tm, d), tok),
        out_shape=jax.ShapeDtypeStruct(x.shape, F32),
        compiler_params=_cparams(2),
        name="outproj",
    )(x, mod_l, conv, o_f, o_b, pg, att, gg, gm, wo_all)


def _ffn_kernel(j, x_ref, mod_ref, g_ref, w1_ref, w2_ref, o_ref, acc_ref):
    o_ref[...] = _ffn_half(x_ref[...], mod_ref, j, g_ref[j], w1_ref, w2_ref, acc_ref)


def _ffn_call(x, mod_l, mod_row, g_l, w1_all, w2_all, li, which, j, tm):
    b, t, d = x.shape
    tok = lambda bi, i: (bi, i, 0)
    return pl.pallas_call(
        functools.partial(_ffn_kernel, j),
        grid=(b, t // tm),
        in_specs=[pl.BlockSpec((None, tm, d), tok),
                  pl.BlockSpec((None, N_MOD, d), lambda bi, i: (mod_row(bi), 0, 0)),
                  _const_spec(g_l.shape), _layer_spec(w1_all, li, which), _layer_spec(w2_all, li, which)],
        out_specs=pl.BlockSpec((None, tm, d), tok),
        out_shape=jax.ShapeDtypeStruct(x.shape, F32),
        scratch_shapes=[pltpu.VMEM((tm, d), F32)],
        compiler_params=_cparams(2),
        name="ffn",
    )(x, mod_l, g_l, w1_all, w2_all)


def _rope_tables(s):
    nf = HEAD_DIM // 4
    rows_n = s // GRID_W
    row = jnp.repeat(jnp.arange(rows_n), GRID_W).astype(F32)
    col = jnp.tile(jnp.arange(GRID_W), rows_n).astype(F32)
    freqs = ROPE_BASE ** (-jnp.arange(nf, dtype=F32) / nf)
    ang_r = row[:, None] * freqs
    ang_c = col[:, None] * freqs
    cos = jnp.concatenate([jnp.cos(ang_r)] * 2 + [jnp.cos(ang_c)] * 2, axis=1)
    sin = jnp.concatenate([-jnp.sin(ang_r), jnp.sin(ang_r), -jnp.sin(ang_c), jnp.sin(ang_c)], axis=1)
    return cos, sin


def kernel(x, c, ctx, c_ctx, w_ada, b_ada, g_norm, w_ffn_in, w_ffn_out, w_in, w_out, w_dw, b_dw,
           conv_norm_g, conv_norm_b, w_gla_gate, b_gla_gate, gla_norm_g, q_norm_g, k_norm_g):
    b, s, d = x.shape
    l = ctx.shape[1]
    depth = w_ada.shape[0]
    d_ff = w_ffn_out.shape[2]
    cw = w_dw.shape[2]
    dkw = w_gla_gate.shape[3]
    dvw = gla_norm_g.shape[1] * gla_norm_g.shape[2]
    n_kv = N_KV_HEADS * HEAD_DIM
    aw = d - cw - dvw
    tm_c = min(TOKEN_TILE, b * l)
    assert d_ff % FFN_CHUNK == 0 and s % FFN2_TILE == 0 and l % GLA_CHUNK == 0 and (b * l) % tm_c == 0
    assert dkw == LANES and n_kv == LANES and cw == 2 * LANES and dvw == 2 * LANES and aw == 4 * LANES

    rows = -(-(b + 1) // SUBLANES) * SUBLANES
    cc = jnp.zeros((rows, d), F32).at[:b].set(c).at[b].set(c_ctx)
    mod = jnp.transpose(_mod_call(cc, w_ada, b_ada), (0, 2, 1, 3))
    row_x = lambda bi: bi
    row_c = lambda bi: b

    cos64, sin64 = _rope_tables(s)
    cos_k, sin_k = jnp.tile(cos64, (1, 2)), jnp.tile(sin64, (1, 2))
    one_k, zero_k = jnp.ones((l, LANES), F32), jnp.zeros((l, LANES), F32)

    gm64_128 = _group_mean_matrix(LANES, HEAD_DIM)
    rope_perm = _rope_partner_matrix(LANES)
    gm_conv = _group_mean_matrix(cw, cw // CONV_GROUPS)
    gm_gla = _group_mean_matrix(dvw, dvw // GLA_HEADS)

    sizes = (cw, cw, dkw, dkw, dvw, dvw, GLA_RANK, GLA_RANK, aw, n_kv, n_kv)
    offs = np.concatenate([[0], np.cumsum(sizes)])

    w1_all = w_ffn_in.astype(BF16)
    w2_all = w_ffn_out.astype(BF16)
    wo_all = w_out.astype(BF16)
    wi = w_in.astype(BF16)
    wc_all = wi[:, :, offs[0]:offs[2]]
    wg_all = jnp.concatenate([wi[:, :, offs[2]:offs[8]],
                              jnp.zeros((depth, d, LANES - 2 * GLA_RANK), BF16)], axis=2)
    wa_all = wi[:, :, offs[8]:offs[11]]
    wg2_all = jnp.zeros((depth, 2, LANES, dkw), F32)
    wg2_all = wg2_all.at[:, 0, 0:GLA_RANK].set(w_gla_gate[:, 0])
    wg2_all = wg2_all.at[:, 1, GLA_RANK:2 * GLA_RANK].set(w_gla_gate[:, 1])
    weights = (w1_all, w2_all, wc_all, wg_all, wa_all)
    flat = lambda a: a.reshape(1, b * l, a.shape[2])

    for i in range(depth):
        last = i == depth - 1
        mod_l = mod[i]
        g_l = g_norm[i].reshape(3, 1, d)
        wg2 = wg2_all[i]
        bg2 = b_gla_gate[i].reshape(2, 1, dkw)
        qg = jnp.tile(q_norm_g[i], 2).reshape(1, LANES)
        kg = jnp.tile(k_norm_g[i], 2).reshape(1, LANES)
        gg = gla_norm_g[i].reshape(1, dvw)
        wdw, bdw = w_dw[i], b_dw[i].reshape(1, cw)
        cng, cnb = conv_norm_g[i].reshape(1, cw), conv_norm_b[i].reshape(1, cw)

        x, pc_x, pg_x, pa_x = _ffn_in_call(x, mod_l, row_x, g_l, *weights, i, TOKEN_TILE)
        ctx_f, pc_c, pg_c, pa_c = _ffn_in_call(flat(ctx), mod_l, row_c, g_l, *weights, i, tm_c)
        ctx = ctx_f.reshape(b, l, d)
        pc_c, pg_c, pa_c = (p.reshape(b, l, p.shape[2]) for p in (pc_c, pg_c, pa_c))
        conv_x = _conv_call(pc_x, wdw, bdw, cng, cnb, gm_conv)
        of_c, ob_c, s_c = _gla_call(pg_c, wg2, bg2, jnp.zeros((b, 2, dvw, dkw), F32), dkw, dvw)
        of_x, ob_x, _ = _gla_call(pg_x, wg2, bg2, s_c, dkw, dvw)
        q_c, k_c, v_c = _kvprep_call(pa_c, one_k, zero_k, qg, kg, gm64_128, rope_perm, min(TOKEN_TILE, l))
        q_x, k_x, v_x = _kvprep_call(pa_x, cos_k, sin_k, qg, kg, gm64_128, rope_perm, FFN2_TILE)
        att_x = _flash_call(q_x, [(k_c, v_c), (k_x, v_x)])
        x = _out_call(x, mod_l, row_x, conv_x, of_x, ob_x, pg_x, att_x, gg, gm_gla, wo_all, i, TOKEN_TILE)
        x = _ffn_call(x, mod_l, row_x, g_l, w1_all, w2_all, i, 1, 2, FFN2_TILE)
        if not last:
            conv_c = _conv_call(pc_c, wdw, bdw, cng, cnb, gm_conv)
            att_c = _flash_call(q_c, [(k_c, v_c)])
            ctx_f = _out_call(flat(ctx), mod_l, row_c, flat(conv_c), flat(of_c), flat(ob_c),
                              flat(pg_c), flat(att_c), gg, gm_gla, wo_all, i, tm_c)
            ctx = _ffn_call(ctx_f, mod_l, row_c, g_l, w1_all, w2_all, i, 1, 2, tm_c).reshape(b, l, d)
    return x
```

```python
import functools

import numpy as np
import jax
import jax.numpy as jnp
from jax import lax
from jax.experimental import pallas as pl
from jax.experimental.pallas import tpu as pltpu

F32 = jnp.float32
BF16 = jnp.bfloat16

CONV_GROUPS = 4
CONV_K = 31
GLA_HEADS = 4
GLA_RANK = 16
GLA_GATE_NORM = 16.0
GLA_CHUNK = 64
HEAD_DIM = 64
N_KV_HEADS = 2
GRID_W = 64
ROPE_BASE = 10000.0
N_MOD = 9
EPS = 1e-6
LOG2E = 1.4426950408889634

LANES = 128
SUBLANES = 8
VMEM_LIMIT_BYTES = 56 * 1024 * 1024

TOKEN_TILE = 512
FFN2_TILE = 1024
FFN_CHUNK = 256
CONV_ROWS = 128
CONV_HALO = 16
CONV_NORM_ROWS = 1024
GLA_BLOCK = 1024
VT_ROWS = 80
ATT_TQ = 256
ATT_TK = 1024
ATT_LOOKAHEAD = 3
ATT_MAX_RISE = 60.0


def _cparams(n_axes):
    return pltpu.CompilerParams(dimension_semantics=("arbitrary",) * n_axes,
                                vmem_limit_bytes=VMEM_LIMIT_BYTES)


def _const_spec(shape):
    nd = len(shape)
    return pl.BlockSpec(shape, lambda *_: (0,) * nd, pipeline_mode=pl.Buffered(1))


def _layer_spec(arr, *idx):
    lead = len(idx)
    blk = (None,) * lead + tuple(arr.shape[lead:])
    tail = (0,) * (arr.ndim - lead)
    return pl.BlockSpec(blk, lambda *_: tuple(idx) + tail, pipeline_mode=pl.Buffered(1))


def _dot(a, b):
    return jnp.dot(a, b, preferred_element_type=F32)


def _dot_nt(a, b):
    return lax.dot_general(a, b, (((1,), (1,)), ((), ())), preferred_element_type=F32)


def _dot_tn(a, b):
    return lax.dot_general(a, b, (((0,), (0,)), ((), ())), preferred_element_type=F32)


def _sigmoid(x):
    return 1.0 / (1.0 + jnp.exp(-x))


def _silu(x):
    return x * _sigmoid(x)


def _split_dot(v, m):
    hi = v.astype(BF16)
    lo = (v - hi.astype(F32)).astype(BF16)
    return _dot(hi, m) + _dot(lo, m)


def _split3_dot_left(m, v):
    hi = v.astype(BF16)
    r1 = v - hi.astype(F32)
    mid = r1.astype(BF16)
    lo = (r1 - mid.astype(F32)).astype(BF16)
    return _dot(m, hi) + _dot(m, mid) + _dot(m, lo)


def _pre(x, g, shift, scale):
    r = lax.rsqrt(jnp.mean(x * x, axis=-1, keepdims=True) + EPS)
    return (x * r * g) * (1.0 + scale) + shift


def _group_mean_matrix(width, group):
    idx = np.arange(width) // group
    return jnp.asarray((idx[:, None] == idx[None, :]).astype(np.float32) / group, dtype=BF16)


def _rope_partner_matrix(width):
    lane = np.arange(width)
    partner = np.where(lane % 32 < 16, lane + 16, lane - 16)
    return jnp.asarray((lane[:, None] == partner[None, :]).astype(np.float32), dtype=BF16)


def _mod_kernel(c_ref, w_ref, b_ref, o_ref):
    sc = _silu(c_ref[...]).astype(BF16)
    o_ref[...] = _dot(sc, w_ref[...].astype(BF16)) + b_ref[...]


def _mod_call(cc, w_ada, b_ada):
    depth, d, _ = w_ada.shape
    rows = cc.shape[0]
    return pl.pallas_call(
        _mod_kernel,
        grid=(depth, N_MOD),
        in_specs=[pl.BlockSpec((rows, d), lambda i, j: (0, 0)),
                  pl.BlockSpec((None, d, d), lambda i, j: (i, 0, j)),
                  pl.BlockSpec((None, 1, d), lambda i, j: (i, 0, j))],
        out_specs=pl.BlockSpec((None, None, rows, d), lambda i, j: (i, j, 0, 0)),
        out_shape=jax.ShapeDtypeStruct((depth, N_MOD, rows, d), F32),
        compiler_params=_cparams(2),
        name="mod",
    )(cc, w_ada, b_ada.reshape(depth, 1, N_MOD * d))


def _ffn_half(x, mod_ref, j, g, w1_ref, w2_ref, acc_ref):
    d_ff = w2_ref.shape[0]
    shift = mod_ref[3 * j:3 * j + 1, :]
    scale = mod_ref[3 * j + 1:3 * j + 2, :]
    gate = mod_ref[3 * j + 2:3 * j + 3, :]
    h = _pre(x, g, shift, scale).astype(BF16)
    for c0 in range(0, d_ff, FFN_CHUNK):
        a = _dot(h, w1_ref[:, c0:c0 + FFN_CHUNK])
        u = _dot(h, w1_ref[:, d_ff + c0:d_ff + c0 + FFN_CHUNK])
        part = _dot((_silu(a) * u).astype(BF16), w2_ref[c0:c0 + FFN_CHUNK, :])
        if c0 == 0:
            acc_ref[...] = part
        else:
            acc_ref[...] += part
    return x + (0.5 * gate) * acc_ref[...]


def _ffn_in_kernel(x_ref, mod_ref, g_ref, w1_ref, w2_ref, wc_ref, wg_ref, wa_ref,
                   xo_ref, pc_ref, pg_ref, pa_ref, acc_ref):
    x1 = _ffn_half(x_ref[...], mod_ref, 0, g_ref[0], w1_ref, w2_ref, acc_ref)
    xo_ref[...] = x1
    h = _pre(x1, g_ref[1], mod_ref[3:4, :], mod_ref[4:5, :]).astype(BF16)
    pc_ref[...] = _dot(h, wc_ref[...])
    pg_ref[...] = _dot(h, wg_ref[...])
    pa_ref[...] = _dot(h, wa_ref[...])


def _ffn_in_call(x, mod_l, mod_row, g_l, w1_all, w2_all, wc_all, wg_all, wa_all, li, tm):
    b, t, d = x.shape
    tok = lambda bi, i: (bi, i, 0)
    widths = (wc_all.shape[2], wg_all.shape[2], wa_all.shape[2])
    return pl.pallas_call(
        _ffn_in_kernel,
        grid=(b, t // tm),
        in_specs=[pl.BlockSpec((None, tm, d), tok),
                  pl.BlockSpec((None, N_MOD, d), lambda bi, i: (mod_row(bi), 0, 0)),
                  _const_spec(g_l.shape), _layer_spec(w1_all, li, 0), _layer_spec(w2_all, li, 0),
                  _layer_spec(wc_all, li), _layer_spec(wg_all, li), _layer_spec(wa_all, li)],
        out_specs=[pl.BlockSpec((None, tm, d), tok)] + [pl.BlockSpec((None, tm, w), tok) for w in widths],
        out_shape=[jax.ShapeDtypeStruct(x.shape, F32)]
        + [jax.ShapeDtypeStruct((b, t, w), F32) for w in widths],
        scratch_shapes=[pltpu.VMEM((tm, d), F32)],
        compiler_params=_cparams(2),
        name="ffn_in",
    )(x, mod_l, g_l, w1_all, w2_all, wc_all, wg_all, wa_all)


def _conv_kernel(t, cw, pc_ref, wdw_ref, bdw_ref, cg_ref, cb_ref, gm_ref, o_ref, hbuf, h2_scr):
    r = CONV_ROWS
    halo = CONV_HALO
    hbuf[0:halo, :] = jnp.zeros((halo, cw), F32)
    hbuf[t + halo:t + 2 * halo, :] = jnp.zeros((halo, cw), F32)

    def glu(i, carry):
        r0 = pl.multiple_of(i * r, r)
        a = pc_ref[pl.ds(r0, r), 0:cw]
        gt = pc_ref[pl.ds(r0, r), cw:2 * cw]
        hbuf[pl.ds(r0 + halo, r), :] = a * _sigmoid(gt)
        return carry

    lax.fori_loop(0, t // r, glu, 0)
    gm = gm_ref[...]

    def tile(i, carry):
        r0 = pl.multiple_of(i * r, r)
        win = hbuf[pl.ds(r0, r + 2 * halo), :]
        acc = jnp.zeros((r, cw), F32)
        for res in range(SUBLANES):
            wr = win if res == 0 else pltpu.roll(win, r + 2 * halo - res, axis=0)
            for a in range(2 * halo // SUBLANES):
                s = SUBLANES * a + res
                if 1 <= s <= CONV_K:
                    acc = acc + wr[SUBLANES * a:SUBLANES * a + r, :] * wdw_ref[s - 1:s, :]
        h2_scr[pl.ds(r0, r), :] = acc + bdw_ref[...]
        return carry

    lax.fori_loop(0, t // r, tile, 0)

    nb = min(CONV_NORM_ROWS, t)
    blocks = [slice(r0, r0 + nb) for r0 in range(0, t, nb)]
    h2 = [h2_scr[rows, :] for rows in blocks]
    mu = [_split_dot(v, gm) for v in h2]
    dlt = [v - m for v, m in zip(h2, mu)]
    var = [_split_dot(dv * dv, gm) for dv in dlt]
    for rows, dv, vr in zip(blocks, dlt, var):
        y = dv * lax.rsqrt(vr + EPS) * cg_ref[...] + cb_ref[...]
        o_ref[rows, :] = _silu(y).astype(o_ref.dtype)


def _conv_call(pc, w_dw, b_dw, cn_g, cn_b, gm):
    b, t, two_cw = pc.shape
    cw = two_cw // 2
    return pl.pallas_call(
        functools.partial(_conv_kernel, t, cw),
        grid=(b,),
        in_specs=[pl.BlockSpec((None, t, two_cw), lambda bi: (bi, 0, 0)),
                  _const_spec(w_dw.shape), _const_spec(b_dw.shape), _const_spec(cn_g.shape),
                  _const_spec(cn_b.shape), _const_spec(gm.shape)],
        out_specs=pl.BlockSpec((None, t, cw), lambda bi: (bi, 0, 0)),
        out_shape=jax.ShapeDtypeStruct((b, t, cw), BF16),
        scratch_shapes=[pltpu.VMEM((t + 2 * CONV_HALO, cw), F32), pltpu.VMEM((t, cw), F32)],
        compiler_params=_cparams(1),
        name="conv",
    )(pc, w_dw, b_dw, cn_g, cn_b, gm)


def _gla_blocks(n_ch, dkw, dvw, pgs, wg_ref, bg_ref, outs, s_scr):
    c = GLA_CHUNK
    dk = dkw // GLA_HEADS
    dv = dvw // GLA_HEADS
    lr_off = 2 * dkw + 2 * dvw
    ri = lax.broadcasted_iota(jnp.int32, (c, c), 0)
    ci = lax.broadcasted_iota(jnp.int32, (c, c), 1)
    rih = lax.broadcasted_iota(jnp.int32, (c, GLA_HEADS * c), 0)
    cih = lax.broadcasted_iota(jnp.int32, (c, GLA_HEADS * c), 1) % c
    tri_m = [jnp.where(ci <= ri, 1.0, 0.0).astype(BF16), jnp.where(ci >= ri, 1.0, 0.0).astype(BF16)]
    tri_h = [cih <= rih, cih >= rih]
    klane = lax.broadcasted_iota(jnp.int32, (c, dkw), 1) // dk
    vlane = lax.broadcasted_iota(jnp.int32, (c, dvw), 1) // dv
    srow = lax.broadcasted_iota(jnp.int32, (dvw, dkw), 0) // dv
    scol = lax.broadcasted_iota(jnp.int32, (dvw, dkw), 1) // dk
    smask = srow == scol

    units = [(d, j) for d in range(2) for j in range(n_ch)]
    rows = [slice(j * c, (j + 1) * c) for j in range(n_ch)]
    la = []
    for d in range(2):
        z = _dot(pgs[d][:, lr_off:lr_off + LANES], wg_ref[d]) + bg_ref[d]
        la.append(-(jnp.maximum(-z, 0.0) + jnp.log(1.0 + jnp.exp(-jnp.abs(z)))) * (1.0 / GLA_GATE_NORM))
    bcum = {u: _split3_dot_left(tri_m[u[0]], la[u[0]][rows[u[1]], :]) for u in units}
    q_e, k_end, kblk, vblk, vb, decay = {}, {}, {}, {}, {}, {}
    for u in units:
        d, j = u
        q = pgs[d][rows[j], 0:dkw] * (dk ** -0.5)
        k = pgs[d][rows[j], dkw:2 * dkw]
        v = pgs[d][rows[j], 2 * dkw:2 * dkw + dvw]
        btot = bcum[u][c - 1:c, :] if d == 0 else bcum[u][0:1, :]
        q_e[u] = (q * jnp.exp(bcum[u])).astype(BF16)
        k_e = k * jnp.exp(-bcum[u])
        k_end[u] = (k * jnp.exp(btot - bcum[u])).astype(BF16)
        kblk[u] = jnp.concatenate([jnp.where(klane == h, k_e, 0.0) for h in range(GLA_HEADS)],
                                  axis=0).astype(BF16)
        vblk[u] = jnp.concatenate([jnp.where(vlane == h, v, 0.0) for h in range(GLA_HEADS)],
                                  axis=0).astype(BF16)
        vb[u] = v.astype(BF16)
        decay[u] = jnp.exp(btot)
    att = {u: jnp.where(tri_h[u[0]], _dot_nt(q_e[u], kblk[u]), 0.0).astype(BF16) for u in units}
    o_intra = {u: _dot(att[u], vblk[u]) for u in units}
    upd = {u: jnp.where(smask, _dot_tn(vb[u], k_end[u]), 0.0) for u in units}
    s_start = {}
    for d in range(2):
        s_t = s_scr[d]
        for j in (range(n_ch) if d == 0 else reversed(range(n_ch))):
            s_start[(d, j)] = s_t.astype(BF16)
            s_t = s_t * decay[(d, j)] + upd[(d, j)]
        s_scr[d] = s_t
    for u in units:
        outs[u[0]][rows[u[1]], :] = (o_intra[u] + _dot_nt(q_e[u], s_start[u])).astype(outs[u[0]].dtype)


def _gla_kernel(n_ch, dkw, dvw, pgf_ref, pgb_ref, wg_ref, bg_ref, s0_ref, of_ref, ob_ref, sfin_ref, s_scr):
    i = pl.program_id(1)

    @pl.when(i == 0)
    def _():
        s_scr[...] = s0_ref[...]

    _gla_blocks(n_ch, dkw, dvw, (pgf_ref, pgb_ref), wg_ref, bg_ref, (of_ref, ob_ref), s_scr)

    @pl.when(i == pl.num_programs(1) - 1)
    def _():
        sfin_ref[...] = s_scr[...]


def _gla_call(pg, wg2, bg2, s0, dkw, dvw):
    b, t, w = pg.shape
    blk = min(GLA_BLOCK, t)
    nb = t // blk
    n_ch = blk // GLA_CHUNK
    return pl.pallas_call(
        functools.partial(_gla_kernel, n_ch, dkw, dvw),
        grid=(b, nb),
        in_specs=[pl.BlockSpec((None, blk, w), lambda bi, i: (bi, i, 0)),
                  pl.BlockSpec((None, blk, w), lambda bi, i: (bi, nb - 1 - i, 0)),
                  _const_spec(wg2.shape), _const_spec(bg2.shape),
                  pl.BlockSpec((None, 2, dvw, dkw), lambda bi, i: (bi, 0, 0, 0))],
        out_specs=[pl.BlockSpec((None, blk, dvw), lambda bi, i: (bi, i, 0)),
                   pl.BlockSpec((None, blk, dvw), lambda bi, i: (bi, nb - 1 - i, 0)),
                   pl.BlockSpec((None, 2, dvw, dkw), lambda bi, i: (bi, 0, 0, 0))],
        out_shape=[jax.ShapeDtypeStruct((b, t, dvw), BF16), jax.ShapeDtypeStruct((b, t, dvw), BF16),
                   jax.ShapeDtypeStruct((b, 2, dvw, dkw), F32)],
        scratch_shapes=[pltpu.VMEM((2, dvw, dkw), F32)],
        compiler_params=_cparams(2),
        name="gla",
    )(pg, pg, wg2, bg2, s0)


def _kvprep_kernel(aw, pa_ref, cos_ref, sin_ref, qg_ref, kg_ref, gm_ref, perm_ref, q_ref, k_ref, vt_ref):
    hd = HEAD_DIM
    cos = cos_ref[...]
    sin = sin_ref[...]

    n_q = aw // LANES
    xs = [pa_ref[:, t * LANES:(t + 1) * LANES] for t in range(n_q + 1)]
    gains = [qg_ref[...]] * n_q + [kg_ref[...]]
    ms = [_dot((x * x).astype(BF16), gm_ref[...]) for x in xs]
    xn = [x * lax.rsqrt(m + EPS) * g for x, m, g in zip(xs, ms, gains)]
    partner = [_split_dot(v, perm_ref[...]) for v in xn]
    rot = [v * cos + pt * sin for v, pt in zip(xn, partner)]
    for t in range(n_q):
        q_ref[:, t * LANES:(t + 1) * LANES] = (rot[t] * (hd ** -0.5 * LOG2E)).astype(BF16)
    vv = pa_ref[:, aw + LANES:aw + 2 * LANES]
    kr = rot[n_q]
    lane = lax.broadcasted_iota(jnp.int32, kr.shape, 1)
    lo = lane < hd
    kswap = pltpu.roll(kr, hd, axis=1)
    k0 = jnp.where(lo, kr, kswap).astype(BF16)
    k1 = jnp.where(lo, kswap, kr).astype(BF16)
    k_ref[...] = jnp.concatenate([k0, k0, k1, k1], axis=1)
    vt_ref[0] = jnp.where(lo, vv, 1.0).T[0:VT_ROWS, :].astype(BF16)
    vt_ref[1] = jnp.where(lo, pltpu.roll(vv, hd, axis=1), 1.0).T[0:VT_ROWS, :].astype(BF16)


def _kvprep_call(pa, cos_t, sin_t, qg, kg, gm, perm, tt):
    b, t, w = pa.shape
    aw = w - 2 * LANES
    return pl.pallas_call(
        functools.partial(_kvprep_kernel, aw),
        grid=(b, t // tt),
        in_specs=[pl.BlockSpec((None, tt, w), lambda bi, i: (bi, i, 0)),
                  pl.BlockSpec((tt, LANES), lambda bi, i: (i, 0)),
                  pl.BlockSpec((tt, LANES), lambda bi, i: (i, 0)),
                  _const_spec(qg.shape), _const_spec(kg.shape), _const_spec(gm.shape),
                  _const_spec(perm.shape)],
        out_specs=[pl.BlockSpec((None, tt, aw), lambda bi, i: (bi, i, 0)),
                   pl.BlockSpec((None, tt, 4 * LANES), lambda bi, i: (bi, i, 0)),
                   pl.BlockSpec((None, N_KV_HEADS, VT_ROWS, tt), lambda bi, i: (bi, 0, 0, i))],
        out_shape=[jax.ShapeDtypeStruct((b, t, aw), BF16),
                   jax.ShapeDtypeStruct((b, t, 4 * LANES), BF16),
                   jax.ShapeDtypeStruct((b, N_KV_HEADS, VT_ROWS, t), BF16)],
        compiler_params=_cparams(2),
        name="kvprep",
    )(pa, cos_t, sin_t, qg, kg, gm, perm)


def _flash_kernel(seg_lens, q_ref, *rest):
    n_seg = len(seg_lens)
    kv_refs = rest[:2 * n_seg]
    o_ref, qs_scr = rest[2 * n_seg:]
    hd = HEAD_DIM
    kvw = 2 * LANES
    g_heads = kvw // hd
    n_heads = N_KV_HEADS * g_heads
    lane_head = lax.broadcasted_iota(jnp.int32, (q_ref.shape[0], kvw), 1) // hd
    for h in range(n_heads):
        j, g = divmod(h, g_heads)
        qj = q_ref[:, j * kvw:(j + 1) * kvw]
        qs_scr[h] = jnp.where(lane_head == g, qj, jnp.zeros_like(qj))

    tasks = []
    for h in range(n_heads):
        first = True
        for si in range(n_seg):
            k_ref, vt_ref = kv_refs[2 * si], kv_refs[2 * si + 1]
            tk = min(ATT_TK, seg_lens[si])
            for c0 in range(0, seg_lens[si], tk):
                tasks.append((h, k_ref, vt_ref, c0, tk, first))
                first = False

    def scores(task):
        h, k_ref, _, c0, tk, _ = task
        j = h // g_heads
        return _dot_nt(k_ref[c0:c0 + tk, j * kvw:(j + 1) * kvw], qs_scr[h])

    def attend(stale_max):
        m_made = [None] * n_heads
        base = [None] * n_heads
        acc = [None] * n_heads
        rise = [None]

        def make(task):
            h, _, _, _, _, first = task
            s = scores(task)
            m_cur = jnp.max(s, axis=0, keepdims=True)
            if first:
                ref = m_cur
            elif stale_max:
                ref = m_made[h]
                rise[0] = m_cur - ref if rise[0] is None else jnp.maximum(rise[0], m_cur - ref)
            else:
                ref = jnp.maximum(m_made[h], m_cur)
            m_made[h] = m_cur if first else jnp.maximum(m_made[h], m_cur)
            return jnp.exp2(s - ref).astype(BF16), ref

        pending = [make(t) for t in tasks[:ATT_LOOKAHEAD]]
        for n, (h, _, vt_ref, c0, tk, first) in enumerate(tasks):
            if n + ATT_LOOKAHEAD < len(tasks):
                pending.append(make(tasks[n + ATT_LOOKAHEAD]))
            p, ref = pending.pop(0)
            pv = _dot(vt_ref[h // g_heads, :, c0:c0 + tk], p)
            acc[h] = pv if first else acc[h] * jnp.exp2(base[h] - ref) + pv
            base[h] = ref
        o_t = jnp.concatenate([acc[h][0:hd, :] * (1.0 / acc[h][hd:hd + 1, :]) for h in range(n_heads)],
                              axis=0)
        return o_t.T.astype(o_ref.dtype), rise[0]

    o_fast, rise = attend(True)
    o_ref[...] = o_fast
    if rise is not None:
        @pl.when(jnp.max(rise) > ATT_MAX_RISE)
        def _():
            o_ref[...] = attend(False)[0]


def _flash_call(qr, segs):
    b, t, _ = qr.shape
    tq = min(ATT_TQ, t)
    seg_lens = tuple(k.shape[1] for k, _ in segs)
    w = qr.shape[2]
    in_specs = [pl.BlockSpec((None, tq, w), lambda bi, i: (bi, i, 0))]
    args = [qr]
    for k, v in segs:
        in_specs.append(pl.BlockSpec((None, k.shape[1], k.shape[2]), lambda bi, i: (bi, 0, 0)))
        in_specs.append(pl.BlockSpec((None, N_KV_HEADS, VT_ROWS, v.shape[3]), lambda bi, i: (bi, 0, 0, 0)))
        args += [k, v]
    return pl.pallas_call(
        functools.partial(_flash_kernel, seg_lens),
        grid=(b, t // tq),
        in_specs=in_specs,
        out_specs=pl.BlockSpec((None, tq, w), lambda bi, i: (bi, i, 0)),
        out_shape=jax.ShapeDtypeStruct((b, t, w), BF16),
        scratch_shapes=[pltpu.VMEM((w // HEAD_DIM, tq, 2 * LANES), BF16)],
        compiler_params=_cparams(2),
        name="flash",
    )(*args)


def _out_kernel(cw, dvw, x_ref, mod_ref, conv_ref, of_ref, ob_ref, r_ref, att_ref, gg_ref, gm_ref,
                wo_ref, o_ref):
    o = of_ref[...].astype(F32) + ob_ref[...].astype(F32)
    ms = _split_dot(o * o, gm_ref[...])
    gl = (o * lax.rsqrt(ms + EPS) * gg_ref[...]) * _silu(r_ref[...])
    y = (_dot(conv_ref[...], wo_ref[0:cw, :])
         + _dot(gl.astype(BF16), wo_ref[cw:cw + dvw, :])
         + _dot(att_ref[...], wo_ref[cw + dvw:, :]))
    o_ref[...] = x_ref[...] + mod_ref[5:6, :] * y


def _out_call(x, mod_l, mod_row, conv, o_f, o_b, pg, att, gg, gm, wo_all, li, tm):
    b, t, d = x.shape
    cw = conv.shape[2]
    dvw = o_f.shape[2]
    aw = att.shape[2]
    tok = lambda bi, i: (bi, i, 0)
    return pl.pallas_call(
        functools.partial(_out_kernel, cw, dvw),
        grid=(b, t // tm),
        in_specs=[pl.BlockSpec((None, tm, d), tok),
                  pl.BlockSpec((None, N_MOD, d), lambda bi, i: (mod_row(bi), 0, 0)),
                  pl.BlockSpec((None, tm, cw), tok),
                  pl.BlockSpec((None, tm, dvw), tok),
                  pl.BlockSpec((None, tm, dvw), tok),
                  pl.BlockSpec((None, tm, dvw), lambda bi, i: (bi, i, 2)),
                  pl.BlockSpec((None, tm, aw), tok),
                  _const_spec(gg.shape), _const_spec(gm.shape), _layer_spec(wo_all, li)],
        out_specs=pl.BlockSpec((None, tm, d), tok),
        out_shape=jax.ShapeDtypeStruct(x.shape, F32),
        compiler_params=_cparams(2),
        name="outproj",
    )(x, mod_l, conv, o_f, o_b, pg, att, gg, gm, wo_all)


def _ffn_kernel(j, x_ref, mod_ref, g_ref, w1_ref, w2_ref, o_ref, acc_ref):
    o_ref[...] = _ffn_half(x_ref[...], mod_ref, j, g_ref[j], w1_ref, w2_ref, acc_ref)


def _ffn_call(x, mod_l, mod_row, g_l, w1_all, w2_all, li, which, j, tm):
    b, t, d = x.shape
    tok = lambda bi, i: (bi, i, 0)
    return pl.pallas_call(
        functools.partial(_ffn_kernel, j),
        grid=(b, t // tm),
        in_specs=[pl.BlockSpec((None, tm, d), tok),
                  pl.BlockSpec((None, N_MOD, d), lambda bi, i: (mod_row(bi), 0, 0)),
                  _const_spec(g_l.shape), _layer_spec(w1_all, li, which), _layer_spec(w2_all, li, which)],
        out_specs=pl.BlockSpec((None, tm, d), tok),
        out_shape=jax.ShapeDtypeStruct(x.shape, F32),
        scratch_shapes=[pltpu.VMEM((tm, d), F32)],
        compiler_params=_cparams(2),
        name="ffn",
    )(x, mod_l, g_l, w1_all, w2_all)


def _rope_tables(s):
    nf = HEAD_DIM // 4
    rows_n = s // GRID_W
    row = jnp.repeat(jnp.arange(rows_n), GRID_W).astype(F32)
    col = jnp.tile(jnp.arange(GRID_W), rows_n).astype(F32)
    freqs = ROPE_BASE ** (-jnp.arange(nf, dtype=F32) / nf)
    ang_r = row[:, None] * freqs
    ang_c = col[:, None] * freqs
    cos = jnp.concatenate([jnp.cos(ang_r)] * 2 + [jnp.cos(ang_c)] * 2, axis=1)
    sin = jnp.concatenate([-jnp.sin(ang_r), jnp.sin(ang_r), -jnp.sin(ang_c), jnp.sin(ang_c)], axis=1)
    return cos, sin


def kernel(x, c, ctx, c_ctx, w_ada, b_ada, g_norm, w_ffn_in, w_ffn_out, w_in, w_out, w_dw, b_dw,
           conv_norm_g, conv_norm_b, w_gla_gate, b_gla_gate, gla_norm_g, q_norm_g, k_norm_g):
    b, s, d = x.shape
    l = ctx.shape[1]
    depth = w_ada.shape[0]
    d_ff = w_ffn_out.shape[2]
    cw = w_dw.shape[2]
    dkw = w_gla_gate.shape[3]
    dvw = gla_norm_g.shape[1] * gla_norm_g.shape[2]
    n_kv = N_KV_HEADS * HEAD_DIM
    aw = d - cw - dvw
    tm_c = min(TOKEN_TILE, b * l)
    assert d_ff % FFN_CHUNK == 0 and s % FFN2_TILE == 0 and l % GLA_CHUNK == 0 and (b * l) % tm_c == 0
    assert dkw == LANES and n_kv == LANES and cw == 2 * LANES and dvw == 2 * LANES and aw == 4 * LANES

    rows = -(-(b + 1) // SUBLANES) * SUBLANES
    cc = jnp.zeros((rows, d), F32).at[:b].set(c).at[b].set(c_ctx)
    mod = jnp.transpose(_mod_call(cc, w_ada, b_ada), (0, 2, 1, 3))
    row_x = lambda bi: bi
    row_c = lambda bi: b

    cos64, sin64 = _rope_tables(s)
    cos_k, sin_k = jnp.tile(cos64, (1, 2)), jnp.tile(sin64, (1, 2))
    one_k, zero_k = jnp.ones((l, LANES), F32), jnp.zeros((l, LANES), F32)

    gm64_128 = _group_mean_matrix(LANES, HEAD_DIM)
    rope_perm = _rope_partner_matrix(LANES)
    gm_conv = _group_mean_matrix(cw, cw // CONV_GROUPS)
    gm_gla = _group_mean_matrix(dvw, dvw // GLA_HEADS)

    sizes = (cw, cw, dkw, dkw, dvw, dvw, GLA_RANK, GLA_RANK, aw, n_kv, n_kv)
    offs = np.concatenate([[0], np.cumsum(sizes)])

    w1_all = w_ffn_in.astype(BF16)
    w2_all = w_ffn_out.astype(BF16)
    wo_all = w_out.astype(BF16)
    wi = w_in.astype(BF16)
    wc_all = wi[:, :, offs[0]:offs[2]]
    wg_all = jnp.concatenate([wi[:, :, offs[2]:offs[8]],
                              jnp.zeros((depth, d, LANES - 2 * GLA_RANK), BF16)], axis=2)
    wa_all = wi[:, :, offs[8]:offs[11]]
    wg2_all = jnp.zeros((depth, 2, LANES, dkw), F32)
    wg2_all = wg2_all.at[:, 0, 0:GLA_RANK].set(w_gla_gate[:, 0])
    wg2_all = wg2_all.at[:, 1, GLA_RANK:2 * GLA_RANK].set(w_gla_gate[:, 1])
    weights = (w1_all, w2_all, wc_all, wg_all, wa_all)
    flat = lambda a: a.reshape(1, b * l, a.shape[2])

    for i in range(depth):
        last = i == depth - 1
        mod_l = mod[i]
        g_l = g_norm[i].reshape(3, 1, d)
        wg2 = wg2_all[i]
        bg2 = b_gla_gate[i].reshape(2, 1, dkw)
        qg = jnp.tile(q_norm_g[i], 2).reshape(1, LANES)
        kg = jnp.tile(k_norm_g[i], 2).reshape(1, LANES)
        gg = gla_norm_g[i].reshape(1, dvw)
        wdw, bdw = w_dw[i], b_dw[i].reshape(1, cw)
        cng, cnb = conv_norm_g[i].reshape(1, cw), conv_norm_b[i].reshape(1, cw)

        x, pc_x, pg_x, pa_x = _ffn_in_call(x, mod_l, row_x, g_l, *weights, i, TOKEN_TILE)
        ctx_f, pc_c, pg_c, pa_c = _ffn_in_call(flat(ctx), mod_l, row_c, g_l, *weights, i, tm_c)
        ctx = ctx_f.reshape(b, l, d)
        pc_c, pg_c, pa_c = (p.reshape(b, l, p.shape[2]) for p in (pc_c, pg_c, pa_c))
        conv_x = _conv_call(pc_x, wdw, bdw, cng, cnb, gm_conv)
        of_c, ob_c, s_c = _gla_call(pg_c, wg2, bg2, jnp.zeros((b, 2, dvw, dkw), F32), dkw, dvw)
        of_x, ob_x, _ = _gla_call(pg_x, wg2, bg2, s_c, dkw, dvw)
        q_c, k_c, v_c = _kvprep_call(pa_c, one_k, zero_k, qg, kg, gm64_128, rope_perm, min(TOKEN_TILE, l))
        q_x, k_x, v_x = _kvprep_call(pa_x, cos_k, sin_k, qg, kg, gm64_128, rope_perm, FFN2_TILE)
        att_x = _flash_call(q_x, [(k_c, v_c), (k_x, v_x)])
        x = _out_call(x, mod_l, row_x, conv_x, of_x, ob_x, pg_x, att_x, gg, gm_gla, wo_all, i, FFN2_TILE)
        x = _ffn_call(x, mod_l, row_x, g_l, w1_all, w2_all, i, 1, 2, FFN2_TILE)
        if not last:
            conv_c = _conv_call(pc_c, wdw, bdw, cng, cnb, gm_conv)
            att_c = _flash_call(q_c, [(k_c, v_c)])
            ctx_f = _out_call(flat(ctx), mod_l, row_c, flat(conv_c), flat(of_c), flat(ob_c),
                              flat(pg_c), flat(att_c), gg, gm_gla, wo_all, i, tm_c)
            ctx = _ffn_call(ctx_f, mod_l, row_c, g_l, w1_all, w2_all, i, 1, 2, tm_c).reshape(b, l, d)
    return x
```
